```python
import jax
import jax.numpy as jnp
from jax import lax
import numpy as np

D_MODEL = 1024
BATCH = 8
SEQ = 4096
DEPTH = 2

N_META = 16
EPS = 1e-6
GATE_CLAMP = 1.0 - 1e-6
CONV_DIM = D_MODEL // 2
CONV_K = 31
MLA_HEADS = D_MODEL // 128
Q_RANK = D_MODEL // 4
KV_RANK = D_MODEL // 8
NOPE_DIM = 64
ROPE_DIM = 32
V_DIM = 64
QK_DIM = NOPE_DIM + ROPE_DIM
ROPE_BASE = 10000.0
Q_BLOCK = 128
HGRN_HEADS = D_MODEL // 256
HGRN_DK = 128
HGRN_DV = (D_MODEL // 2) // HGRN_HEADS
HGRN_CHUNK = 64
D_FF = 4 * D_MODEL
N_BRANCH = 3

SPLIT_SIZES = (
    2 * CONV_DIM,
    Q_RANK,
    KV_RANK,
    ROPE_DIM,
    HGRN_HEADS * HGRN_DK,
    HGRN_HEADS * HGRN_DK,
    HGRN_HEADS * HGRN_DV,
    HGRN_HEADS * HGRN_DV,
    N_BRANCH * D_MODEL,
)
SPLIT_POINTS = tuple(int(s) for s in np.cumsum(SPLIT_SIZES)[:-1])
N_IN = int(sum(SPLIT_SIZES))

kernel_name = 'hybrid_conv_mla_hgrn2_block'


def rms_norm(x, g):
    xf = x.astype(jnp.float32)
    y = xf * lax.rsqrt(jnp.mean(xf * xf, axis=-1, keepdims=True) + EPS)
    return (y * g.astype(jnp.float32)).astype(x.dtype)


def layer_norm(x, g, b):
    xf = x.astype(jnp.float32)
    mu = jnp.mean(xf, axis=-1, keepdims=True)
    xc = xf - mu
    y = xc * lax.rsqrt(jnp.mean(xc * xc, axis=-1, keepdims=True) + EPS)
    return (y * g.astype(jnp.float32) + b.astype(jnp.float32)).astype(x.dtype)


def apply_rope(x, cos, sin):
    half = ROPE_DIM // 2
    xf = x.astype(jnp.float32)
    x1, x2 = xf[..., :half], xf[..., half:]
    out = jnp.concatenate([x1 * cos - x2 * sin, x1 * sin + x2 * cos], axis=-1)
    return out.astype(x.dtype)


def conv_module(u, conv_w, conv_b, ln_g, ln_b, w_proj):
    a, gt = jnp.split(u, 2, axis=-1)
    h = a * jax.nn.sigmoid(gt)
    h = lax.conv_general_dilated(
        h, conv_w[:, None, :].astype(h.dtype), window_strides=(1,), padding=[(CONV_K - 1, 0)],
        dimension_numbers=('NWC', 'WIO', 'NWC'), feature_group_count=CONV_DIM) + conv_b
    h = jax.nn.silu(layer_norm(h, ln_g, ln_b))
    return h @ w_proj


def causal_block_attention(q, k, v):
    B, L, H, Dq = q.shape
    n_blk = -(-L // Q_BLOCK)
    Lp = n_blk * Q_BLOCK
    qp = jnp.pad(q, ((0, 0), (0, Lp - L), (0, 0), (0, 0)))
    qb = qp.reshape(B, n_blk, Q_BLOCK, H, Dq).transpose(1, 0, 2, 3, 4)
    starts = jnp.arange(n_blk, dtype=jnp.int32) * Q_BLOCK
    k_pos = jnp.arange(L, dtype=jnp.int32)
    scale = Dq ** -0.5

    def one_block(args):
        q_blk, start = args
        s = jnp.einsum('bqhd,bkhd->bhqk', q_blk, k).astype(jnp.float32) * scale
        q_pos = start + jnp.arange(Q_BLOCK, dtype=jnp.int32)
        mask = k_pos[None, :] <= q_pos[:, None]
        s = jnp.where(mask, s, -1e30)
        p = jax.nn.softmax(s, axis=-1).astype(v.dtype)
        return jnp.einsum('bhqk,bkhd->bqhd', p, v)

    ob = lax.map(one_block, (qb, starts))
    return ob.transpose(1, 0, 2, 3, 4).reshape(B, Lp, H, V_DIM)[:, :L]


def mla(c_q, c_kv, k_rope, cos, sin, q_a_g, w_uq, kv_a_g, w_ukv, q_norm_g, k_norm_g, w_proj):
    B, L = c_q.shape[:2]
    q = (rms_norm(c_q, q_a_g) @ w_uq).reshape(B, L, MLA_HEADS, QK_DIM)
    kv = (rms_norm(c_kv, kv_a_g) @ w_ukv).reshape(B, L, MLA_HEADS, NOPE_DIM + V_DIM)
    k_nope, v = kv[..., :NOPE_DIM], kv[..., NOPE_DIM:]
    k_r = jnp.broadcast_to(k_rope[:, :, None, :], (B, L, MLA_HEADS, ROPE_DIM))
    k = jnp.concatenate([k_nope, k_r], axis=-1)
    q = rms_norm(q, q_norm_g)
    k = rms_norm(k, k_norm_g)
    q = jnp.concatenate([q[..., :NOPE_DIM], apply_rope(q[..., NOPE_DIM:], cos, sin)], axis=-1)
    k = jnp.concatenate([k[..., :NOPE_DIM], apply_rope(k[..., NOPE_DIM:], cos, sin)], axis=-1)
    o = causal_block_attention(q, k, v)
    return o.reshape(B, L, MLA_HEADS * V_DIM) @ w_proj


def hgrn2(q, f_raw, i, g, lb, norm_g, w_proj):
    B, L = q.shape[:2]
    f32 = jnp.float32
    lbf = lb.astype(f32)
    fr = f_raw.astype(f32)
    k = (1.0 - lbf) * jax.nn.sigmoid(-fr)
    log_f = jnp.log1p(-jnp.minimum(k, GATE_CLAMP))
    v = jax.nn.silu(i.astype(f32))
    qf = q.astype(f32)
    pad_front = (-N_META) % HGRN_CHUNK
    pad_back = (-(pad_front + L)) % HGRN_CHUNK
    pads = ((0, 0), (pad_front, pad_back), (0, 0))
    qf, k, v, log_f = [jnp.pad(a, pads) for a in (qf, k, v, log_f)]
    Lp = L + pad_front + pad_back
    nc = Lp // HGRN_CHUNK

    def to_chunks(a, d):
        return a.reshape(B, nc, HGRN_CHUNK, HGRN_HEADS, d).transpose(1, 0, 3, 2, 4)

    qc, kc, lfc = to_chunks(qf, HGRN_DK), to_chunks(k, HGRN_DK), to_chunks(log_f, HGRN_DK)
    vc = to_chunks(v, HGRN_DV)
    causal = jnp.tril(jnp.ones((HGRN_CHUNK, HGRN_CHUNK), dtype=bool))[:, :, None]

    def chunk_step(S, inp):
        q_c, k_c, v_c, lf_c = inp
        b = jnp.cumsum(lf_c, axis=2)
        o_inter = jnp.einsum('bhtk,bhkv->bhtv', q_c * jnp.exp(b), S)
        diff = b[:, :, :, None, :] - b[:, :, None, :, :]
        decay = jnp.where(causal, jnp.exp(jnp.where(causal, diff, 0.0)), 0.0)
        A = jnp.einsum('bhtk,bhtsk,bhsk->bhts', q_c, decay, k_c)
        o_intra = jnp.einsum('bhts,bhsv->bhtv', A, v_c)
        b_last = b[:, :, -1:, :]
        S_new = jnp.exp(b_last[:, :, 0, :])[..., None] * S + jnp.einsum(
            'bhsk,bhsv->bhkv', k_c * jnp.exp(b_last - b), v_c)
        return S_new, o_inter + o_intra

    S0 = jnp.zeros((B, HGRN_HEADS, HGRN_DK, HGRN_DV), f32)
    _, oc = lax.scan(chunk_step, S0, (qc, kc, vc, lfc))
    o = oc.transpose(1, 0, 3, 2, 4).reshape(B, Lp, HGRN_HEADS, HGRN_DV)[:, pad_front:pad_front + L]
    o = o * lax.rsqrt(jnp.mean(o * o, axis=-1, keepdims=True) + EPS)
    o = o.reshape(B, L, HGRN_HEADS * HGRN_DV) * norm_g.astype(f32)
    o = (o * jax.nn.silu(g.astype(f32))).astype(q.dtype)
    return o @ w_proj


def setup_inputs(seed: int = 0) -> dict:
    key = jax.random.key(seed)
    ks = iter(jax.random.split(key, 32))

    def nrm(shape, scale):
        return jax.random.normal(next(ks), shape, jnp.float32) * scale

    def gain(shape):
        return 1.0 + 0.1 * nrm(shape, 1.0)

    D = D_MODEL
    return {
        'x': nrm((BATCH, SEQ, D), 1.0),
        'meta': nrm((N_META, D), 1.0),
        'norm1_g': gain((DEPTH, D)),
        'w_in': nrm((DEPTH, D, N_IN), D ** -0.5),
        'conv_w': nrm((DEPTH, CONV_K, CONV_DIM), CONV_K ** -0.5),
        'conv_b': nrm((DEPTH, CONV_DIM), 0.02),
        'conv_ln_g': gain((DEPTH, CONV_DIM)),
        'conv_ln_b': nrm((DEPTH, CONV_DIM), 0.02),
        'w_conv_out': nrm((DEPTH, CONV_DIM, D), CONV_DIM ** -0.5),
        'q_a_norm_g': gain((DEPTH, Q_RANK)),
        'w_uq': nrm((DEPTH, Q_RANK, MLA_HEADS * QK_DIM), Q_RANK ** -0.5),
        'kv_a_norm_g': gain((DEPTH, KV_RANK)),
        'w_ukv': nrm((DEPTH, KV_RANK, MLA_HEADS * (NOPE_DIM + V_DIM)), KV_RANK ** -0.5),
        'q_norm_g': gain((DEPTH, QK_DIM)),
        'k_norm_g': gain((DEPTH, QK_DIM)),
        'w_attn_out': nrm((DEPTH, MLA_HEADS * V_DIM, D), (MLA_HEADS * V_DIM) ** -0.5),
        'hgrn_lb_logits': nrm((DEPTH, HGRN_HEADS * HGRN_DK), 1.0),
        'hgrn_norm_g': gain((DEPTH, HGRN_HEADS * HGRN_DV)),
        'w_hgrn_out': nrm((DEPTH, HGRN_HEADS * HGRN_DV, D), (HGRN_HEADS * HGRN_DV) ** -0.5),
        'w_out': nrm((DEPTH, D, D), D ** -0.5),
        'norm2_g': gain((DEPTH, D)),
        'w_ff1': nrm((DEPTH, D, D_FF), D ** -0.5),
        'w_ff2': nrm((DEPTH, D_FF, D), D_FF ** -0.5),
    }


def reference(x, meta, norm1_g, w_in, conv_w, conv_b, conv_ln_g, conv_ln_b, w_conv_out,
              q_a_norm_g, w_uq, kv_a_norm_g, w_ukv, q_norm_g, k_norm_g, w_attn_out,
              hgrn_lb_logits, hgrn_norm_g, w_hgrn_out, w_out, norm2_g, w_ff1, w_ff2):
    B = x.shape[0]
    D = D_MODEL
    x = jnp.concatenate([jnp.broadcast_to(meta[None].astype(x.dtype), (B, N_META, D)), x], axis=1)
    L = x.shape[1]
    half = ROPE_DIM // 2
    pos = jnp.arange(L, dtype=jnp.float32)
    inv_freq = ROPE_BASE ** (-jnp.arange(half, dtype=jnp.float32) / half)
    ang = pos[:, None] * inv_freq[None, :]
    cos = jnp.cos(ang)[None, :, None, :]
    sin = jnp.sin(ang)[None, :, None, :]
    p_lb = jax.nn.softmax(hgrn_lb_logits.astype(jnp.float32), axis=0)
    lower_bounds = jnp.cumsum(p_lb, axis=0) - p_lb[0:1]

    for l in range(DEPTH):
        h = rms_norm(x, norm1_g[l])
        u = h @ w_in[l]
        (u_conv, c_q, c_kv, k_rope, hq, hf, hi, hg, u_gate) = jnp.split(u, SPLIT_POINTS, axis=-1)
        y_a = conv_module(u_conv, conv_w[l], conv_b[l], conv_ln_g[l], conv_ln_b[l], w_conv_out[l])
        y_b = mla(c_q, c_kv, k_rope, cos, sin, q_a_norm_g[l], w_uq[l], kv_a_norm_g[l], w_ukv[l],
                  q_norm_g[l], k_norm_g[l], w_attn_out[l])
        y_c = hgrn2(hq, hf, hi, hg, lower_bounds[l], hgrn_norm_g[l], w_hgrn_out[l])
        gates = jax.nn.sigmoid(u_gate).reshape(B, L, N_BRANCH, D)
        mix = gates[:, :, 0] * y_a + gates[:, :, 1] * y_b + gates[:, :, 2] * y_c
        x = x + mix @ w_out[l]
        h2 = rms_norm(x, norm2_g[l])
        x = x + jnp.square(jax.nn.relu(h2 @ w_ff1[l])) @ w_ff2[l]

    return x[:, N_META:]
```

```python
import functools

import jax
import jax.numpy as jnp
import numpy as np
from jax import lax
from jax.experimental import pallas as pl
from jax.experimental.pallas import tpu as pltpu

F32 = jnp.float32
BF16 = jnp.bfloat16

D_MODEL = 1024
N_META = 16
EPS = 1e-6
GATE_CLAMP = 1.0 - 1e-6
CONV_DIM = 512
CONV_K = 31
MLA_HEADS = 8
Q_RANK = 256
KV_RANK = 128
NOPE_DIM = 64
ROPE_DIM = 32
V_DIM = 64
QK_DIM = NOPE_DIM + ROPE_DIM
ROPE_BASE = 10000.0
HGRN_HEADS = 4
HGRN_DK = 128
HGRN_DV = 128
D_FF = 4096

LANE = 128
SEQ_ALIGN = 128
HEAD_SLOT = LANE

COL_GATE = 0
COL_CONV = 3072
COL_CQ = 4096
COL_CKV = 4352
COL_KROPE = 4480
COL_HGRN = 4608
N_IN_PAD = 6656

VMEM_LIMIT = 52 * 1024 * 1024

TM_IN = 1024
TN_IN = 512
TL_CONV = 128
RC_CONV = 32
CONV_HALO = 32
TL_PREP = 384
TQ = 384
HGRN_C = 128
TM_MERGE = 512
TM_FF = 1024
TF_FF = 1024


def _rms(x, eps=EPS):
    return x * lax.rsqrt(jnp.mean(x * x, axis=-1, keepdims=True) + eps)


def _inproj_kernel(x_ref, g_ref, w_ref, o_ref, h_ref):
    @pl.when(pl.program_id(1) == 0)
    def _():
        h_ref[...] = (_rms(x_ref[...]) * g_ref[...]).astype(BF16)

    o_ref[...] = jnp.dot(h_ref[...], w_ref[...], preferred_element_type=F32).astype(o_ref.dtype)


def _inproj(x2, g, w):
    t = x2.shape[0]
    return pl.pallas_call(
        _inproj_kernel,
        out_shape=jax.ShapeDtypeStruct((t, N_IN_PAD), BF16),
        grid=(t // TM_IN, N_IN_PAD // TN_IN),
        in_specs=[
            pl.BlockSpec((TM_IN, D_MODEL), lambda i, j: (i, 0)),
            pl.BlockSpec((1, D_MODEL), lambda i, j: (0, 0)),
            pl.BlockSpec((D_MODEL, TN_IN), lambda i, j: (0, j)),
        ],
        out_specs=pl.BlockSpec((TM_IN, TN_IN), lambda i, j: (i, j)),
        scratch_shapes=[pltpu.VMEM((TM_IN, D_MODEL), BF16)],
        compiler_params=pltpu.CompilerParams(
            dimension_semantics=("parallel", "arbitrary"), vmem_limit_bytes=VMEM_LIMIT),
    )(x2, g, w)


def _conv_kernel(a_ref, gt_ref, w_ref, cb_ref, lg_ref, lb_ref, o_ref, hbuf):
    j = pl.program_id(1)

    @pl.when(j == 0)
    def _():
        hbuf[0:CONV_HALO, :] = jnp.zeros((CONV_HALO, CONV_DIM), F32)

    @pl.when(j > 0)
    def _():
        hbuf[0:CONV_HALO, :] = hbuf[TL_CONV:TL_CONV + CONV_HALO, :]

    a = a_ref[0].astype(F32)
    gt = gt_ref[0].astype(F32)
    hbuf[CONV_HALO:CONV_HALO + TL_CONV, :] = a * jax.nn.sigmoid(gt)

    base = CONV_HALO - (CONV_K - 1)
    for c in range(TL_CONV // RC_CONV):
        r0 = c * RC_CONV
        acc = jnp.zeros((RC_CONV, CONV_DIM), F32) + cb_ref[...]
        for k in range(CONV_K):
            acc = acc + w_ref[k:k + 1, :] * hbuf[r0 + base + k:r0 + base + k + RC_CONV, :]
        mu = jnp.mean(acc, axis=-1, keepdims=True)
        xc = acc - mu
        y = xc * lax.rsqrt(jnp.mean(xc * xc, axis=-1, keepdims=True) + EPS)
        y = y * lg_ref[...] + lb_ref[...]
        o_ref[0, r0:r0 + RC_CONV, :] = (y * jax.nn.sigmoid(y)).astype(o_ref.dtype)


def _conv_branch(u3, conv_w, conv_b, ln_g, ln_b):
    b, lp, _ = u3.shape
    wblk = CONV_DIM
    return pl.pallas_call(
        _conv_kernel,
        out_shape=jax.ShapeDtypeStruct((b, lp, CONV_DIM), BF16),
        grid=(b, lp // TL_CONV),
        in_specs=[
            pl.BlockSpec((1, TL_CONV, wblk), lambda bb, j: (bb, j, COL_CONV // wblk)),
            pl.BlockSpec((1, TL_CONV, wblk), lambda bb, j: (bb, j, COL_CONV // wblk + 1)),
            pl.BlockSpec((CONV_K, CONV_DIM), lambda bb, j: (0, 0)),
            pl.BlockSpec((1, CONV_DIM), lambda bb, j: (0, 0)),
            pl.BlockSpec((1, CONV_DIM), lambda bb, j: (0, 0)),
            pl.BlockSpec((1, CONV_DIM), lambda bb, j: (0, 0)),
        ],
        out_specs=pl.BlockSpec((1, TL_CONV, CONV_DIM), lambda bb, j: (bb, j, 0)),
        scratch_shapes=[pltpu.VMEM((CONV_HALO + TL_CONV, CONV_DIM), F32)],
        compiler_params=pltpu.CompilerParams(
            dimension_semantics=("parallel", "arbitrary"), vmem_limit_bytes=VMEM_LIMIT),
    )(u3, u3, conv_w, conv_b, ln_g, ln_b)


def _rope(x, c, s1, s2):
    half = ROPE_DIM // 2
    return x * c + pltpu.roll(x, LANE - half, 1) * s1 + pltpu.roll(x, half, 1) * s2


def _mla_prep_kernel(cq_ref, ckv_ref, kr_ref, qag_ref, kvag_ref, wuq_ref, wuk_ref, wuv_ref,
                     qng_ref, kng_ref, rc_ref, rs1_ref, rs2_ref, vone_ref,
                     q_out, k_out, v_out):
    cq = _rms(cq_ref[0].astype(F32)) * qag_ref[...]
    q = jnp.dot(cq.astype(BF16), wuq_ref[...], preferred_element_type=F32)
    ckv = (_rms(ckv_ref[0].astype(F32)) * kvag_ref[...]).astype(BF16)
    kn = jnp.dot(ckv, wuk_ref[...], preferred_element_type=F32)
    v = jnp.dot(ckv, wuv_ref[...], preferred_element_type=F32)
    v_out[0] = (v + vone_ref[...]).astype(v_out.dtype)
    kr = kr_ref[0].astype(F32)
    rc, rs1, rs2 = rc_ref[...], rs1_ref[...], rs2_ref[...]
    inv_d = 1.0 / QK_DIM
    for h in range(MLA_HEADS):
        sl = slice(h * HEAD_SLOT, (h + 1) * HEAD_SLOT)
        qh = q[:, sl]
        qh = qh * lax.rsqrt(jnp.sum(qh * qh, axis=-1, keepdims=True) * inv_d + EPS) * qng_ref[...]
        q_out[0, :, sl] = _rope(qh, rc, rs1, rs2).astype(q_out.dtype)
        kh = kn[:, sl] + kr
        kh = kh * lax.rsqrt(jnp.sum(kh * kh, axis=-1, keepdims=True) * inv_d + EPS) * kng_ref[...]
        k_out[0, :, sl] = _rope(kh, rc, rs1, rs2).astype(k_out.dtype)


def _mla_prep(u3, qag, kvag, wuq, wuk, wuv, qng, kng, rc, rs1, rs2, vone):
    b, lp, _ = u3.shape
    hw = MLA_HEADS * HEAD_SLOT
    const = lambda shape: pl.BlockSpec(shape, lambda bb, j: (0, 0))
    tab = pl.BlockSpec((TL_PREP, LANE), lambda bb, j: (j, 0))
    out = jax.ShapeDtypeStruct((b, lp, hw), BF16)
    ospec = pl.BlockSpec((1, TL_PREP, hw), lambda bb, j: (bb, j, 0))
    return pl.pallas_call(
        _mla_prep_kernel,
        out_shape=(out, out, out),
        grid=(b, lp // TL_PREP),
        in_specs=[
            pl.BlockSpec((1, TL_PREP, Q_RANK), lambda bb, j: (bb, j, COL_CQ // Q_RANK)),
            pl.BlockSpec((1, TL_PREP, KV_RANK), lambda bb, j: (bb, j, COL_CKV // KV_RANK)),
            pl.BlockSpec((1, TL_PREP, LANE), lambda bb, j: (bb, j, COL_KROPE // LANE)),
            const((1, Q_RANK)), const((1, KV_RANK)),
            const((Q_RANK, hw)), const((KV_RANK, hw)), const((KV_RANK, hw)),
            const((1, LANE)), const((1, LANE)),
            tab, tab, tab,
            const((1, hw)),
        ],
        out_specs=(ospec, ospec, ospec),
        compiler_params=pltpu.CompilerParams(
            dimension_semantics=("parallel", "parallel"), vmem_limit_bytes=VMEM_LIMIT),
    )(u3, u3, u3, qag, kvag, wuq, wuk, wuv, qng, kng, rc, rs1, rs2, vone)


def _attn_kernel(q_ref, k_ref, v_ref, o_ref, m_ref, acc_ref):
    i = pl.program_id(1)
    j = pl.program_id(2)

    @pl.when(j == 0)
    def _():
        m_ref[...] = jnp.full(m_ref.shape, -1e30, F32)
        acc_ref[...] = jnp.zeros(acc_ref.shape, F32)

    def step(masked):
        if masked:
            row = lax.broadcasted_iota(jnp.int32, (TQ, TQ), 0)
            col = lax.broadcasted_iota(jnp.int32, (TQ, TQ), 1)
            keep = row >= col
        for h in range(MLA_HEADS):
            sl = slice(h * HEAD_SLOT, (h + 1) * HEAD_SLOT)
            s = lax.dot_general(q_ref[0, :, sl], k_ref[0, :, sl], (((1,), (1,)), ((), ())),
                                preferred_element_type=F32)
            if masked:
                s = jnp.where(keep, s, -1e30)
            m_prev = m_ref[h]
            m_new = jnp.maximum(m_prev, jnp.max(s, axis=-1, keepdims=True))
            alpha = jnp.exp(m_prev - m_new)
            p = jnp.exp(s - m_new).astype(BF16)
            acc_ref[h] = alpha * acc_ref[h] + jnp.dot(p, v_ref[0, :, sl], preferred_element_type=F32)
            m_ref[h] = m_new

    @pl.when(j < i)
    def _():
        step(False)

    @pl.when(j == i)
    def _():
        step(True)
        for h in range(MLA_HEADS):
            a = acc_ref[h]
            o_ref[0, :, h * V_DIM:(h + 1) * V_DIM] = (
                a[:, :V_DIM] / a[:, V_DIM:V_DIM + 1]).astype(o_ref.dtype)


def _attention(q, k, v):
    b, lp, hw = q.shape
    nq = lp // TQ
    return pl.pallas_call(
        _attn_kernel,
        out_shape=jax.ShapeDtypeStruct((b, lp, MLA_HEADS * V_DIM), BF16),
        grid=(b, nq, nq),
        in_specs=[
            pl.BlockSpec((1, TQ, hw), lambda bb, i, j: (bb, i, 0)),
            pl.BlockSpec((1, TQ, hw), lambda bb, i, j: (bb, jnp.minimum(i, j), 0)),
            pl.BlockSpec((1, TQ, hw), lambda bb, i, j: (bb, jnp.minimum(i, j), 0)),
        ],
        out_specs=pl.BlockSpec((1, TQ, MLA_HEADS * V_DIM), lambda bb, i, j: (bb, i, 0)),
        scratch_shapes=[pltpu.VMEM((MLA_HEADS, TQ, 1), F32),
                        pltpu.VMEM((MLA_HEADS, TQ, HEAD_SLOT), F32)],
        compiler_params=pltpu.CompilerParams(
            dimension_semantics=("parallel", "parallel", "arbitrary"), vmem_limit_bytes=VMEM_LIMIT),
    )(q, k, v)


def _hgrn_kernel(q_ref, f_ref, i_ref, g_ref, oml_ref, ng_ref, lvl_ref, o_ref, st_ref):
    C = HGRN_C

    @pl.when(pl.program_id(1) == 0)
    def _():
        st_ref[...] = jnp.zeros(st_ref.shape, F32)

    lvl = lvl_ref[...]
    rowid = lax.broadcasted_iota(jnp.int32, (C, HGRN_DK), 0)
    nt = (((1,), (1,)), ((), ()))
    tn = (((0,), (0,)), ((), ()))
    for h in range(HGRN_HEADS):
        sl = slice(h * HGRN_DK, (h + 1) * HGRN_DK)
        q = q_ref[0, :, sl].astype(F32)
        fr = f_ref[0, :, sl].astype(F32)
        kk = oml_ref[:, sl] * jax.nn.sigmoid(-fr)
        lf = jnp.log1p(-jnp.minimum(kk, GATE_CLAMP))
        iv = i_ref[0, :, sl].astype(F32)
        v = (iv * jax.nn.sigmoid(iv)).astype(BF16)

        fwd = lf
        bwd = jnp.zeros_like(lf)
        a = jnp.where(lvl == 0, lax.dot_general(q.astype(BF16), kk.astype(BF16), nt,
                                                preferred_element_type=F32), 0.0)
        n, level = 1, 1
        while n < C:
            qs = (q * jnp.exp(fwd)).astype(BF16)
            ks = (kk * jnp.exp(bwd)).astype(BF16)
            a = a + jnp.where(lvl == level,
                              lax.dot_general(qs, ks, nt, preferred_element_type=F32), 0.0)
            tot = fwd + bwd
            upper = (rowid & n) != 0
            fwd = fwd + jnp.where(upper, pltpu.roll(tot, n, 0), 0.0)
            bwd = bwd + jnp.where(upper, 0.0, pltpu.roll(tot, C - n, 0))
            n, level = 2 * n, level + 1

        st = st_ref[h]
        o = jnp.dot(a.astype(BF16), v, preferred_element_type=F32)
        o = o + lax.dot_general((q * jnp.exp(fwd)).astype(BF16), st.astype(BF16), nt,
                                preferred_element_type=F32)
        ks = (kk * jnp.exp(bwd)).astype(BF16)
        st_ref[h] = st * jnp.exp(fwd[C - 1:C, :]) + lax.dot_general(
            v, ks, tn, preferred_element_type=F32)

        o = _rms(o) * ng_ref[:, sl]
        gv = g_ref[0, :, sl].astype(F32)
        o_ref[0, :, sl] = (o * (gv * jax.nn.sigmoid(gv))).astype(o_ref.dtype)


def _hgrn_levels():
    t = np.arange(HGRN_C)[:, None]
    s = np.arange(HGRN_C)[None, :]
    x = t ^ s
    lvl = np.where(x == 0, 0, np.floor(np.log2(np.maximum(x, 1))).astype(np.int64) + 1)
    return jnp.asarray(np.where(s > t, -1, lvl), dtype=jnp.int32)


def _hgrn(u3, one_minus_lb, norm_g):
    b, lp, _ = u3.shape
    w = HGRN_HEADS * HGRN_DK
    blk = lambda n: pl.BlockSpec((1, HGRN_C, w), lambda bb, c: (bb, c, COL_HGRN // w + n))
    const = lambda shape: pl.BlockSpec(shape, lambda bb, c: (0, 0))
    return pl.pallas_call(
        _hgrn_kernel,
        out_shape=jax.ShapeDtypeStruct((b, lp, w), BF16),
        grid=(b, lp // HGRN_C),
        in_specs=[blk(0), blk(1), blk(2), blk(3), const((1, w)), const((1, w)),
                  const((HGRN_C, HGRN_C))],
        out_specs=pl.BlockSpec((1, HGRN_C, w), lambda bb, c: (bb, c, 0)),
        scratch_shapes=[pltpu.VMEM((HGRN_HEADS, HGRN_DV, HGRN_DK), F32)],
        compiler_params=pltpu.CompilerParams(
            dimension_semantics=("parallel", "arbitrary"), vmem_limit_bytes=VMEM_LIMIT),
    )(u3, u3, u3, u3, one_minus_lb, norm_g, _hgrn_levels())


def _merge_kernel(ha_ref, ob_ref, oc_ref, gate_ref, x_ref, wa_ref, wb_ref, wc_ref, wo_ref, o_ref):
    gate = gate_ref[...]
    mix = None
    for n, (br_ref, w_ref) in enumerate(((ha_ref, wa_ref), (ob_ref, wb_ref), (oc_ref, wc_ref))):
        y = jnp.dot(br_ref[...], w_ref[...], preferred_element_type=F32)
        y = y * jax.nn.sigmoid(gate[:, n * D_MODEL:(n + 1) * D_MODEL].astype(F32))
        mix = y if mix is None else mix + y
    o_ref[...] = x_ref[...] + jnp.dot(mix.astype(BF16), wo_ref[...], preferred_element_type=F32)


def _merge(ha, ob, oc, u2, x2, wa, wb, wc, wo):
    t = x2.shape[0]
    br = lambda: pl.BlockSpec((TM_MERGE, 512), lambda i: (i, 0))
    wbr = lambda: pl.BlockSpec((512, D_MODEL), lambda i: (0, 0))
    return pl.pallas_call(
        _merge_kernel,
        out_shape=jax.ShapeDtypeStruct((t, D_MODEL), F32),
        grid=(t // TM_MERGE,),
        in_specs=[br(), br(), br(),
                  pl.BlockSpec((TM_MERGE, 3 * D_MODEL), lambda i: (i, COL_GATE // (3 * D_MODEL))),
                  pl.BlockSpec((TM_MERGE, D_MODEL), lambda i: (i, 0)),
                  wbr(), wbr(), wbr(),
                  pl.BlockSpec((D_MODEL, D_MODEL), lambda i: (0, 0))],
        out_specs=pl.BlockSpec((TM_MERGE, D_MODEL), lambda i: (i, 0)),
        compiler_params=pltpu.CompilerParams(
            dimension_semantics=("parallel",), vmem_limit_bytes=VMEM_LIMIT),
    )(ha, ob, oc, u2, x2, wa, wb, wc, wo)


def _ffn_kernel(x_ref, g_ref, w1_ref, w2_ref, o_ref, h_ref):
    j = pl.program_id(1)

    @pl.when(j == 0)
    def _():
        x = x_ref[...]
        h_ref[...] = (_rms(x) * g_ref[...]).astype(BF16)
        o_ref[...] = x

    a = jnp.maximum(jnp.dot(h_ref[...], w1_ref[...], preferred_element_type=F32), 0.0)
    o_ref[...] += jnp.dot((a * a).astype(BF16), w2_ref[...], preferred_element_type=F32)


def _ffn(x2, g, w1, w2):
    t = x2.shape[0]
    return pl.pallas_call(
        _ffn_kernel,
        out_shape=jax.ShapeDtypeStruct((t, D_MODEL), F32),
        grid=(t // TM_FF, D_FF // TF_FF),
        in_specs=[
            pl.BlockSpec((TM_FF, D_MODEL), lambda i, j: (i, 0)),
            pl.BlockSpec((1, D_MODEL), lambda i, j: (0, 0)),
            pl.BlockSpec((D_MODEL, TF_FF), lambda i, j: (0, j)),
            pl.BlockSpec((TF_FF, D_MODEL), lambda i, j: (j, 0)),
        ],
        out_specs=pl.BlockSpec((TM_FF, D_MODEL), lambda i, j: (i, 0)),
        scratch_shapes=[pltpu.VMEM((TM_FF, D_MODEL), BF16)],
        compiler_params=pltpu.CompilerParams(
            dimension_semantics=("parallel", "arbitrary"), vmem_limit_bytes=VMEM_LIMIT),
    )(x2, g, w1, w2)


def _regroup_w_in(w):
    o = 0
    conv = w[:, o:o + 2 * CONV_DIM]; o += 2 * CONV_DIM
    cq = w[:, o:o + Q_RANK]; o += Q_RANK
    ckv = w[:, o:o + KV_RANK]; o += KV_RANK
    kr = w[:, o:o + ROPE_DIM]; o += ROPE_DIM
    hg = w[:, o:o + 4 * HGRN_HEADS * HGRN_DK]; o += 4 * HGRN_HEADS * HGRN_DK
    gate = w[:, o:]
    kr_slot = jnp.pad(kr, ((0, 0), (NOPE_DIM, LANE - QK_DIM)))
    return jnp.concatenate([gate, conv, cq, ckv, kr_slot, hg], axis=1).astype(BF16)


def _head_slots(w, per_head, start, width):
    r = w.shape[0]
    wh = w.reshape(r, MLA_HEADS, per_head)[:, :, start:start + width]
    wh = jnp.pad(wh, ((0, 0), (0, 0), (0, HEAD_SLOT - width)))
    return wh.reshape(r, MLA_HEADS * HEAD_SLOT)


def _rope_tables(lp):
    half = ROPE_DIM // 2
    pos = jnp.arange(lp, dtype=F32)
    inv_freq = ROPE_BASE ** (-jnp.arange(half, dtype=F32) / half)
    ang = pos[:, None] * inv_freq[None, :]
    cos, sin = jnp.cos(ang), jnp.sin(ang)
    ones = jnp.ones((lp, NOPE_DIM), F32)
    z16 = jnp.zeros((lp, half), F32)
    z64 = jnp.zeros((lp, NOPE_DIM), F32)
    tail = jnp.zeros((lp, LANE - QK_DIM), F32)
    rc = jnp.concatenate([ones, cos, cos, tail], axis=1)
    rs1 = jnp.concatenate([z64, -sin, z16, tail], axis=1)
    rs2 = jnp.concatenate([z64, z16, sin, tail], axis=1)
    return rc, rs1, rs2


def kernel(x, meta, norm1_g, w_in, conv_w, conv_b, conv_ln_g, conv_ln_b, w_conv_out, q_a_norm_g, w_uq, kv_a_norm_g, w_ukv, q_norm_g, k_norm_g, w_attn_out, hgrn_lb_logits, hgrn_norm_g, w_hgrn_out, w_out, norm2_g, w_ff1, w_ff2):
    b, seq, d = x.shape
    depth = w_in.shape[0]
    l = seq + N_META
    lp = -(-l // (3 * SEQ_ALIGN)) * (3 * SEQ_ALIGN)
    xc = jnp.concatenate([jnp.broadcast_to(meta[None].astype(x.dtype), (b, N_META, d)), x,
                          jnp.zeros((b, lp - l, d), x.dtype)], axis=1)
    x2 = xc.reshape(b * lp, d)

    rc, rs1, rs2 = _rope_tables(lp)
    p_lb = jax.nn.softmax(hgrn_lb_logits.astype(F32), axis=0)
    lower_bounds = jnp.cumsum(p_lb, axis=0) - p_lb[0:1]
    vone = jnp.zeros((MLA_HEADS, HEAD_SLOT), F32).at[:, V_DIM].set(1.0).reshape(1, -1)
    row = lambda a: a.astype(F32).reshape(1, -1)
    pad_qk = lambda g: jnp.pad(g.astype(F32), (0, LANE - QK_DIM)).reshape(1, LANE)

    for li in range(depth):
        u2 = _inproj(x2, row(norm1_g[li]), _regroup_w_in(w_in[li]))
        u3 = u2.reshape(b, lp, N_IN_PAD)
        ha = _conv_branch(u3, conv_w[li].astype(F32), row(conv_b[li]), row(conv_ln_g[li]),
                          row(conv_ln_b[li]))
        q, k, v = _mla_prep(
            u3, row(q_a_norm_g[li]), row(kv_a_norm_g[li]),
            _head_slots(w_uq[li], QK_DIM, 0, QK_DIM).astype(BF16),
            _head_slots(w_ukv[li], NOPE_DIM + V_DIM, 0, NOPE_DIM).astype(BF16),
            _head_slots(w_ukv[li], NOPE_DIM + V_DIM, NOPE_DIM, V_DIM).astype(BF16),
            pad_qk(q_norm_g[li]) * (QK_DIM ** -0.5), pad_qk(k_norm_g[li]), rc, rs1, rs2, vone)
        ob = _attention(q, k, v)
        oc = _hgrn(u3, 1.0 - row(lower_bounds[li]), row(hgrn_norm_g[li]))
        x2 = _merge(ha.reshape(b * lp, -1), ob.reshape(b * lp, -1), oc.reshape(b * lp, -1), u2, x2,
                    w_conv_out[li].astype(BF16), w_attn_out[li].astype(BF16),
                    w_hgrn_out[li].astype(BF16), w_out[li].astype(BF16))
        x2 = _ffn(x2, row(norm2_g[li]), w_ff1[li].astype(BF16), w_ff2[li].astype(BF16))

    return x2.reshape(b, lp, d)[:, N_META:l]
```

```python
import functools

import jax
import jax.numpy as jnp
import numpy as np
from jax import lax
from jax.experimental import pallas as pl
from jax.experimental.pallas import tpu as pltpu

F32 = jnp.float32
BF16 = jnp.bfloat16

D_MODEL = 1024
N_META = 16
EPS = 1e-6
GATE_CLAMP = 1.0 - 1e-6
CONV_DIM = 512
CONV_K = 31
MLA_HEADS = 8
Q_RANK = 256
KV_RANK = 128
NOPE_DIM = 64
ROPE_DIM = 32
V_DIM = 64
QK_DIM = NOPE_DIM + ROPE_DIM
ROPE_BASE = 10000.0
HGRN_HEADS = 4
HGRN_DK = 128
HGRN_DV = 128
D_FF = 4096

LANE = 128
SEQ_ALIGN = 128
HEAD_SLOT = LANE

COL_GATE = 0
COL_CONV = 3072
COL_CQ = 4096
COL_CKV = 4352
COL_KROPE = 4480
COL_HGRN = 4608
N_IN_PAD = 6656

VMEM_LIMIT = 52 * 1024 * 1024

TM_IN = 1024
TN_IN = 512
TL_CONV = 128
RC_CONV = 32
CONV_HALO = 32
TL_PREP = 384
TQ = 384
ATTN_HG = 4
LOG2E = 1.4426950408889634
HGRN_C = 128
TM_MERGE = 512
TM_FF = 1024
TF_FF = 1024


def _rms(x, eps=EPS):
    return x * lax.rsqrt(jnp.mean(x * x, axis=-1, keepdims=True) + eps)


def _inproj_kernel(x_ref, g_ref, w_ref, o_ref, h_ref):
    @pl.when(pl.program_id(1) == 0)
    def _():
        h_ref[...] = (_rms(x_ref[...]) * g_ref[...]).astype(BF16)

    o_ref[...] = jnp.dot(h_ref[...], w_ref[...], preferred_element_type=F32).astype(o_ref.dtype)


def _inproj(x2, g, w):
    t = x2.shape[0]
    return pl.pallas_call(
        _inproj_kernel,
        out_shape=jax.ShapeDtypeStruct((t, N_IN_PAD), BF16),
        grid=(t // TM_IN, N_IN_PAD // TN_IN),
        in_specs=[
            pl.BlockSpec((TM_IN, D_MODEL), lambda i, j: (i, 0)),
            pl.BlockSpec((1, D_MODEL), lambda i, j: (0, 0)),
            pl.BlockSpec((D_MODEL, TN_IN), lambda i, j: (0, j)),
        ],
        out_specs=pl.BlockSpec((TM_IN, TN_IN), lambda i, j: (i, j)),
        scratch_shapes=[pltpu.VMEM((TM_IN, D_MODEL), BF16)],
        compiler_params=pltpu.CompilerParams(
            dimension_semantics=("parallel", "arbitrary"), vmem_limit_bytes=VMEM_LIMIT),
    )(x2, g, w)


def _conv_kernel(a_ref, gt_ref, w_ref, cb_ref, lg_ref, lb_ref, o_ref, hbuf):
    j = pl.program_id(1)

    @pl.when(j == 0)
    def _():
        hbuf[0:CONV_HALO, :] = jnp.zeros((CONV_HALO, CONV_DIM), F32)

    @pl.when(j > 0)
    def _():
        hbuf[0:CONV_HALO, :] = hbuf[TL_CONV:TL_CONV + CONV_HALO, :]

    a = a_ref[0].astype(F32)
    gt = gt_ref[0].astype(F32)
    hbuf[CONV_HALO:CONV_HALO + TL_CONV, :] = a * jax.nn.sigmoid(gt)

    base = CONV_HALO - (CONV_K - 1)
    for c in range(TL_CONV // RC_CONV):
        r0 = c * RC_CONV
        acc = jnp.zeros((RC_CONV, CONV_DIM), F32) + cb_ref[...]
        for k in range(CONV_K):
            acc = acc + w_ref[k:k + 1, :] * hbuf[r0 + base + k:r0 + base + k + RC_CONV, :]
        mu = jnp.mean(acc, axis=-1, keepdims=True)
        xc = acc - mu
        y = xc * lax.rsqrt(jnp.mean(xc * xc, axis=-1, keepdims=True) + EPS)
        y = y * lg_ref[...] + lb_ref[...]
        o_ref[0, r0:r0 + RC_CONV, :] = (y * jax.nn.sigmoid(y)).astype(o_ref.dtype)


def _conv_branch(u3, conv_w, conv_b, ln_g, ln_b):
    b, lp, _ = u3.shape
    wblk = CONV_DIM
    return pl.pallas_call(
        _conv_kernel,
        out_shape=jax.ShapeDtypeStruct((b, lp, CONV_DIM), BF16),
        grid=(b, lp // TL_CONV),
        in_specs=[
            pl.BlockSpec((1, TL_CONV, wblk), lambda bb, j: (bb, j, COL_CONV // wblk)),
            pl.BlockSpec((1, TL_CONV, wblk), lambda bb, j: (bb, j, COL_CONV // wblk + 1)),
            pl.BlockSpec((CONV_K, CONV_DIM), lambda bb, j: (0, 0)),
            pl.BlockSpec((1, CONV_DIM), lambda bb, j: (0, 0)),
            pl.BlockSpec((1, CONV_DIM), lambda bb, j: (0, 0)),
            pl.BlockSpec((1, CONV_DIM), lambda bb, j: (0, 0)),
        ],
        out_specs=pl.BlockSpec((1, TL_CONV, CONV_DIM), lambda bb, j: (bb, j, 0)),
        scratch_shapes=[pltpu.VMEM((CONV_HALO + TL_CONV, CONV_DIM), F32)],
        compiler_params=pltpu.CompilerParams(
            dimension_semantics=("parallel", "arbitrary"), vmem_limit_bytes=VMEM_LIMIT),
    )(u3, u3, conv_w, conv_b, ln_g, ln_b)


def _rope(x, c, s1, s2):
    half = ROPE_DIM // 2
    return x * c + pltpu.roll(x, LANE - half, 1) * s1 + pltpu.roll(x, half, 1) * s2


def _mla_prep_kernel(cq_ref, ckv_ref, kr_ref, qag_ref, kvag_ref, wuq_ref, wuk_ref, wuv_ref,
                     qng_ref, kng_ref, rc_ref, rs1_ref, rs2_ref, vone_ref,
                     q_out, k_out, v_out):
    cq = _rms(cq_ref[0].astype(F32)) * qag_ref[...]
    q = jnp.dot(cq.astype(BF16), wuq_ref[...], preferred_element_type=F32)
    ckv = (_rms(ckv_ref[0].astype(F32)) * kvag_ref[...]).astype(BF16)
    kn = jnp.dot(ckv, wuk_ref[...], preferred_element_type=F32)
    v = jnp.dot(ckv, wuv_ref[...], preferred_element_type=F32) + vone_ref[...]
    kr = kr_ref[0].astype(F32)
    rc, rs1, rs2 = rc_ref[...], rs1_ref[...], rs2_ref[...]
    inv_d = 1.0 / QK_DIM
    for h in range(MLA_HEADS):
        sl = slice(h * HEAD_SLOT, (h + 1) * HEAD_SLOT)
        qh = q[:, sl]
        qh = qh * lax.rsqrt(jnp.sum(qh * qh, axis=-1, keepdims=True) * inv_d + EPS) * qng_ref[...]
        q_out[0, h] = _rope(qh, rc, rs1, rs2).astype(q_out.dtype)
        kh = kn[:, sl] + kr
        kh = kh * lax.rsqrt(jnp.sum(kh * kh, axis=-1, keepdims=True) * inv_d + EPS) * kng_ref[...]
        k_out[0, h] = _rope(kh, rc, rs1, rs2).astype(k_out.dtype)
        v_out[0, h] = v[:, sl].astype(v_out.dtype)


def _mla_prep(u3, qag, kvag, wuq, wuk, wuv, qng, kng, rc, rs1, rs2, vone):
    b, lp, _ = u3.shape
    hw = MLA_HEADS * HEAD_SLOT
    const = lambda shape: pl.BlockSpec(shape, lambda bb, j: (0, 0))
    tab = pl.BlockSpec((TL_PREP, LANE), lambda bb, j: (j, 0))
    out = jax.ShapeDtypeStruct((b, MLA_HEADS, lp, HEAD_SLOT), BF16)
    ospec = pl.BlockSpec((1, MLA_HEADS, TL_PREP, HEAD_SLOT), lambda bb, j: (bb, 0, j, 0))
    return pl.pallas_call(
        _mla_prep_kernel,
        out_shape=(out, out, out),
        grid=(b, lp // TL_PREP),
        in_specs=[
            pl.BlockSpec((1, TL_PREP, Q_RANK), lambda bb, j: (bb, j, COL_CQ // Q_RANK)),
            pl.BlockSpec((1, TL_PREP, KV_RANK), lambda bb, j: (bb, j, COL_CKV // KV_RANK)),
            pl.BlockSpec((1, TL_PREP, LANE), lambda bb, j: (bb, j, COL_KROPE // LANE)),
            const((1, Q_RANK)), const((1, KV_RANK)),
            const((Q_RANK, hw)), const((KV_RANK, hw)), const((KV_RANK, hw)),
            const((1, LANE)), const((1, LANE)),
            tab, tab, tab,
            const((1, hw)),
        ],
        out_specs=(ospec, ospec, ospec),
        compiler_params=pltpu.CompilerParams(
            dimension_semantics=("parallel", "parallel"), vmem_limit_bytes=VMEM_LIMIT),
    )(u3, u3, u3, qag, kvag, wuq, wuk, wuv, qng, kng, rc, rs1, rs2, vone)


def _attn_kernel(q_ref, k_ref, v_ref, o_ref, s_ref):
    i = pl.program_id(2)
    nt = (((1,), (1,)), ((), ()))
    qs = [q_ref[0, h] for h in range(ATTN_HG)]

    def scores(h, j):
        kt = k_ref[0, h, pl.ds(pl.multiple_of(j * TQ, TQ), TQ), :]
        return lax.dot_general(qs[h], kt, nt, preferred_element_type=F32)

    def fold(s):
        m = s[:, 0:LANE]
        for c in range(1, TQ // LANE):
            m = jnp.maximum(m, s[:, c * LANE:(c + 1) * LANE])
        return m

    def pass1(j, mloc):
        out = []
        for h in range(ATTN_HG):
            s = scores(h, j)
            s_ref[h, j] = s
            out.append(jnp.maximum(mloc[h], fold(s)))
        return tuple(out)

    mloc = lax.fori_loop(
        0, i, pass1, tuple(jnp.full((TQ, LANE), -1e30, F32) for _ in range(ATTN_HG)))

    keep = (lax.broadcasted_iota(jnp.int32, (TQ, TQ), 0)
            >= lax.broadcasted_iota(jnp.int32, (TQ, TQ), 1))
    ms = []
    for h in range(ATTN_HG):
        s = jnp.where(keep, scores(h, i), -1e30)
        s_ref[h, i] = s
        ms.append(jnp.max(jnp.maximum(mloc[h], fold(s)), axis=-1, keepdims=True))

    def pass2(j, acc):
        out = []
        for h in range(ATTN_HG):
            p = jnp.exp2(s_ref[h, j] - ms[h]).astype(BF16)
            vt = v_ref[0, h, pl.ds(pl.multiple_of(j * TQ, TQ), TQ), :]
            out.append(acc[h] + jnp.dot(p, vt, preferred_element_type=F32))
        return tuple(out)

    acc = lax.fori_loop(
        0, i + 1, pass2, tuple(jnp.zeros((TQ, HEAD_SLOT), F32) for _ in range(ATTN_HG)))
    for h in range(ATTN_HG):
        o_ref[0, :, h * V_DIM:(h + 1) * V_DIM] = (
            acc[h][:, :V_DIM] / acc[h][:, V_DIM:V_DIM + 1]).astype(o_ref.dtype)


def _attention(q, k, v):
    b, nh, lp, hs = q.shape
    nq = lp // TQ
    return pl.pallas_call(
        _attn_kernel,
        out_shape=jax.ShapeDtypeStruct((b, lp, nh * V_DIM), BF16),
        grid=(b, nh // ATTN_HG, nq),
        in_specs=[
            pl.BlockSpec((1, ATTN_HG, TQ, hs), lambda bb, g, i: (bb, g, i, 0)),
            pl.BlockSpec((1, ATTN_HG, lp, hs), lambda bb, g, i: (bb, g, 0, 0)),
            pl.BlockSpec((1, ATTN_HG, lp, hs), lambda bb, g, i: (bb, g, 0, 0)),
        ],
        out_specs=pl.BlockSpec((1, TQ, ATTN_HG * V_DIM), lambda bb, g, i: (bb, i, g)),
        scratch_shapes=[pltpu.VMEM((ATTN_HG, nq, TQ, TQ), F32)],
        compiler_params=pltpu.CompilerParams(
            dimension_semantics=("parallel", "parallel", "arbitrary"), vmem_limit_bytes=VMEM_LIMIT),
    )(q, k, v)


def _hgrn_kernel(q_ref, f_ref, i_ref, g_ref, oml_ref, ng_ref, lvl_ref, o_ref, st_ref):
    C = HGRN_C

    @pl.when(pl.program_id(1) == 0)
    def _():
        st_ref[...] = jnp.zeros(st_ref.shape, F32)

    lvl = lvl_ref[...]
    rowid = lax.broadcasted_iota(jnp.int32, (C, HGRN_DK), 0)
    nt = (((1,), (1,)), ((), ()))
    tn = (((0,), (0,)), ((), ()))
    for h in range(HGRN_HEADS):
        sl = slice(h * HGRN_DK, (h + 1) * HGRN_DK)
        q = q_ref[0, :, sl].astype(F32)
        fr = f_ref[0, :, sl].astype(F32)
        kk = oml_ref[:, sl] * jax.nn.sigmoid(-fr)
        lf = jnp.log1p(-jnp.minimum(kk, GATE_CLAMP))
        iv = i_ref[0, :, sl].astype(F32)
        v = (iv * jax.nn.sigmoid(iv)).astype(BF16)

        fwd = lf
        bwd = jnp.zeros_like(lf)
        a = jnp.where(lvl == 0, lax.dot_general(q.astype(BF16), kk.astype(BF16), nt,
                                                preferred_element_type=F32), 0.0)
        n, level = 1, 1
        while n < C:
            qs = (q * jnp.exp(fwd)).astype(BF16)
            ks = (kk * jnp.exp(bwd)).astype(BF16)
            a = a + jnp.where(lvl == level,
                              lax.dot_general(qs, ks, nt, preferred_element_type=F32), 0.0)
            tot = fwd + bwd
            upper = (rowid & n) != 0
            fwd = fwd + jnp.where(upper, pltpu.roll(tot, n, 0), 0.0)
            bwd = bwd + jnp.where(upper, 0.0, pltpu.roll(tot, C - n, 0))
            n, level = 2 * n, level + 1

        st = st_ref[h]
        o = jnp.dot(a.astype(BF16), v, preferred_element_type=F32)
        o = o + lax.dot_general((q * jnp.exp(fwd)).astype(BF16), st.astype(BF16), nt,
                                preferred_element_type=F32)
        ks = (kk * jnp.exp(bwd)).astype(BF16)
        st_ref[h] = st * jnp.exp(fwd[C - 1:C, :]) + lax.dot_general(
            v, ks, tn, preferred_element_type=F32)

        o = _rms(o) * ng_ref[:, sl]
        gv = g_ref[0, :, sl].astype(F32)
        o_ref[0, :, sl] = (o * (gv * jax.nn.sigmoid(gv))).astype(o_ref.dtype)


def _hgrn_levels():
    t = np.arange(HGRN_C)[:, None]
    s = np.arange(HGRN_C)[None, :]
    x = t ^ s
    lvl = np.where(x == 0, 0, np.floor(np.log2(np.maximum(x, 1))).astype(np.int64) + 1)
    return jnp.asarray(np.where(s > t, -1, lvl), dtype=jnp.int32)


def _hgrn(u3, one_minus_lb, norm_g):
    b, lp, _ = u3.shape
    w = HGRN_HEADS * HGRN_DK
    blk = lambda n: pl.BlockSpec((1, HGRN_C, w), lambda bb, c: (bb, c, COL_HGRN // w + n))
    const = lambda shape: pl.BlockSpec(shape, lambda bb, c: (0, 0))
    return pl.pallas_call(
        _hgrn_kernel,
        out_shape=jax.ShapeDtypeStruct((b, lp, w), BF16),
        grid=(b, lp // HGRN_C),
        in_specs=[blk(0), blk(1), blk(2), blk(3), const((1, w)), const((1, w)),
                  const((HGRN_C, HGRN_C))],
        out_specs=pl.BlockSpec((1, HGRN_C, w), lambda bb, c: (bb, c, 0)),
        scratch_shapes=[pltpu.VMEM((HGRN_HEADS, HGRN_DV, HGRN_DK), F32)],
        compiler_params=pltpu.CompilerParams(
            dimension_semantics=("parallel", "arbitrary"), vmem_limit_bytes=VMEM_LIMIT),
    )(u3, u3, u3, u3, one_minus_lb, norm_g, _hgrn_levels())


def _merge_kernel(ha_ref, ob_ref, oc_ref, gate_ref, x_ref, wa_ref, wb_ref, wc_ref, wo_ref, o_ref):
    gate = gate_ref[...]
    mix = None
    for n, (br_ref, w_ref) in enumerate(((ha_ref, wa_ref), (ob_ref, wb_ref), (oc_ref, wc_ref))):
        y = jnp.dot(br_ref[...], w_ref[...], preferred_element_type=F32)
        y = y * jax.nn.sigmoid(gate[:, n * D_MODEL:(n + 1) * D_MODEL].astype(F32))
        mix = y if mix is None else mix + y
    o_ref[...] = x_ref[...] + jnp.dot(mix.astype(BF16), wo_ref[...], preferred_element_type=F32)


def _merge(ha, ob, oc, u2, x2, wa, wb, wc, wo):
    t = x2.shape[0]
    br = lambda: pl.BlockSpec((TM_MERGE, 512), lambda i: (i, 0))
    wbr = lambda: pl.BlockSpec((512, D_MODEL), lambda i: (0, 0))
    return pl.pallas_call(
        _merge_kernel,
        out_shape=jax.ShapeDtypeStruct((t, D_MODEL), F32),
        grid=(t // TM_MERGE,),
        in_specs=[br(), br(), br(),
                  pl.BlockSpec((TM_MERGE, 3 * D_MODEL), lambda i: (i, COL_GATE // (3 * D_MODEL))),
                  pl.BlockSpec((TM_MERGE, D_MODEL), lambda i: (i, 0)),
                  wbr(), wbr(), wbr(),
                  pl.BlockSpec((D_MODEL, D_MODEL), lambda i: (0, 0))],
        out_specs=pl.BlockSpec((TM_MERGE, D_MODEL), lambda i: (i, 0)),
        compiler_params=pltpu.CompilerParams(
            dimension_semantics=("parallel",), vmem_limit_bytes=VMEM_LIMIT),
    )(ha, ob, oc, u2, x2, wa, wb, wc, wo)


def _ffn_kernel(x_ref, g_ref, w1_ref, w2_ref, o_ref, h_ref):
    j = pl.program_id(1)

    @pl.when(j == 0)
    def _():
        x = x_ref[...]
        h_ref[...] = (_rms(x) * g_ref[...]).astype(BF16)
        o_ref[...] = x

    a = jnp.maximum(jnp.dot(h_ref[...], w1_ref[...], preferred_element_type=F32), 0.0)
    o_ref[...] += jnp.dot((a * a).astype(BF16), w2_ref[...], preferred_element_type=F32)


def _ffn(x2, g, w1, w2):
    t = x2.shape[0]
    return pl.pallas_call(
        _ffn_kernel,
        out_shape=jax.ShapeDtypeStruct((t, D_MODEL), F32),
        grid=(t // TM_FF, D_FF // TF_FF),
        in_specs=[
            pl.BlockSpec((TM_FF, D_MODEL), lambda i, j: (i, 0)),
            pl.BlockSpec((1, D_MODEL), lambda i, j: (0, 0)),
            pl.BlockSpec((D_MODEL, TF_FF), lambda i, j: (0, j)),
            pl.BlockSpec((TF_FF, D_MODEL), lambda i, j: (j, 0)),
        ],
        out_specs=pl.BlockSpec((TM_FF, D_MODEL), lambda i, j: (i, 0)),
        scratch_shapes=[pltpu.VMEM((TM_FF, D_MODEL), BF16)],
        compiler_params=pltpu.CompilerParams(
            dimension_semantics=("parallel", "arbitrary"), vmem_limit_bytes=VMEM_LIMIT),
    )(x2, g, w1, w2)


def _regroup_w_in(w):
    o = 0
    conv = w[:, o:o + 2 * CONV_DIM]; o += 2 * CONV_DIM
    cq = w[:, o:o + Q_RANK]; o += Q_RANK
    ckv = w[:, o:o + KV_RANK]; o += KV_RANK
    kr = w[:, o:o + ROPE_DIM]; o += ROPE_DIM
    hg = w[:, o:o + 4 * HGRN_HEADS * HGRN_DK]; o += 4 * HGRN_HEADS * HGRN_DK
    gate = w[:, o:]
    kr_slot = jnp.pad(kr, ((0, 0), (NOPE_DIM, LANE - QK_DIM)))
    return jnp.concatenate([gate, conv, cq, ckv, kr_slot, hg], axis=1).astype(BF16)


def _head_slots(w, per_head, start, width):
    r = w.shape[0]
    wh = w.reshape(r, MLA_HEADS, per_head)[:, :, start:start + width]
    wh = jnp.pad(wh, ((0, 0), (0, 0), (0, HEAD_SLOT - width)))
    return wh.reshape(r, MLA_HEADS * HEAD_SLOT)


def _rope_tables(lp):
    half = ROPE_DIM // 2
    pos = jnp.arange(lp, dtype=F32)
    inv_freq = ROPE_BASE ** (-jnp.arange(half, dtype=F32) / half)
    ang = pos[:, None] * inv_freq[None, :]
    cos, sin = jnp.cos(ang), jnp.sin(ang)
    ones = jnp.ones((lp, NOPE_DIM), F32)
    z16 = jnp.zeros((lp, half), F32)
    z64 = jnp.zeros((lp, NOPE_DIM), F32)
    tail = jnp.zeros((lp, LANE - QK_DIM), F32)
    rc = jnp.concatenate([ones, cos, cos, tail], axis=1)
    rs1 = jnp.concatenate([z64, -sin, z16, tail], axis=1)
    rs2 = jnp.concatenate([z64, z16, sin, tail], axis=1)
    return rc, rs1, rs2


def kernel(x, meta, norm1_g, w_in, conv_w, conv_b, conv_ln_g, conv_ln_b, w_conv_out, q_a_norm_g, w_uq, kv_a_norm_g, w_ukv, q_norm_g, k_norm_g, w_attn_out, hgrn_lb_logits, hgrn_norm_g, w_hgrn_out, w_out, norm2_g, w_ff1, w_ff2):
    b, seq, d = x.shape
    depth = w_in.shape[0]
    l = seq + N_META
    lp = -(-l // (3 * SEQ_ALIGN)) * (3 * SEQ_ALIGN)
    xc = jnp.concatenate([jnp.broadcast_to(meta[None].astype(x.dtype), (b, N_META, d)), x,
                          jnp.zeros((b, lp - l, d), x.dtype)], axis=1)
    x2 = xc.reshape(b * lp, d)

    rc, rs1, rs2 = _rope_tables(lp)
    p_lb = jax.nn.softmax(hgrn_lb_logits.astype(F32), axis=0)
    lower_bounds = jnp.cumsum(p_lb, axis=0) - p_lb[0:1]
    vone = jnp.zeros((MLA_HEADS, HEAD_SLOT), F32).at[:, V_DIM].set(1.0).reshape(1, -1)
    row = lambda a: a.astype(F32).reshape(1, -1)
    pad_qk = lambda g: jnp.pad(g.astype(F32), (0, LANE - QK_DIM)).reshape(1, LANE)

    for li in range(depth):
        u2 = _inproj(x2, row(norm1_g[li]), _regroup_w_in(w_in[li]))
        u3 = u2.reshape(b, lp, N_IN_PAD)
        ha = _conv_branch(u3, conv_w[li].astype(F32), row(conv_b[li]), row(conv_ln_g[li]),
                          row(conv_ln_b[li]))
        q, k, v = _mla_prep(
            u3, row(q_a_norm_g[li]), row(kv_a_norm_g[li]),
            _head_slots(w_uq[li], QK_DIM, 0, QK_DIM).astype(BF16),
            _head_slots(w_ukv[li], NOPE_DIM + V_DIM, 0, NOPE_DIM).astype(BF16),
            _head_slots(w_ukv[li], NOPE_DIM + V_DIM, NOPE_DIM, V_DIM).astype(BF16),
            pad_qk(q_norm_g[li]) * (QK_DIM ** -0.5 * LOG2E), pad_qk(k_norm_g[li]), rc, rs1, rs2, vone)
        ob = _attention(q, k, v)
        oc = _hgrn(u3, 1.0 - row(lower_bounds[li]), row(hgrn_norm_g[li]))
        x2 = _merge(ha.reshape(b * lp, -1), ob.reshape(b * lp, -1), oc.reshape(b * lp, -1), u2, x2,
                    w_conv_out[li].astype(BF16), w_attn_out[li].astype(BF16),
                    w_hgrn_out[li].astype(BF16), w_out[li].astype(BF16))
        x2 = _ffn(x2, row(norm2_g[li]), w_ff1[li].astype(BF16), w_ff2[li].astype(BF16))

    return x2.reshape(b, lp, d)[:, N_META:l]
```

```python
import functools

import jax
import jax.numpy as jnp
import numpy as np
from jax import lax
from jax.experimental import pallas as pl
from jax.experimental.pallas import tpu as pltpu

F32 = jnp.float32
BF16 = jnp.bfloat16

D_MODEL = 1024
N_META = 16
EPS = 1e-6
GATE_CLAMP = 1.0 - 1e-6
CONV_DIM = 512
CONV_K = 31
MLA_HEADS = 8
Q_RANK = 256
KV_RANK = 128
NOPE_DIM = 64
ROPE_DIM = 32
V_DIM = 64
QK_DIM = NOPE_DIM + ROPE_DIM
ROPE_BASE = 10000.0
HGRN_HEADS = 4
HGRN_DK = 128
HGRN_DV = 128
D_FF = 4096

LANE = 128
SEQ_ALIGN = 128
HEAD_SLOT = LANE

COL_GATE = 0
COL_CONV = 3072
COL_CQ = 4096
COL_CKV = 4352
COL_KROPE = 4480
COL_HGRN = 4608
N_IN_PAD = 6656

VMEM_LIMIT = 52 * 1024 * 1024

TM_IN = 512
TN_IN = 512
SUBLANE = 8
TL_CONV = 384
RC_CONV = 32
CONV_HALO = 32
CONV_SPAN = TL_CONV + CONV_HALO - SUBLANE
TL_PREP = 384
RC_PREP = 128
TQ = 384
ATTN_HG = 4
LOG2E = 1.4426950408889634
HGRN_C = 128
TM_MERGE = 512
TM_FF = 1024
TF_FF = 1024


def _rms(x, eps=EPS):
    return x * lax.rsqrt(jnp.mean(x * x, axis=-1, keepdims=True) + eps)


def _inproj_kernel(x_ref, g_ref, w_ref, o_ref):
    h = (_rms(x_ref[...]) * g_ref[...]).astype(BF16)
    for c in range(N_IN_PAD // TN_IN):
        cols = slice(c * TN_IN, (c + 1) * TN_IN)
        o_ref[:, cols] = jnp.dot(h, w_ref[:, cols], preferred_element_type=F32).astype(o_ref.dtype)


def _inproj(x2, g, w):
    t = x2.shape[0]
    return pl.pallas_call(
        _inproj_kernel,
        out_shape=jax.ShapeDtypeStruct((t, N_IN_PAD), BF16),
        grid=(t // TM_IN,),
        in_specs=[
            pl.BlockSpec((TM_IN, D_MODEL), lambda i: (i, 0)),
            pl.BlockSpec((1, D_MODEL), lambda i: (0, 0)),
            pl.BlockSpec((D_MODEL, N_IN_PAD), lambda i: (0, 0), pipeline_mode=pl.Buffered(1)),
        ],
        out_specs=pl.BlockSpec((TM_IN, N_IN_PAD), lambda i: (i, 0)),
        compiler_params=pltpu.CompilerParams(
            dimension_semantics=("parallel",), vmem_limit_bytes=VMEM_LIMIT),
    )(x2, g, w)


def _conv_kernel(a_ref, gt_ref, w_ref, cb_ref, lg_ref, lb_ref, o_ref, hbuf, sbuf):
    j = pl.program_id(1)

    @pl.when(j == 0)
    def _():
        hbuf[0:CONV_HALO, :] = jnp.zeros((CONV_HALO, CONV_DIM), F32)

    @pl.when(j > 0)
    def _():
        hbuf[0:CONV_HALO, :] = hbuf[TL_CONV:TL_CONV + CONV_HALO, :]

    a = a_ref[0].astype(F32)
    gt = gt_ref[0].astype(F32)
    hbuf[CONV_HALO:CONV_HALO + TL_CONV, :] = a * jax.nn.sigmoid(gt)

    for r in range(1, SUBLANE):
        sbuf[r - 1] = hbuf[r:r + CONV_SPAN, :]

    base = CONV_HALO - (CONV_K - 1)
    for c in range(TL_CONV // RC_CONV):
        r0 = c * RC_CONV
        acc = jnp.zeros((RC_CONV, CONV_DIM), F32) + cb_ref[...]
        for k in range(CONV_K):
            r = (base + k) % SUBLANE
            lo = r0 + base + k - r
            src = hbuf if r == 0 else sbuf.at[r - 1]
            wk = jnp.tile(w_ref[k * SUBLANE:(k + 1) * SUBLANE, :], (RC_CONV // SUBLANE, 1))
            acc = acc + wk * src[lo:lo + RC_CONV, :]
        mu = jnp.mean(acc, axis=-1, keepdims=True)
        xc = acc - mu
        y = xc * lax.rsqrt(jnp.mean(xc * xc, axis=-1, keepdims=True) + EPS)
        y = y * lg_ref[...] + lb_ref[...]
        o_ref[0, r0:r0 + RC_CONV, :] = (y * jax.nn.sigmoid(y)).astype(o_ref.dtype)


def _conv_branch(u3, conv_w, conv_b, ln_g, ln_b):
    b, lp, _ = u3.shape
    wblk = CONV_DIM
    return pl.pallas_call(
        _conv_kernel,
        out_shape=jax.ShapeDtypeStruct((b, lp, CONV_DIM), BF16),
        grid=(b, lp // TL_CONV),
        in_specs=[
            pl.BlockSpec((1, TL_CONV, wblk), lambda bb, j: (bb, j, COL_CONV // wblk)),
            pl.BlockSpec((1, TL_CONV, wblk), lambda bb, j: (bb, j, COL_CONV // wblk + 1)),
            pl.BlockSpec((CONV_K * SUBLANE, CONV_DIM), lambda bb, j: (0, 0)),
            pl.BlockSpec((1, CONV_DIM), lambda bb, j: (0, 0)),
            pl.BlockSpec((1, CONV_DIM), lambda bb, j: (0, 0)),
            pl.BlockSpec((1, CONV_DIM), lambda bb, j: (0, 0)),
        ],
        out_specs=pl.BlockSpec((1, TL_CONV, CONV_DIM), lambda bb, j: (bb, j, 0)),
        scratch_shapes=[pltpu.VMEM((CONV_HALO + TL_CONV, CONV_DIM), F32),
                        pltpu.VMEM((SUBLANE - 1, CONV_SPAN, CONV_DIM), F32)],
        compiler_params=pltpu.CompilerParams(
            dimension_semantics=("parallel", "arbitrary"), vmem_limit_bytes=VMEM_LIMIT),
    )(u3, u3, conv_w, conv_b, ln_g, ln_b)


def _rope(x, c, s1, s2):
    half = ROPE_DIM // 2
    return x * c + pltpu.roll(x, LANE - half, 1) * s1 + pltpu.roll(x, half, 1) * s2


def _lane_sumsq(x, ones):
    sq = x * x
    hi = sq.astype(BF16)
    lo = (sq - hi.astype(F32)).astype(BF16)
    return jnp.dot(jnp.concatenate([hi, lo], axis=1), ones, preferred_element_type=F32)


def _mla_prep_kernel(cq_ref, ckv_ref, kr_ref, qag_ref, kvag_ref, wuq_ref, wuqs_ref, wuk_ref, wuv_ref,
                     qng_ref, qngs_ref, kng_ref, rc_ref, rs1_ref, rs2_ref, vone_ref,
                     q_out, k_out, v_out):
    inv_d = 1.0 / QK_DIM
    ones = jnp.ones((2 * LANE, LANE), BF16)

    def chunk(c, carry):
        rows = pl.ds(pl.multiple_of(c * RC_PREP, RC_PREP), RC_PREP)
        cq = (_rms(cq_ref[0, rows, :].astype(F32)) * qag_ref[...]).astype(BF16)
        q = jnp.dot(cq, wuq_ref[...], preferred_element_type=F32)
        qs = jnp.dot(cq, wuqs_ref[...], preferred_element_type=F32)
        ckv = (_rms(ckv_ref[0, rows, :].astype(F32)) * kvag_ref[...]).astype(BF16)
        kn = jnp.dot(ckv, wuk_ref[...], preferred_element_type=F32)
        v = jnp.dot(ckv, wuv_ref[...], preferred_element_type=F32) + vone_ref[...]
        rc, rs1, rs2 = rc_ref[rows, :], rs1_ref[rows, :], rs2_ref[rows, :]
        q_c = qng_ref[...] * rc
        q_s = qngs_ref[...] * (rs1 + rs2)
        kr = kr_ref[0, rows, :].astype(F32)
        kr_ss = _lane_sumsq(kr, ones)
        kr_rot = _rope(kr * kng_ref[...], rc, rs1, rs2)
        for h in range(MLA_HEADS):
            sl = slice(h * HEAD_SLOT, (h + 1) * HEAD_SLOT)
            qh = q[:, sl]
            q_out[0, h, rows, :] = ((qh * q_c + qs[:, sl] * q_s) * lax.rsqrt(
                _lane_sumsq(qh, ones) * inv_d + EPS)).astype(q_out.dtype)
            kh = kn[:, sl]
            k_out[0, h, rows, :] = ((kh * kng_ref[...] + kr_rot) * lax.rsqrt(
                (_lane_sumsq(kh, ones) + kr_ss) * inv_d + EPS)).astype(k_out.dtype)
            v_out[0, h, rows, :] = v[:, sl].astype(v_out.dtype)
        return carry

    lax.fori_loop(0, TL_PREP // RC_PREP, chunk, 0)


def _mla_prep(u3, qag, kvag, wuq, wuqs, wuk, wuv, qng, qngs, kng, rc, rs1, rs2, vone):
    b, lp, _ = u3.shape
    hw = MLA_HEADS * HEAD_SLOT
    const = lambda shape: pl.BlockSpec(shape, lambda bb, j: (0, 0))
    tab = pl.BlockSpec((TL_PREP, LANE), lambda bb, j: (j, 0))
    out = jax.ShapeDtypeStruct((b, MLA_HEADS, lp, HEAD_SLOT), BF16)
    ospec = pl.BlockSpec((1, MLA_HEADS, TL_PREP, HEAD_SLOT), lambda bb, j: (bb, 0, j, 0))
    return pl.pallas_call(
        _mla_prep_kernel,
        out_shape=(out, out, out),
        grid=(b, lp // TL_PREP),
        in_specs=[
            pl.BlockSpec((1, TL_PREP, Q_RANK), lambda bb, j: (bb, j, COL_CQ // Q_RANK)),
            pl.BlockSpec((1, TL_PREP, KV_RANK), lambda bb, j: (bb, j, COL_CKV // KV_RANK)),
            pl.BlockSpec((1, TL_PREP, LANE), lambda bb, j: (bb, j, COL_KROPE // LANE)),
            const((1, Q_RANK)), const((1, KV_RANK)),
            const((Q_RANK, hw)), const((Q_RANK, hw)), const((KV_RANK, hw)), const((KV_RANK, hw)),
            const((1, LANE)), const((1, LANE)), const((1, LANE)),
            tab, tab, tab,
            const((1, hw)),
        ],
        out_specs=(ospec, ospec, ospec),
        compiler_params=pltpu.CompilerParams(
            dimension_semantics=("parallel", "parallel"), vmem_limit_bytes=VMEM_LIMIT),
    )(u3, u3, u3, qag, kvag, wuq, wuqs, wuk, wuv, qng, qngs, kng, rc, rs1, rs2, vone)


def _attn_kernel(q_ref, k_ref, v_ref, o_ref, s_ref, m_ref, acc_ref):
    i = pl.program_id(2)
    nt = (((1,), (1,)), ((), ()))
    heads = range(ATTN_HG)
    qs = [q_ref[0, h] for h in heads]
    pair = 2 * TQ
    odd = (i & 1) == 1

    def scores(h, start, width):
        kt = k_ref[0, h, pl.ds(pl.multiple_of(start, TQ), width), :]
        return lax.dot_general(qs[h], kt, nt, preferred_element_type=F32)

    def fold(s):
        m = s[:, 0:LANE]
        for c in range(1, s.shape[1] // LANE):
            m = jnp.maximum(m, s[:, c * LANE:(c + 1) * LANE])
        return m

    def pv(h, s, start, width):
        p = jnp.exp2(s - ms[h]).astype(BF16)
        vt = v_ref[0, h, pl.ds(pl.multiple_of(start, TQ), width), :]
        return jnp.dot(p, vt, preferred_element_type=F32)

    def pass1(p, mloc):
        out = []
        for h in heads:
            s = scores(h, p * pair, pair)
            s_ref[h, 2 * p] = s[:, :TQ]
            s_ref[h, 2 * p + 1] = s[:, TQ:]
            out.append(jnp.maximum(mloc[h], fold(s)))
        return tuple(out)

    mloc = lax.fori_loop(0, lax.shift_right_logical(i, 1), pass1,
                         tuple(jnp.full((TQ, LANE), -1e30, F32) for _ in heads))
    keep = (lax.broadcasted_iota(jnp.int32, (TQ, TQ), 0)
            >= lax.broadcasted_iota(jnp.int32, (TQ, TQ), 1))

    @pl.when(odd)
    def _():
        for h in heads:
            s = scores(h, (i - 1) * TQ, pair)
            lo, hi = s[:, :TQ], jnp.where(keep, s[:, TQ:], -1e30)
            s_ref[h, i - 1] = lo
            s_ref[h, i] = hi
            m_ref[h] = jnp.maximum(mloc[h], jnp.maximum(fold(lo), fold(hi)))

    @pl.when(jnp.logical_not(odd))
    def _():
        for h in heads:
            s = jnp.where(keep, scores(h, i * TQ, TQ), -1e30)
            s_ref[h, i] = s
            m_ref[h] = jnp.maximum(mloc[h], fold(s))

    ms = [jnp.max(m_ref[h], axis=-1, keepdims=True) for h in heads]

    def pass2(p, acc):
        return tuple(
            acc[h] + pv(h, jnp.concatenate([s_ref[h, 2 * p], s_ref[h, 2 * p + 1]], axis=1),
                        p * pair, pair)
            for h in heads)

    acc = lax.fori_loop(0, lax.shift_right_logical(i + 1, 1), pass2,
                        tuple(jnp.zeros((TQ, HEAD_SLOT), F32) for _ in heads))
    for h in heads:
        acc_ref[h] = acc[h]

    @pl.when(jnp.logical_not(odd))
    def _():
        for h in heads:
            acc_ref[h] += pv(h, s_ref[h, i], i * TQ, TQ)

    first = lax.broadcasted_iota(jnp.int32, (TQ, HEAD_SLOT), 1) < V_DIM
    for hp in range(ATTN_HG // 2):
        a, b = acc_ref[2 * hp], acc_ref[2 * hp + 1]
        den = pltpu.roll(jnp.where(first, b, a), V_DIM, 1)
        o_ref[0, :, hp * HEAD_SLOT:(hp + 1) * HEAD_SLOT] = (
            jnp.where(first, a, b) / den).astype(o_ref.dtype)


def _attention(q, k, v):
    b, nh, lp, hs = q.shape
    nq = lp // TQ
    return pl.pallas_call(
        _attn_kernel,
        out_shape=jax.ShapeDtypeStruct((b, lp, nh * V_DIM), BF16),
        grid=(b, nh // ATTN_HG, nq),
        in_specs=[
            pl.BlockSpec((1, ATTN_HG, TQ, hs), lambda bb, g, i: (bb, g, i, 0)),
            pl.BlockSpec((1, ATTN_HG, lp, hs), lambda bb, g, i: (bb, g, 0, 0)),
            pl.BlockSpec((1, ATTN_HG, lp, hs), lambda bb, g, i: (bb, g, 0, 0)),
        ],
        out_specs=pl.BlockSpec((1, TQ, ATTN_HG * V_DIM), lambda bb, g, i: (bb, i, g)),
        scratch_shapes=[pltpu.VMEM((ATTN_HG, nq, TQ, TQ), F32),
                        pltpu.VMEM((ATTN_HG, TQ, LANE), F32),
                        pltpu.VMEM((ATTN_HG, TQ, HEAD_SLOT), F32)],
        compiler_params=pltpu.CompilerParams(
            dimension_semantics=("parallel", "parallel", "arbitrary"), vmem_limit_bytes=VMEM_LIMIT),
    )(q, k, v)


def _hgrn_kernel(q_ref, f_ref, i_ref, g_ref, oml_ref, ng_ref, lvl_ref, o_ref, st_ref):
    C = HGRN_C

    @pl.when(pl.program_id(1) == 0)
    def _():
        st_ref[...] = jnp.zeros(st_ref.shape, F32)

    lvl = lvl_ref[...]
    rowid = lax.broadcasted_iota(jnp.int32, (C, HGRN_DK), 0)
    nt = (((1,), (1,)), ((), ()))
    tn = (((0,), (0,)), ((), ()))
    for h in range(HGRN_HEADS):
        sl = slice(h * HGRN_DK, (h + 1) * HGRN_DK)
        q = q_ref[0, :, sl].astype(F32)
        fr = f_ref[0, :, sl].astype(F32)
        kk = oml_ref[:, sl] * jax.nn.sigmoid(-fr)
        lf = jnp.log1p(-jnp.minimum(kk, GATE_CLAMP))
        iv = i_ref[0, :, sl].astype(F32)
        v = (iv * jax.nn.sigmoid(iv)).astype(BF16)

        fwd = lf
        bwd = jnp.zeros_like(lf)
        a = jnp.where(lvl == 0, lax.dot_general(q.astype(BF16), kk.astype(BF16), nt,
                                                preferred_element_type=F32), 0.0)
        n, level = 1, 1
        while n < C:
            qs = (q * jnp.exp(fwd)).astype(BF16)
            ks = (kk * jnp.exp(bwd)).astype(BF16)
            a = a + jnp.where(lvl == level,
                              lax.dot_general(qs, ks, nt, preferred_element_type=F32), 0.0)
            tot = fwd + bwd
            upper = (rowid & n) != 0
            fwd = fwd + jnp.where(upper, pltpu.roll(tot, n, 0), 0.0)
            bwd = bwd + jnp.where(upper, 0.0, pltpu.roll(tot, C - n, 0))
            n, level = 2 * n, level + 1

        st = st_ref[h]
        o = jnp.dot(a.astype(BF16), v, preferred_element_type=F32)
        o = o + lax.dot_general((q * jnp.exp(fwd)).astype(BF16), st.astype(BF16), nt,
                                preferred_element_type=F32)
        ks = (kk * jnp.exp(bwd)).astype(BF16)
        st_ref[h] = st * jnp.exp(fwd[C - 1:C, :]) + lax.dot_general(
            v, ks, tn, preferred_element_type=F32)

        o = _rms(o) * ng_ref[:, sl]
        gv = g_ref[0, :, sl].astype(F32)
        o_ref[0, :, sl] = (o * (gv * jax.nn.sigmoid(gv))).astype(o_ref.dtype)


def _hgrn_levels():
    t = np.arange(HGRN_C)[:, None]
    s = np.arange(HGRN_C)[None, :]
    x = t ^ s
    lvl = np.where(x == 0, 0, np.floor(np.log2(np.maximum(x, 1))).astype(np.int64) + 1)
    return jnp.asarray(np.where(s > t, -1, lvl), dtype=jnp.int32)


def _hgrn(u3, one_minus_lb, norm_g):
    b, lp, _ = u3.shape
    w = HGRN_HEADS * HGRN_DK
    blk = lambda n: pl.BlockSpec((1, HGRN_C, w), lambda bb, c: (bb, c, COL_HGRN // w + n))
    const = lambda shape: pl.BlockSpec(shape, lambda bb, c: (0, 0))
    return pl.pallas_call(
        _hgrn_kernel,
        out_shape=jax.ShapeDtypeStruct((b, lp, w), BF16),
        grid=(b, lp // HGRN_C),
        in_specs=[blk(0), blk(1), blk(2), blk(3), const((1, w)), const((1, w)),
                  const((HGRN_C, HGRN_C))],
        out_specs=pl.BlockSpec((1, HGRN_C, w), lambda bb, c: (bb, c, 0)),
        scratch_shapes=[pltpu.VMEM((HGRN_HEADS, HGRN_DV, HGRN_DK), F32)],
        compiler_params=pltpu.CompilerParams(
            dimension_semantics=("parallel", "arbitrary"), vmem_limit_bytes=VMEM_LIMIT),
    )(u3, u3, u3, u3, one_minus_lb, norm_g, _hgrn_levels())


def _merge_kernel(ha_ref, ob_ref, oc_ref, gate_ref, x_ref, wa_ref, wb_ref, wc_ref, wo_ref, o_ref):
    gate = gate_ref[...]
    mix = None
    for n, (br_ref, w_ref) in enumerate(((ha_ref, wa_ref), (ob_ref, wb_ref), (oc_ref, wc_ref))):
        y = jnp.dot(br_ref[...], w_ref[...], preferred_element_type=F32)
        y = y * jax.nn.sigmoid(gate[:, n * D_MODEL:(n + 1) * D_MODEL].astype(F32))
        mix = y if mix is None else mix + y
    o_ref[...] = x_ref[...] + jnp.dot(mix.astype(BF16), wo_ref[...], preferred_element_type=F32)


def _merge(ha, ob, oc, u2, x2, wa, wb, wc, wo):
    t = x2.shape[0]
    br = lambda: pl.BlockSpec((TM_MERGE, 512), lambda i: (i, 0))
    wbr = lambda: pl.BlockSpec((512, D_MODEL), lambda i: (0, 0))
    return pl.pallas_call(
        _merge_kernel,
        out_shape=jax.ShapeDtypeStruct((t, D_MODEL), F32),
        grid=(t // TM_MERGE,),
        in_specs=[br(), br(), br(),
                  pl.BlockSpec((TM_MERGE, 3 * D_MODEL), lambda i: (i, COL_GATE // (3 * D_MODEL))),
                  pl.BlockSpec((TM_MERGE, D_MODEL), lambda i: (i, 0)),
                  wbr(), wbr(), wbr(),
                  pl.BlockSpec((D_MODEL, D_MODEL), lambda i: (0, 0))],
        out_specs=pl.BlockSpec((TM_MERGE, D_MODEL), lambda i: (i, 0)),
        compiler_params=pltpu.CompilerParams(
            dimension_semantics=("parallel",), vmem_limit_bytes=VMEM_LIMIT),
    )(ha, ob, oc, u2, x2, wa, wb, wc, wo)


def _ffn_kernel(x_ref, g_ref, w1_ref, w2_ref, o_ref, h_ref):
    j = pl.program_id(1)

    @pl.when(j == 0)
    def _():
        x = x_ref[...]
        h_ref[...] = (_rms(x) * g_ref[...]).astype(BF16)
        o_ref[...] = x

    a = jnp.maximum(jnp.dot(h_ref[...], w1_ref[...], preferred_element_type=F32), 0.0)
    o_ref[...] += jnp.dot((a * a).astype(BF16), w2_ref[...], preferred_element_type=F32)


def _ffn(x2, g, w1, w2):
    t = x2.shape[0]
    return pl.pallas_call(
        _ffn_kernel,
        out_shape=jax.ShapeDtypeStruct((t, D_MODEL), F32),
        grid=(t // TM_FF, D_FF // TF_FF),
        in_specs=[
            pl.BlockSpec((TM_FF, D_MODEL), lambda i, j: (i, 0)),
            pl.BlockSpec((1, D_MODEL), lambda i, j: (0, 0)),
            pl.BlockSpec((D_MODEL, TF_FF), lambda i, j: (0, j)),
            pl.BlockSpec((TF_FF, D_MODEL), lambda i, j: (j, 0)),
        ],
        out_specs=pl.BlockSpec((TM_FF, D_MODEL), lambda i, j: (i, 0)),
        scratch_shapes=[pltpu.VMEM((TM_FF, D_MODEL), BF16)],
        compiler_params=pltpu.CompilerParams(
            dimension_semantics=("parallel", "arbitrary"), vmem_limit_bytes=VMEM_LIMIT),
    )(x2, g, w1, w2)


def _regroup_w_in(w):
    o = 0
    conv = w[:, o:o + 2 * CONV_DIM]; o += 2 * CONV_DIM
    cq = w[:, o:o + Q_RANK]; o += Q_RANK
    ckv = w[:, o:o + KV_RANK]; o += KV_RANK
    kr = w[:, o:o + ROPE_DIM]; o += ROPE_DIM
    hg = w[:, o:o + 4 * HGRN_HEADS * HGRN_DK]; o += 4 * HGRN_HEADS * HGRN_DK
    gate = w[:, o:]
    kr_slot = jnp.pad(kr, ((0, 0), (NOPE_DIM, LANE - QK_DIM)))
    return jnp.concatenate([gate, conv, cq, ckv, kr_slot, hg], axis=1).astype(BF16)


def _head_slots(w, per_head, start, width):
    r = w.shape[0]
    wh = w.reshape(r, MLA_HEADS, per_head)[:, :, start:start + width]
    wh = jnp.pad(wh, ((0, 0), (0, 0), (0, HEAD_SLOT - width)))
    return wh.reshape(r, MLA_HEADS * HEAD_SLOT)


def _swap_rope_halves(a):
    half = ROPE_DIM // 2
    return jnp.concatenate([jnp.zeros_like(a[..., :NOPE_DIM]), a[..., NOPE_DIM + half:],
                            a[..., NOPE_DIM:NOPE_DIM + half]], axis=-1)


def _v_slots(w_ukv):
    r = w_ukv.shape[0]
    wv = w_ukv.reshape(r, MLA_HEADS // 2, 2, NOPE_DIM + V_DIM)[..., NOPE_DIM:]
    z = jnp.zeros_like(wv[:, :, 0])
    slots = jnp.stack([jnp.concatenate([wv[:, :, 0], z], axis=-1),
                       jnp.concatenate([z, wv[:, :, 1]], axis=-1)], axis=2)
    ones = jnp.ones((MLA_HEADS // 2, V_DIM), F32)
    zo = jnp.zeros_like(ones)
    vone = jnp.stack([jnp.concatenate([zo, ones], axis=-1),
                      jnp.concatenate([ones, zo], axis=-1)], axis=1)
    return slots.reshape(r, MLA_HEADS * HEAD_SLOT), vone.reshape(1, MLA_HEADS * HEAD_SLOT)


def _rope_tables(lp):
    half = ROPE_DIM // 2
    pos = jnp.arange(lp, dtype=F32)
    inv_freq = ROPE_BASE ** (-jnp.arange(half, dtype=F32) / half)
    ang = pos[:, None] * inv_freq[None, :]
    cos, sin = jnp.cos(ang), jnp.sin(ang)
    ones = jnp.ones((lp, NOPE_DIM), F32)
    z16 = jnp.zeros((lp, half), F32)
    z64 = jnp.zeros((lp, NOPE_DIM), F32)
    tail = jnp.zeros((lp, LANE - QK_DIM), F32)
    rc = jnp.concatenate([ones, cos, cos, tail], axis=1)
    rs1 = jnp.concatenate([z64, -sin, z16, tail], axis=1)
    rs2 = jnp.concatenate([z64, z16, sin, tail], axis=1)
    return rc, rs1, rs2


def kernel(x, meta, norm1_g, w_in, conv_w, conv_b, conv_ln_g, conv_ln_b, w_conv_out, q_a_norm_g, w_uq, kv_a_norm_g, w_ukv, q_norm_g, k_norm_g, w_attn_out, hgrn_lb_logits, hgrn_norm_g, w_hgrn_out, w_out, norm2_g, w_ff1, w_ff2):
    b, seq, d = x.shape
    depth = w_in.shape[0]
    l = seq + N_META
    lp = -(-l // (3 * SEQ_ALIGN)) * (3 * SEQ_ALIGN)
    xc = jnp.concatenate([jnp.broadcast_to(meta[None].astype(x.dtype), (b, N_META, d)), x,
                          jnp.zeros((b, lp - l, d), x.dtype)], axis=1)
    x2 = xc.reshape(b * lp, d)

    rc, rs1, rs2 = _rope_tables(lp)
    p_lb = jax.nn.softmax(hgrn_lb_logits.astype(F32), axis=0)
    lower_bounds = jnp.cumsum(p_lb, axis=0) - p_lb[0:1]
    row = lambda a: a.astype(F32).reshape(1, -1)
    pad_qk = lambda g: jnp.pad(g.astype(F32), (0, LANE - QK_DIM)).reshape(1, LANE)

    for li in range(depth):
        u2 = _inproj(x2, row(norm1_g[li]), _regroup_w_in(w_in[li]))
        u3 = u2.reshape(b, lp, N_IN_PAD)
        ha = _conv_branch(u3, jnp.repeat(conv_w[li].astype(F32), SUBLANE, axis=0),
                          row(conv_b[li]), row(conv_ln_g[li]),
                          row(conv_ln_b[li]))
        wuv, vone = _v_slots(w_ukv[li])
        w_uq_heads = w_uq[li].reshape(Q_RANK, MLA_HEADS, QK_DIM)
        w_uq_swapped = _swap_rope_halves(w_uq_heads).reshape(Q_RANK, MLA_HEADS * QK_DIM)
        q_gain = q_norm_g[li].astype(F32) * (QK_DIM ** -0.5 * LOG2E)
        q, k, v = _mla_prep(
            u3, row(q_a_norm_g[li]), row(kv_a_norm_g[li]),
            _head_slots(w_uq[li], QK_DIM, 0, QK_DIM).astype(BF16),
            _head_slots(w_uq_swapped, QK_DIM, 0, QK_DIM).astype(BF16),
            _head_slots(w_ukv[li], NOPE_DIM + V_DIM, 0, NOPE_DIM).astype(BF16),
            wuv.astype(BF16),
            pad_qk(q_gain), pad_qk(_swap_rope_halves(q_gain)), pad_qk(k_norm_g[li]),
            rc, rs1, rs2, vone)
        ob = _attention(q, k, v)
        oc = _hgrn(u3, 1.0 - row(lower_bounds[li]), row(hgrn_norm_g[li]))
        x2 = _merge(ha.reshape(b * lp, -1), ob.reshape(b * lp, -1), oc.reshape(b * lp, -1), u2, x2,
                    w_conv_out[li].astype(BF16), w_attn_out[li].astype(BF16),
                    w_hgrn_out[li].astype(BF16), w_out[li].astype(BF16))
        x2 = _ffn(x2, row(norm2_g[li]), w_ff1[li].astype(BF16), w_ff2[li].astype(BF16))

    return x2.reshape(b, lp, d)[:, N_META:l]
```

```python
import functools

import jax
import jax.numpy as jnp
import numpy as np
from jax import lax
from jax.experimental import pallas as pl
from jax.experimental.pallas import tpu as pltpu

F32 = jnp.float32
BF16 = jnp.bfloat16

D_MODEL = 1024
N_META = 16
EPS = 1e-6
GATE_CLAMP = 1.0 - 1e-6
CONV_DIM = 512
CONV_K = 31
MLA_HEADS = 8
Q_RANK = 256
KV_RANK = 128
NOPE_DIM = 64
ROPE_DIM = 32
V_DIM = 64
QK_DIM = NOPE_DIM + ROPE_DIM
ROPE_BASE = 10000.0
HGRN_HEADS = 4
HGRN_DK = 128
HGRN_DV = 128
D_FF = 4096

LANE = 128
SEQ_ALIGN = 128
HEAD_SLOT = LANE

COL_GATE = 0
COL_CONV = 3072
COL_CQ = 4096
COL_CKV = 4352
COL_KROPE = 4480
COL_HGRN = 4608
N_IN_PAD = 6656

VMEM_LIMIT = 52 * 1024 * 1024

TM_IN = 512
TN_IN = 512
SUBLANE = 8
TL_CONV = 384
RC_CONV = 32
CONV_HALO = 32
CONV_SPAN = TL_CONV + CONV_HALO - SUBLANE
TL_PREP = 384
RC_PREP = 128
TQ = 384
ATTN_HG = 4
LOG2E = 1.4426950408889634
HGRN_C = 128
TM_MERGE = 1024
TM_FF = 1024
TF_FF = 1024


def _rms(x, eps=EPS):
    return x * lax.rsqrt(jnp.mean(x * x, axis=-1, keepdims=True) + eps)


def _sigmoid(x):
    return 0.5 * jnp.tanh(0.5 * x) + 0.5


def _silu(x):
    return x * _sigmoid(x)


def _in_act(col):
    hw = HGRN_HEADS * HGRN_DK
    if col < COL_CONV or COL_CONV + CONV_DIM <= col < COL_CQ:
        return _sigmoid
    if col >= COL_HGRN + 2 * hw:
        return _silu
    return None


def _inproj_kernel(x_ref, g_ref, w_ref, o_ref):
    h = (_rms(x_ref[...]) * g_ref[...]).astype(BF16)
    for c in range(N_IN_PAD // TN_IN):
        cols = slice(c * TN_IN, (c + 1) * TN_IN)
        y = jnp.dot(h, w_ref[:, cols], preferred_element_type=F32)
        act = _in_act(c * TN_IN)
        o_ref[:, cols] = (y if act is None else act(y)).astype(o_ref.dtype)


def _inproj(x2, g, w):
    t = x2.shape[0]
    return pl.pallas_call(
        _inproj_kernel,
        out_shape=jax.ShapeDtypeStruct((t, N_IN_PAD), BF16),
        grid=(t // TM_IN,),
        in_specs=[
            pl.BlockSpec((TM_IN, D_MODEL), lambda i: (i, 0)),
            pl.BlockSpec((1, D_MODEL), lambda i: (0, 0)),
            pl.BlockSpec((D_MODEL, N_IN_PAD), lambda i: (0, 0), pipeline_mode=pl.Buffered(1)),
        ],
        out_specs=pl.BlockSpec((TM_IN, N_IN_PAD), lambda i: (i, 0)),
        compiler_params=pltpu.CompilerParams(
            dimension_semantics=("parallel",), vmem_limit_bytes=VMEM_LIMIT),
    )(x2, g, w)


def _conv_kernel(a_ref, gt_ref, w_ref, cb_ref, lg_ref, lb_ref, o_ref, hbuf, sbuf):
    j = pl.program_id(1)

    @pl.when(j == 0)
    def _():
        hbuf[0:CONV_HALO, :] = jnp.zeros((CONV_HALO, CONV_DIM), F32)

    @pl.when(j > 0)
    def _():
        hbuf[0:CONV_HALO, :] = hbuf[TL_CONV:TL_CONV + CONV_HALO, :]

    hbuf[CONV_HALO:CONV_HALO + TL_CONV, :] = a_ref[0].astype(F32) * gt_ref[0].astype(F32)

    for r in range(1, SUBLANE):
        sbuf[r - 1] = hbuf[r:r + CONV_SPAN, :]

    base = CONV_HALO - (CONV_K - 1)
    for c in range(TL_CONV // RC_CONV):
        r0 = c * RC_CONV
        acc = jnp.zeros((RC_CONV, CONV_DIM), F32) + cb_ref[...]
        for k in range(CONV_K):
            r = (base + k) % SUBLANE
            lo = r0 + base + k - r
            src = hbuf if r == 0 else sbuf.at[r - 1]
            wk = jnp.tile(w_ref[k * SUBLANE:(k + 1) * SUBLANE, :], (RC_CONV // SUBLANE, 1))
            acc = acc + wk * src[lo:lo + RC_CONV, :]
        mu = jnp.mean(acc, axis=-1, keepdims=True)
        xc = acc - mu
        y = xc * lax.rsqrt(jnp.mean(xc * xc, axis=-1, keepdims=True) + EPS)
        y = y * lg_ref[...] + lb_ref[...]
        o_ref[0, r0:r0 + RC_CONV, :] = (y * jax.nn.sigmoid(y)).astype(o_ref.dtype)


def _conv_branch(u3, conv_w, conv_b, ln_g, ln_b):
    b, lp, _ = u3.shape
    wblk = CONV_DIM
    return pl.pallas_call(
        _conv_kernel,
        out_shape=jax.ShapeDtypeStruct((b, lp, CONV_DIM), BF16),
        grid=(b, lp // TL_CONV),
        in_specs=[
            pl.BlockSpec((1, TL_CONV, wblk), lambda bb, j: (bb, j, COL_CONV // wblk)),
            pl.BlockSpec((1, TL_CONV, wblk), lambda bb, j: (bb, j, COL_CONV // wblk + 1)),
            pl.BlockSpec((CONV_K * SUBLANE, CONV_DIM), lambda bb, j: (0, 0)),
            pl.BlockSpec((1, CONV_DIM), lambda bb, j: (0, 0)),
            pl.BlockSpec((1, CONV_DIM), lambda bb, j: (0, 0)),
            pl.BlockSpec((1, CONV_DIM), lambda bb, j: (0, 0)),
        ],
        out_specs=pl.BlockSpec((1, TL_CONV, CONV_DIM), lambda bb, j: (bb, j, 0)),
        scratch_shapes=[pltpu.VMEM((CONV_HALO + TL_CONV, CONV_DIM), F32),
                        pltpu.VMEM((SUBLANE - 1, CONV_SPAN, CONV_DIM), F32)],
        compiler_params=pltpu.CompilerParams(
            dimension_semantics=("parallel", "arbitrary"), vmem_limit_bytes=VMEM_LIMIT),
    )(u3, u3, conv_w, conv_b, ln_g, ln_b)


def _rope(x, c, s1, s2):
    half = ROPE_DIM // 2
    return x * c + pltpu.roll(x, LANE - half, 1) * s1 + pltpu.roll(x, half, 1) * s2


def _lane_sumsq(x, ones):
    sq = x * x
    hi = sq.astype(BF16)
    lo = (sq - hi.astype(F32)).astype(BF16)
    return jnp.dot(jnp.concatenate([hi, lo], axis=1), ones, preferred_element_type=F32)


def _mla_prep_kernel(cq_ref, ckv_ref, kr_ref, qag_ref, kvag_ref, wuq_ref, wuqs_ref, wuk_ref, wuv_ref,
                     qng_ref, qngs_ref, kng_ref, rc_ref, rs1_ref, rs2_ref, vone_ref,
                     q_out, k_out, v_out):
    inv_d = 1.0 / QK_DIM
    ones = jnp.ones((2 * LANE, LANE), BF16)

    def chunk(c, carry):
        rows = pl.ds(pl.multiple_of(c * RC_PREP, RC_PREP), RC_PREP)
        cq = (_rms(cq_ref[0, rows, :].astype(F32)) * qag_ref[...]).astype(BF16)
        q = jnp.dot(cq, wuq_ref[...], preferred_element_type=F32)
        qs = jnp.dot(cq, wuqs_ref[...], preferred_element_type=F32)
        ckv = (_rms(ckv_ref[0, rows, :].astype(F32)) * kvag_ref[...]).astype(BF16)
        kn = jnp.dot(ckv, wuk_ref[...], preferred_element_type=F32)
        v = jnp.dot(ckv, wuv_ref[...], preferred_element_type=F32) + vone_ref[...]
        rc, rs1, rs2 = rc_ref[rows, :], rs1_ref[rows, :], rs2_ref[rows, :]
        q_c = qng_ref[...] * rc
        q_s = qngs_ref[...] * (rs1 + rs2)
        kr = kr_ref[0, rows, :].astype(F32)
        kr_ss = _lane_sumsq(kr, ones)
        kr_rot = _rope(kr * kng_ref[...], rc, rs1, rs2)
        for h in range(MLA_HEADS):
            sl = slice(h * HEAD_SLOT, (h + 1) * HEAD_SLOT)
            qh = q[:, sl]
            q_out[0, h, rows, :] = ((qh * q_c + qs[:, sl] * q_s) * lax.rsqrt(
                _lane_sumsq(qh, ones) * inv_d + EPS)).astype(q_out.dtype)
            kh = kn[:, sl]
            k_out[0, h, rows, :] = ((kh * kng_ref[...] + kr_rot) * lax.rsqrt(
                (_lane_sumsq(kh, ones) + kr_ss) * inv_d + EPS)).astype(k_out.dtype)
            v_out[0, h, rows, :] = v[:, sl].astype(v_out.dtype)
        return carry

    lax.fori_loop(0, TL_PREP // RC_PREP, chunk, 0)


def _mla_prep(u3, qag, kvag, wuq, wuqs, wuk, wuv, qng, qngs, kng, rc, rs1, rs2, vone):
    b, lp, _ = u3.shape
    hw = MLA_HEADS * HEAD_SLOT
    const = lambda shape: pl.BlockSpec(shape, lambda bb, j: (0, 0))
    tab = pl.BlockSpec((TL_PREP, LANE), lambda bb, j: (j, 0))
    out = jax.ShapeDtypeStruct((b, MLA_HEADS, lp, HEAD_SLOT), BF16)
    ospec = pl.BlockSpec((1, MLA_HEADS, TL_PREP, HEAD_SLOT), lambda bb, j: (bb, 0, j, 0))
    return pl.pallas_call(
        _mla_prep_kernel,
        out_shape=(out, out, out),
        grid=(b, lp // TL_PREP),
        in_specs=[
            pl.BlockSpec((1, TL_PREP, Q_RANK), lambda bb, j: (bb, j, COL_CQ // Q_RANK)),
            pl.BlockSpec((1, TL_PREP, KV_RANK), lambda bb, j: (bb, j, COL_CKV // KV_RANK)),
            pl.BlockSpec((1, TL_PREP, LANE), lambda bb, j: (bb, j, COL_KROPE // LANE)),
            const((1, Q_RANK)), const((1, KV_RANK)),
            const((Q_RANK, hw)), const((Q_RANK, hw)), const((KV_RANK, hw)), const((KV_RANK, hw)),
            const((1, LANE)), const((1, LANE)), const((1, LANE)),
            tab, tab, tab,
            const((1, hw)),
        ],
        out_specs=(ospec, ospec, ospec),
        compiler_params=pltpu.CompilerParams(
            dimension_semantics=("parallel", "parallel"), vmem_limit_bytes=VMEM_LIMIT),
    )(u3, u3, u3, qag, kvag, wuq, wuqs, wuk, wuv, qng, qngs, kng, rc, rs1, rs2, vone)


def _attn_kernel(q_ref, k_ref, v_ref, o_ref, s_ref, m_ref, acc_ref):
    i = pl.program_id(2)
    nt = (((1,), (1,)), ((), ()))
    heads = range(ATTN_HG)
    qs = [q_ref[0, h] for h in heads]
    pair = 2 * TQ
    odd = (i & 1) == 1

    def scores(h, start, width):
        kt = k_ref[0, h, pl.ds(pl.multiple_of(start, TQ), width), :]
        return lax.dot_general(qs[h], kt, nt, preferred_element_type=F32)

    def fold(s):
        m = s[:, 0:LANE]
        for c in range(1, s.shape[1] // LANE):
            m = jnp.maximum(m, s[:, c * LANE:(c + 1) * LANE])
        return m

    def pv(h, s, start, width):
        p = jnp.exp2(s - ms[h]).astype(BF16)
        vt = v_ref[0, h, pl.ds(pl.multiple_of(start, TQ), width), :]
        return jnp.dot(p, vt, preferred_element_type=F32)

    def pass1(p, mloc):
        out = []
        for h in heads:
            s = scores(h, p * pair, pair)
            s_ref[h, 2 * p] = s[:, :TQ]
            s_ref[h, 2 * p + 1] = s[:, TQ:]
            out.append(jnp.maximum(mloc[h], fold(s)))
        return tuple(out)

    mloc = lax.fori_loop(0, lax.shift_right_logical(i, 1), pass1,
                         tuple(jnp.full((TQ, LANE), -1e30, F32) for _ in heads))
    keep = (lax.broadcasted_iota(jnp.int32, (TQ, TQ), 0)
            >= lax.broadcasted_iota(jnp.int32, (TQ, TQ), 1))

    @pl.when(odd)
    def _():
        for h in heads:
            s = scores(h, (i - 1) * TQ, pair)
            lo, hi = s[:, :TQ], jnp.where(keep, s[:, TQ:], -1e30)
            s_ref[h, i - 1] = lo
            s_ref[h, i] = hi
            m_ref[h] = jnp.maximum(mloc[h], jnp.maximum(fold(lo), fold(hi)))

    @pl.when(jnp.logical_not(odd))
    def _():
        for h in heads:
            s = jnp.where(keep, scores(h, i * TQ, TQ), -1e30)
            s_ref[h, i] = s
            m_ref[h] = jnp.maximum(mloc[h], fold(s))

    ms = [jnp.max(m_ref[h], axis=-1, keepdims=True) for h in heads]

    def pass2(p, acc):
        return tuple(
            acc[h] + pv(h, jnp.concatenate([s_ref[h, 2 * p], s_ref[h, 2 * p + 1]], axis=1),
                        p * pair, pair)
            for h in heads)

    acc = lax.fori_loop(0, lax.shift_right_logical(i + 1, 1), pass2,
                        tuple(jnp.zeros((TQ, HEAD_SLOT), F32) for _ in heads))
    for h in heads:
        acc_ref[h] = acc[h]

    @pl.when(jnp.logical_not(odd))
    def _():
        for h in heads:
            acc_ref[h] += pv(h, s_ref[h, i], i * TQ, TQ)

    first = lax.broadcasted_iota(jnp.int32, (TQ, HEAD_SLOT), 1) < V_DIM
    for hp in range(ATTN_HG // 2):
        a, b = acc_ref[2 * hp], acc_ref[2 * hp + 1]
        den = pltpu.roll(jnp.where(first, b, a), V_DIM, 1)
        o_ref[0, :, hp * HEAD_SLOT:(hp + 1) * HEAD_SLOT] = (
            jnp.where(first, a, b) / den).astype(o_ref.dtype)


def _attention(q, k, v):
    b, nh, lp, hs = q.shape
    nq = lp // TQ
    return pl.pallas_call(
        _attn_kernel,
        out_shape=jax.ShapeDtypeStruct((b, lp, nh * V_DIM), BF16),
        grid=(b, nh // ATTN_HG, nq),
        in_specs=[
            pl.BlockSpec((1, ATTN_HG, TQ, hs), lambda bb, g, i: (bb, g, i, 0)),
            pl.BlockSpec((1, ATTN_HG, lp, hs), lambda bb, g, i: (bb, g, 0, 0)),
            pl.BlockSpec((1, ATTN_HG, lp, hs), lambda bb, g, i: (bb, g, 0, 0)),
        ],
        out_specs=pl.BlockSpec((1, TQ, ATTN_HG * V_DIM), lambda bb, g, i: (bb, i, g)),
        scratch_shapes=[pltpu.VMEM((ATTN_HG, nq, TQ, TQ), F32),
                        pltpu.VMEM((ATTN_HG, TQ, LANE), F32),
                        pltpu.VMEM((ATTN_HG, TQ, HEAD_SLOT), F32)],
        compiler_params=pltpu.CompilerParams(
            dimension_semantics=("parallel", "parallel", "arbitrary"), vmem_limit_bytes=VMEM_LIMIT),
    )(q, k, v)


def _hgrn_kernel(q_ref, f_ref, i_ref, g_ref, oml_ref, ng_ref, lvl_ref, o_ref, st_ref):
    C = HGRN_C

    @pl.when(pl.program_id(1) == 0)
    def _():
        st_ref[...] = jnp.zeros(st_ref.shape, F32)

    lvl = lvl_ref[...]
    rowid = lax.broadcasted_iota(jnp.int32, (C, HGRN_DK), 0)
    nt = (((1,), (1,)), ((), ()))
    tn = (((0,), (0,)), ((), ()))
    for h in range(HGRN_HEADS):
        sl = slice(h * HGRN_DK, (h + 1) * HGRN_DK)
        q = q_ref[0, :, sl].astype(F32)
        fr = f_ref[0, :, sl].astype(F32)
        kk = oml_ref[:, sl] * jax.nn.sigmoid(-fr)
        lf = jnp.log1p(-jnp.minimum(kk, GATE_CLAMP))
        v = i_ref[0, :, sl]

        fwd = lf
        bwd = jnp.zeros_like(lf)
        a = jnp.where(lvl == 0, lax.dot_general(q.astype(BF16), kk.astype(BF16), nt,
                                                preferred_element_type=F32), 0.0)
        n, level = 1, 1
        while n < C:
            qs = (q * jnp.exp(fwd)).astype(BF16)
            ks = (kk * jnp.exp(bwd)).astype(BF16)
            a = a + jnp.where(lvl == level,
                              lax.dot_general(qs, ks, nt, preferred_element_type=F32), 0.0)
            tot = fwd + bwd
            upper = (rowid & n) != 0
            fwd = fwd + jnp.where(upper, pltpu.roll(tot, n, 0), 0.0)
            bwd = bwd + jnp.where(upper, 0.0, pltpu.roll(tot, C - n, 0))
            n, level = 2 * n, level + 1

        st = st_ref[h]
        o = jnp.dot(a.astype(BF16), v, preferred_element_type=F32)
        o = o + lax.dot_general((q * jnp.exp(fwd)).astype(BF16), st.astype(BF16), nt,
                                preferred_element_type=F32)
        ks = (kk * jnp.exp(bwd)).astype(BF16)
        st_ref[h] = st * jnp.exp(fwd[C - 1:C, :]) + lax.dot_general(
            v, ks, tn, preferred_element_type=F32)

        o = _rms(o) * ng_ref[:, sl]
        o_ref[0, :, sl] = (o * g_ref[0, :, sl].astype(F32)).astype(o_ref.dtype)


def _hgrn_levels():
    t = np.arange(HGRN_C)[:, None]
    s = np.arange(HGRN_C)[None, :]
    x = t ^ s
    lvl = np.where(x == 0, 0, np.floor(np.log2(np.maximum(x, 1))).astype(np.int64) + 1)
    return jnp.asarray(np.where(s > t, -1, lvl), dtype=jnp.int32)


def _hgrn(u3, one_minus_lb, norm_g):
    b, lp, _ = u3.shape
    w = HGRN_HEADS * HGRN_DK
    blk = lambda n: pl.BlockSpec((1, HGRN_C, w), lambda bb, c: (bb, c, COL_HGRN // w + n))
    const = lambda shape: pl.BlockSpec(shape, lambda bb, c: (0, 0))
    return pl.pallas_call(
        _hgrn_kernel,
        out_shape=jax.ShapeDtypeStruct((b, lp, w), BF16),
        grid=(b, lp // HGRN_C),
        in_specs=[blk(0), blk(1), blk(2), blk(3), const((1, w)), const((1, w)),
                  const((HGRN_C, HGRN_C))],
        out_specs=pl.BlockSpec((1, HGRN_C, w), lambda bb, c: (bb, c, 0)),
        scratch_shapes=[pltpu.VMEM((HGRN_HEADS, HGRN_DV, HGRN_DK), F32)],
        compiler_params=pltpu.CompilerParams(
            dimension_semantics=("parallel", "arbitrary"), vmem_limit_bytes=VMEM_LIMIT),
    )(u3, u3, u3, u3, one_minus_lb, norm_g, _hgrn_levels())


def _merge_kernel(ha_ref, ob_ref, oc_ref, gate_ref, x_ref, wa_ref, wb_ref, wc_ref, wo_ref, o_ref):
    mix = None
    for n, (br_ref, w_ref) in enumerate(((ha_ref, wa_ref), (ob_ref, wb_ref), (oc_ref, wc_ref))):
        y = jnp.dot(br_ref[...], w_ref[...], preferred_element_type=F32)
        y = y * gate_ref[:, n * D_MODEL:(n + 1) * D_MODEL].astype(F32)
        mix = y if mix is None else mix + y
    o_ref[...] = x_ref[...] + jnp.dot(mix.astype(BF16), wo_ref[...], preferred_element_type=F32)


def _merge(ha, ob, oc, u2, x2, wa, wb, wc, wo):
    t = x2.shape[0]
    br = lambda: pl.BlockSpec((TM_MERGE, 512), lambda i: (i, 0))
    wbr = lambda: pl.BlockSpec((512, D_MODEL), lambda i: (0, 0))
    return pl.pallas_call(
        _merge_kernel,
        out_shape=jax.ShapeDtypeStruct((t, D_MODEL), F32),
        grid=(t // TM_MERGE,),
        in_specs=[br(), br(), br(),
                  pl.BlockSpec((TM_MERGE, 3 * D_MODEL), lambda i: (i, COL_GATE // (3 * D_MODEL))),
                  pl.BlockSpec((TM_MERGE, D_MODEL), lambda i: (i, 0)),
                  wbr(), wbr(), wbr(),
                  pl.BlockSpec((D_MODEL, D_MODEL), lambda i: (0, 0))],
        out_specs=pl.BlockSpec((TM_MERGE, D_MODEL), lambda i: (i, 0)),
        compiler_params=pltpu.CompilerParams(
            dimension_semantics=("parallel",), vmem_limit_bytes=VMEM_LIMIT),
    )(ha, ob, oc, u2, x2, wa, wb, wc, wo)


def _ffn_kernel(x_ref, g_ref, w1_ref, w2_ref, o_ref):
    x = x_ref[...]
    h = (_rms(x) * g_ref[...]).astype(BF16)
    acc = x
    for c in range(D_FF // TF_FF):
        cols = slice(c * TF_FF, (c + 1) * TF_FF)
        a = jnp.maximum(jnp.dot(h, w1_ref[:, cols], preferred_element_type=F32), 0.0)
        acc = acc + jnp.dot((a * a).astype(BF16), w2_ref[cols, :], preferred_element_type=F32)
    o_ref[...] = acc


def _ffn(x2, g, w1, w2):
    t = x2.shape[0]
    return pl.pallas_call(
        _ffn_kernel,
        out_shape=jax.ShapeDtypeStruct((t, D_MODEL), F32),
        grid=(t // TM_FF,),
        in_specs=[
            pl.BlockSpec((TM_FF, D_MODEL), lambda i: (i, 0)),
            pl.BlockSpec((1, D_MODEL), lambda i: (0, 0)),
            pl.BlockSpec((D_MODEL, D_FF), lambda i: (0, 0), pipeline_mode=pl.Buffered(1)),
            pl.BlockSpec((D_FF, D_MODEL), lambda i: (0, 0), pipeline_mode=pl.Buffered(1)),
        ],
        out_specs=pl.BlockSpec((TM_FF, D_MODEL), lambda i: (i, 0)),
        compiler_params=pltpu.CompilerParams(
            dimension_semantics=("parallel",), vmem_limit_bytes=VMEM_LIMIT),
    )(x2, g, w1, w2)


def _regroup_w_in(w):
    o = 0
    conv = w[:, o:o + 2 * CONV_DIM]; o += 2 * CONV_DIM
    cq = w[:, o:o + Q_RANK]; o += Q_RANK
    ckv = w[:, o:o + KV_RANK]; o += KV_RANK
    kr = w[:, o:o + ROPE_DIM]; o += ROPE_DIM
    hg = w[:, o:o + 4 * HGRN_HEADS * HGRN_DK]; o += 4 * HGRN_HEADS * HGRN_DK
    gate = w[:, o:]
    kr_slot = jnp.pad(kr, ((0, 0), (NOPE_DIM, LANE - QK_DIM)))
    return jnp.concatenate([gate, conv, cq, ckv, kr_slot, hg], axis=1).astype(BF16)


def _head_slots(w, per_head, start, width):
    r = w.shape[0]
    wh = w.reshape(r, MLA_HEADS, per_head)[:, :, start:start + width]
    wh = jnp.pad(wh, ((0, 0), (0, 0), (0, HEAD_SLOT - width)))
    return wh.reshape(r, MLA_HEADS * HEAD_SLOT)


def _swap_rope_halves(a):
    half = ROPE_DIM // 2
    return jnp.concatenate([jnp.zeros_like(a[..., :NOPE_DIM]), a[..., NOPE_DIM + half:],
                            a[..., NOPE_DIM:NOPE_DIM + half]], axis=-1)


def _v_slots(w_ukv):
    r = w_ukv.shape[0]
    wv = w_ukv.reshape(r, MLA_HEADS // 2, 2, NOPE_DIM + V_DIM)[..., NOPE_DIM:]
    z = jnp.zeros_like(wv[:, :, 0])
    slots = jnp.stack([jnp.concatenate([wv[:, :, 0], z], axis=-1),
                       jnp.concatenate([z, wv[:, :, 1]], axis=-1)], axis=2)
    ones = jnp.ones((MLA_HEADS // 2, V_DIM), F32)
    zo = jnp.zeros_like(ones)
    vone = jnp.stack([jnp.concatenate([zo, ones], axis=-1),
                      jnp.concatenate([ones, zo], axis=-1)], axis=1)
    return slots.reshape(r, MLA_HEADS * HEAD_SLOT), vone.reshape(1, MLA_HEADS * HEAD_SLOT)


def _rope_tables(lp):
    half = ROPE_DIM // 2
    pos = jnp.arange(lp, dtype=F32)
    inv_freq = ROPE_BASE ** (-jnp.arange(half, dtype=F32) / half)
    ang = pos[:, None] * inv_freq[None, :]
    cos, sin = jnp.cos(ang), jnp.sin(ang)
    ones = jnp.ones((lp, NOPE_DIM), F32)
    z16 = jnp.zeros((lp, half), F32)
    z64 = jnp.zeros((lp, NOPE_DIM), F32)
    tail = jnp.zeros((lp, LANE - QK_DIM), F32)
    rc = jnp.concatenate([ones, cos, cos, tail], axis=1)
    rs1 = jnp.concatenate([z64, -sin, z16, tail], axis=1)
    rs2 = jnp.concatenate([z64, z16, sin, tail], axis=1)
    return rc, rs1, rs2


def kernel(x, meta, norm1_g, w_in, conv_w, conv_b, conv_ln_g, conv_ln_b, w_conv_out, q_a_norm_g, w_uq, kv_a_norm_g, w_ukv, q_norm_g, k_norm_g, w_attn_out, hgrn_lb_logits, hgrn_norm_g, w_hgrn_out, w_out, norm2_g, w_ff1, w_ff2):
    b, seq, d = x.shape
    depth = w_in.shape[0]
    l = seq + N_META
    lp = -(-l // (3 * SEQ_ALIGN)) * (3 * SEQ_ALIGN)
    xc = jnp.concatenate([jnp.broadcast_to(meta[None].astype(x.dtype), (b, N_META, d)), x,
                          jnp.zeros((b, lp - l, d), x.dtype)], axis=1)
    x2 = xc.reshape(b * lp, d)

    rc, rs1, rs2 = _rope_tables(lp)
    p_lb = jax.nn.softmax(hgrn_lb_logits.astype(F32), axis=0)
    lower_bounds = jnp.cumsum(p_lb, axis=0) - p_lb[0:1]
    row = lambda a: a.astype(F32).reshape(1, -1)
    pad_qk = lambda g: jnp.pad(g.astype(F32), (0, LANE - QK_DIM)).reshape(1, LANE)

    for li in range(depth):
        u2 = _inproj(x2, row(norm1_g[li]), _regroup_w_in(w_in[li]))
        u3 = u2.reshape(b, lp, N_IN_PAD)
        ha = _conv_branch(u3, jnp.repeat(conv_w[li].astype(F32), SUBLANE, axis=0),
                          row(conv_b[li]), row(conv_ln_g[li]),
                          row(conv_ln_b[li]))
        wuv, vone = _v_slots(w_ukv[li])
        w_uq_heads = w_uq[li].reshape(Q_RANK, MLA_HEADS, QK_DIM)
        w_uq_swapped = _swap_rope_halves(w_uq_heads).reshape(Q_RANK, MLA_HEADS * QK_DIM)
        q_gain = q_norm_g[li].astype(F32) * (QK_DIM ** -0.5 * LOG2E)
        q, k, v = _mla_prep(
            u3, row(q_a_norm_g[li]), row(kv_a_norm_g[li]),
            _head_slots(w_uq[li], QK_DIM, 0, QK_DIM).astype(BF16),
            _head_slots(w_uq_swapped, QK_DIM, 0, QK_DIM).astype(BF16),
            _head_slots(w_ukv[li], NOPE_DIM + V_DIM, 0, NOPE_DIM).astype(BF16),
            wuv.astype(BF16),
            pad_qk(q_gain), pad_qk(_swap_rope_halves(q_gain)), pad_qk(k_norm_g[li]),
            rc, rs1, rs2, vone)
        ob = _attention(q, k, v)
        oc = _hgrn(u3, 1.0 - row(lower_bounds[li]), row(hgrn_norm_g[li]))
        x2 = _merge(ha.reshape(b * lp, -1), ob.reshape(b * lp, -1), oc.reshape(b * lp, -1), u2, x2,
                    w_conv_out[li].astype(BF16), w_attn_out[li].astype(BF16),
                    w_hgrn_out[li].astype(BF16), w_out[li].astype(BF16))
        x2 = _ffn(x2, row(norm2_g[li]), w_ff1[li].astype(BF16), w_ff2[li].astype(BF16))

    return x2.reshape(b, lp, d)[:, N_META:l]
```

```python
import functools

import jax
import jax.numpy as jnp
import numpy as np
from jax import lax
from jax.experimental import pallas as pl
from jax.experimental.pallas import tpu as pltpu

F32 = jnp.float32
BF16 = jnp.bfloat16

D_MODEL = 1024
N_META = 16
EPS = 1e-6
GATE_CLAMP = 1.0 - 1e-6
CONV_DIM = 512
CONV_K = 31
MLA_HEADS = 8
Q_RANK = 256
KV_RANK = 128
NOPE_DIM = 64
ROPE_DIM = 32
V_DIM = 64
QK_DIM = NOPE_DIM + ROPE_DIM
ROPE_BASE = 10000.0
HGRN_HEADS = 4
HGRN_DK = 128
HGRN_DV = 128
D_FF = 4096

LANE = 128
HEAD_SLOT = LANE

COL_GATE = 0
COL_CONV = 3072
COL_CQ = 4096
COL_CKV = 4352
COL_KROPE = 4480
COL_HGRN = 4608
N_IN_PAD = 6656

VMEM_LIMIT = 52 * 1024 * 1024

TL = 384
TN_IN = 512
N_GATE_CHUNK = COL_CONV // TN_IN
SUBLANE = 8
RC_CONV = 32
CONV_HALO = 32
CONV_SPAN = TL + CONV_HALO - SUBLANE
RC_PREP = 128
TQ = 384
ATTN_HG = 4
LOG2E = 1.4426950408889634
HGRN_C = 128
HGRN_BLK = 4
TM_MERGE = 1024
TM_FF = 1024
TF_FF = 1024


def _rms(x, eps=EPS):
    return x * lax.rsqrt(jnp.mean(x * x, axis=-1, keepdims=True) + eps)


def _sigmoid(x):
    return 0.5 * jnp.tanh(0.5 * x) + 0.5


def _silu(x):
    return x * _sigmoid(x)


def _conv_stage(glu, j, hbuf, sbuf):
    @pl.when(j == 0)
    def _():
        hbuf[0:CONV_HALO, :] = jnp.zeros((CONV_HALO, CONV_DIM), F32)

    @pl.when(j > 0)
    def _():
        hbuf[0:CONV_HALO, :] = hbuf[TL:TL + CONV_HALO, :]

    hbuf[CONV_HALO:CONV_HALO + TL, :] = glu

    for r in range(1, SUBLANE):
        sbuf[r - 1] = hbuf[r:r + CONV_SPAN, :]


def _conv_rows(r0, w_ref, cb_ref, lg_ref, lb_ref, o_ref, hbuf, sbuf):
    base = CONV_HALO - (CONV_K - 1)
    acc = jnp.zeros((RC_CONV, CONV_DIM), F32) + cb_ref[...]
    for k in range(CONV_K):
        r = (base + k) % SUBLANE
        lo = pl.multiple_of(r0 + (base + k - r), SUBLANE)
        src = hbuf if r == 0 else sbuf.at[r - 1]
        wk = jnp.tile(w_ref[k * SUBLANE:(k + 1) * SUBLANE, :], (RC_CONV // SUBLANE, 1))
        acc = acc + wk * src[pl.ds(lo, RC_CONV), :]
    mu = jnp.mean(acc, axis=-1, keepdims=True)
    xc = acc - mu
    y = xc * lax.rsqrt(jnp.mean(xc * xc, axis=-1, keepdims=True) + EPS)
    y = y * lg_ref[...] + lb_ref[...]
    o_ref[0, pl.ds(r0, RC_CONV), :] = (y * jax.nn.sigmoid(y)).astype(o_ref.dtype)


def _rope(x, c, s1, s2):
    half = ROPE_DIM // 2
    return x * c + pltpu.roll(x, LANE - half, 1) * s1 + pltpu.roll(x, half, 1) * s2


def _lane_sumsq(x, ones):
    sq = x * x
    hi = sq.astype(BF16)
    lo = (sq - hi.astype(F32)).astype(BF16)
    return jnp.dot(jnp.concatenate([hi, lo], axis=1), ones, preferred_element_type=F32)


def _mla_rows(cq, ckv, kr, rows, qag_ref, kvag_ref, wuq_ref, wuqs_ref, wuk_ref, wuv_ref,
              qng_ref, qngs_ref, kng_ref, rc_ref, rs1_ref, rs2_ref, vone_ref, q_out, k_out, v_out):
    inv_d = 1.0 / QK_DIM
    ones = jnp.ones((2 * LANE, LANE), BF16)
    cq = (_rms(cq) * qag_ref[...]).astype(BF16)
    q = jnp.dot(cq, wuq_ref[...], preferred_element_type=F32)
    qs = jnp.dot(cq, wuqs_ref[...], preferred_element_type=F32)
    ckv = (_rms(ckv) * kvag_ref[...]).astype(BF16)
    kn = jnp.dot(ckv, wuk_ref[...], preferred_element_type=F32)
    v = jnp.dot(ckv, wuv_ref[...], preferred_element_type=F32) + vone_ref[...]
    rc, rs1, rs2 = rc_ref[rows, :], rs1_ref[rows, :], rs2_ref[rows, :]
    q_c = qng_ref[...] * rc
    q_s = qngs_ref[...] * (rs1 + rs2)
    kr_ss = _lane_sumsq(kr, ones)
    kr_rot = _rope(kr * kng_ref[...], rc, rs1, rs2)
    for h in range(MLA_HEADS):
        sl = slice(h * HEAD_SLOT, (h + 1) * HEAD_SLOT)
        qh = q[:, sl]
        q_out[0, h, rows, :] = ((qh * q_c + qs[:, sl] * q_s) * lax.rsqrt(
            _lane_sumsq(qh, ones) * inv_d + EPS)).astype(q_out.dtype)
        kh = kn[:, sl]
        k_out[0, h, rows, :] = ((kh * kng_ref[...] + kr_rot) * lax.rsqrt(
            (_lane_sumsq(kh, ones) + kr_ss) * inv_d + EPS)).astype(k_out.dtype)
        v_out[0, h, rows, :] = v[:, sl].astype(v_out.dtype)


def _hgrn_chunk(q, fr, iv, gv, rows, oml_ref, ng_ref, lvl, st_ref, o_ref):
    C = HGRN_C
    rowid = lax.broadcasted_iota(jnp.int32, (C, HGRN_DK), 0)
    nt = (((1,), (1,)), ((), ()))
    tn = (((0,), (0,)), ((), ()))
    for h in range(HGRN_HEADS):
        sl = slice(h * HGRN_DK, (h + 1) * HGRN_DK)
        qh = q[:, sl]
        kk = oml_ref[:, sl] * jax.nn.sigmoid(-fr[:, sl])
        lf = jnp.log1p(-jnp.minimum(kk, GATE_CLAMP)) * LOG2E
        v = _silu(iv[:, sl]).astype(BF16)

        fwd = lf
        bwd = jnp.zeros_like(lf)
        n, level = 1, 0
        a = None
        while n < C:
            if n >= HGRN_BLK:
                qs = (qh * jnp.exp2(fwd)).astype(BF16)
                if a is None:
                    a = jnp.where(lvl == 0, lax.dot_general(
                        qs, (kk * jnp.exp2(-fwd)).astype(BF16), nt,
                        preferred_element_type=F32), 0.0)
                level += 1
                ks = (kk * jnp.exp2(bwd)).astype(BF16)
                a = a + jnp.where(lvl == level,
                                  lax.dot_general(qs, ks, nt, preferred_element_type=F32), 0.0)
            tot = fwd + bwd
            upper = (rowid & n) != 0
            fwd = fwd + jnp.where(upper, pltpu.roll(tot, n, 0), 0.0)
            bwd = bwd + jnp.where(upper, 0.0, pltpu.roll(tot, C - n, 0))
            n *= 2

        st = st_ref[h]
        o = jnp.dot(a.astype(BF16), v, preferred_element_type=F32)
        o = o + lax.dot_general((qh * jnp.exp2(fwd)).astype(BF16), st.astype(BF16), nt,
                                preferred_element_type=F32)
        ks = (kk * jnp.exp2(bwd)).astype(BF16)
        st_ref[h] = st * jnp.exp2(fwd[C - 1:C, :]) + lax.dot_general(
            v, ks, tn, preferred_element_type=F32)

        o = _rms(o) * ng_ref[:, sl]
        o_ref[0, rows, sl] = (o * _silu(gv[:, sl])).astype(o_ref.dtype)


def _hgrn_levels():
    t = np.arange(HGRN_C)[:, None]
    s = np.arange(HGRN_C)[None, :]
    x = (t ^ s) // HGRN_BLK
    lvl = np.where(x == 0, 0, np.floor(np.log2(np.maximum(x, 1))).astype(np.int64) + 1)
    return jnp.asarray(np.where(s > t, -1, lvl), dtype=jnp.int32)


def _front_kernel(x_ref, g1_ref, wbr_ref, wgate_ref,
                  cw_ref, cb_ref, lg_ref, lb_ref,
                  qag_ref, kvag_ref, wuq_ref, wuqs_ref, wuk_ref, wuv_ref, qng_ref, qngs_ref, kng_ref,
                  rc_ref, rs1_ref, rs2_ref, vone_ref,
                  oml_ref, ng_ref, lvl_ref,
                  gate_out, ha_out, q_out, k_out, v_out, oc_out,
                  h_scr, ym_scr, yh_scr, hbuf, sbuf, st_ref):
    j = pl.program_id(1)
    hw = HGRN_HEADS * HGRN_DK
    n_mla = COL_HGRN - COL_CQ

    @pl.when(j == 0)
    def _():
        st_ref[...] = jnp.zeros(st_ref.shape, F32)

    h = (_rms(x_ref[0]) * g1_ref[...]).astype(BF16)
    h_scr[...] = h
    yc = jnp.dot(h, wbr_ref[:, 0:2 * CONV_DIM], preferred_element_type=F32)
    _conv_stage(yc[:, :CONV_DIM] * _sigmoid(yc[:, CONV_DIM:]), j, hbuf, sbuf)
    ym_scr[...] = jnp.dot(h, wbr_ref[:, 2 * CONV_DIM:2 * CONV_DIM + n_mla],
                          preferred_element_type=F32)
    yh_scr[...] = jnp.dot(h, wbr_ref[:, 2 * CONV_DIM + n_mla:], preferred_element_type=F32)

    conv_per_gate = (TL // RC_CONV) // N_GATE_CHUNK

    def gate_and_conv(c, carry):
        gate_out[c, 0] = _sigmoid(jnp.dot(h_scr[...], wgate_ref[c],
                                          preferred_element_type=F32)).astype(gate_out.dtype)
        for t in range(conv_per_gate):
            r0 = pl.multiple_of((c * conv_per_gate + t) * RC_CONV, RC_CONV)
            _conv_rows(r0, cw_ref, cb_ref, lg_ref, lb_ref, ha_out, hbuf, sbuf)
        return carry

    for c in range(N_GATE_CHUNK):
        gate_and_conv(c, 0)

    def mla_chunk(c, carry):
        rows = pl.ds(pl.multiple_of(c * RC_PREP, RC_PREP), RC_PREP)
        _mla_rows(ym_scr[rows, 0:Q_RANK], ym_scr[rows, Q_RANK:Q_RANK + KV_RANK],
                  ym_scr[rows, Q_RANK + KV_RANK:], rows,
                  qag_ref, kvag_ref, wuq_ref, wuqs_ref, wuk_ref, wuv_ref,
                  qng_ref, qngs_ref, kng_ref, rc_ref, rs1_ref, rs2_ref, vone_ref, q_out, k_out, v_out)
        return carry

    for c in range(TL // RC_PREP):
        mla_chunk(c, 0)

    lvl = lvl_ref[...]

    def hgrn_chunk(c, carry):
        rows = pl.ds(pl.multiple_of(c * HGRN_C, HGRN_C), HGRN_C)
        _hgrn_chunk(yh_scr[rows, 0:hw], yh_scr[rows, hw:2 * hw], yh_scr[rows, 2 * hw:3 * hw],
                    yh_scr[rows, 3 * hw:], rows, oml_ref, ng_ref, lvl, st_ref, oc_out)
        return carry

    lax.fori_loop(0, TL // HGRN_C, hgrn_chunk, 0)


def _front(x3, g1, w_br, w_gate, conv_w, conv_b, ln_g, ln_b, qag, kvag, wuq, wuqs, wuk, wuv,
           qng, qngs, kng, rc, rs1, rs2, vone, one_minus_lb, hgrn_ng):
    b, lp, d = x3.shape
    hw = MLA_HEADS * HEAD_SLOT
    gw = HGRN_HEADS * HGRN_DK
    n_br = N_IN_PAD - COL_CONV
    const = lambda shape: pl.BlockSpec(shape, lambda bb, j: (0,) * len(shape))
    resident = lambda shape: pl.BlockSpec(shape, lambda bb, j: (0,) * len(shape),
                                          pipeline_mode=pl.Buffered(1))
    tab = pl.BlockSpec((TL, LANE), lambda bb, j: (j, 0))
    seq = lambda w: pl.BlockSpec((1, TL, w), lambda bb, j: (bb, j, 0))
    heads = pl.BlockSpec((1, MLA_HEADS, TL, HEAD_SLOT), lambda bb, j: (bb, 0, j, 0))
    qkv = jax.ShapeDtypeStruct((b, MLA_HEADS, lp, HEAD_SLOT), BF16)
    return pl.pallas_call(
        _front_kernel,
        out_shape=(jax.ShapeDtypeStruct((N_GATE_CHUNK, b, lp, TN_IN), BF16),
                   jax.ShapeDtypeStruct((b, lp, CONV_DIM), BF16),
                   qkv, qkv, qkv,
                   jax.ShapeDtypeStruct((b, lp, gw), BF16)),
        grid=(b, lp // TL),
        in_specs=[
            seq(d), const((1, d)), resident((d, n_br)), resident((N_GATE_CHUNK, d, TN_IN)),
            const((CONV_K * SUBLANE, CONV_DIM)), const((1, CONV_DIM)), const((1, CONV_DIM)),
            const((1, CONV_DIM)),
            const((1, Q_RANK)), const((1, KV_RANK)),
            const((Q_RANK, hw)), const((Q_RANK, hw)), const((KV_RANK, hw)), const((KV_RANK, hw)),
            const((1, LANE)), const((1, LANE)), const((1, LANE)),
            tab, tab, tab, const((1, hw)),
            const((1, gw)), const((1, gw)), const((HGRN_C, HGRN_C)),
        ],
        out_specs=(pl.BlockSpec((N_GATE_CHUNK, 1, TL, TN_IN), lambda bb, j: (0, bb, j, 0)),
                   seq(CONV_DIM), heads, heads, heads, seq(gw)),
        scratch_shapes=[pltpu.VMEM((TL, d), BF16),
                        pltpu.VMEM((TL, COL_HGRN - COL_CQ), F32),
                        pltpu.VMEM((TL, 4 * gw), F32),
                        pltpu.VMEM((CONV_HALO + TL, CONV_DIM), F32),
                        pltpu.VMEM((SUBLANE - 1, CONV_SPAN, CONV_DIM), F32),
                        pltpu.VMEM((HGRN_HEADS, HGRN_DV, HGRN_DK), F32)],
        compiler_params=pltpu.CompilerParams(
            dimension_semantics=("parallel", "arbitrary"), vmem_limit_bytes=VMEM_LIMIT),
    )(x3, g1, w_br, w_gate, conv_w, conv_b, ln_g, ln_b, qag, kvag, wuq, wuqs, wuk, wuv,
      qng, qngs, kng, rc, rs1, rs2, vone, one_minus_lb, hgrn_ng, _hgrn_levels())


def _attn_kernel(q_ref, k_ref, v_ref, o_ref, s_ref, m_ref, acc_ref):
    i = pl.program_id(2)
    nt = (((1,), (1,)), ((), ()))
    heads = range(ATTN_HG)
    qs = [q_ref[0, h] for h in heads]
    pair = 2 * TQ
    odd = (i & 1) == 1

    def scores(h, start, width):
        kt = k_ref[0, h, pl.ds(pl.multiple_of(start, TQ), width), :]
        return lax.dot_general(qs[h], kt, nt, preferred_element_type=F32)

    def fold(s):
        m = s[:, 0:LANE]
        for c in range(1, s.shape[1] // LANE):
            m = jnp.maximum(m, s[:, c * LANE:(c + 1) * LANE])
        return m

    def pv(h, s, start, width):
        p = jnp.exp2(s - ms[h]).astype(BF16)
        vt = v_ref[0, h, pl.ds(pl.multiple_of(start, TQ), width), :]
        return jnp.dot(p, vt, preferred_element_type=F32)

    def pass1(p, mloc):
        out = []
        for h in heads:
            s = scores(h, p * pair, pair)
            s_ref[h, 2 * p] = s[:, :TQ]
            s_ref[h, 2 * p + 1] = s[:, TQ:]
            out.append(jnp.maximum(mloc[h], fold(s)))
        return tuple(out)

    mloc = lax.fori_loop(0, lax.shift_right_logical(i, 1), pass1,
                         tuple(jnp.full((TQ, LANE), -1e30, F32) for _ in heads))
    keep = (lax.broadcasted_iota(jnp.int32, (TQ, TQ), 0)
            >= lax.broadcasted_iota(jnp.int32, (TQ, TQ), 1))

    @pl.when(odd)
    def _():
        for h in heads:
            s = scores(h, (i - 1) * TQ, pair)
            lo, hi = s[:, :TQ], jnp.where(keep, s[:, TQ:], -1e30)
            s_ref[h, i - 1] = lo
            s_ref[h, i] = hi
            m_ref[h] = jnp.maximum(mloc[h], jnp.maximum(fold(lo), fold(hi)))

    @pl.when(jnp.logical_not(odd))
    def _():
        for h in heads:
            s = jnp.where(keep, scores(h, i * TQ, TQ), -1e30)
            s_ref[h, i] = s
            m_ref[h] = jnp.maximum(mloc[h], fold(s))

    ms = [jnp.max(m_ref[h], axis=-1, keepdims=True) for h in heads]

    def pass2(p, acc):
        return tuple(
            acc[h] + pv(h, jnp.concatenate([s_ref[h, 2 * p], s_ref[h, 2 * p + 1]], axis=1),
                        p * pair, pair)
            for h in heads)

    acc = lax.fori_loop(0, lax.shift_right_logical(i + 1, 1), pass2,
                        tuple(jnp.zeros((TQ, HEAD_SLOT), F32) for _ in heads))
    for h in heads:
        acc_ref[h] = acc[h]

    @pl.when(jnp.logical_not(odd))
    def _():
        for h in heads:
            acc_ref[h] += pv(h, s_ref[h, i], i * TQ, TQ)

    first = lax.broadcasted_iota(jnp.int32, (TQ, HEAD_SLOT), 1) < V_DIM
    for hp in range(ATTN_HG // 2):
        a, b = acc_ref[2 * hp], acc_ref[2 * hp + 1]
        den = pltpu.roll(jnp.where(first, b, a), V_DIM, 1)
        o_ref[0, :, hp * HEAD_SLOT:(hp + 1) * HEAD_SLOT] = (
            jnp.where(first, a, b) / den).astype(o_ref.dtype)


def _attention(q, k, v):
    b, nh, lp, hs = q.shape
    nq = lp // TQ
    return pl.pallas_call(
        _attn_kernel,
        out_shape=jax.ShapeDtypeStruct((b, lp, nh * V_DIM), BF16),
        grid=(b, nh // ATTN_HG, nq),
        in_specs=[
            pl.BlockSpec((1, ATTN_HG, TQ, hs), lambda bb, g, i: (bb, g, i, 0)),
            pl.BlockSpec((1, ATTN_HG, lp, hs), lambda bb, g, i: (bb, g, 0, 0)),
            pl.BlockSpec((1, ATTN_HG, lp, hs), lambda bb, g, i: (bb, g, 0, 0)),
        ],
        out_specs=pl.BlockSpec((1, TQ, ATTN_HG * V_DIM), lambda bb, g, i: (bb, i, g)),
        scratch_shapes=[pltpu.VMEM((ATTN_HG, nq, TQ, TQ), F32),
                        pltpu.VMEM((ATTN_HG, TQ, LANE), F32),
                        pltpu.VMEM((ATTN_HG, TQ, HEAD_SLOT), F32)],
        compiler_params=pltpu.CompilerParams(
            dimension_semantics=("parallel", "parallel", "arbitrary"), vmem_limit_bytes=VMEM_LIMIT),
    )(q, k, v)


def _merge_kernel(ha_ref, ob_ref, oc_ref, gate_ref, x_ref, wa_ref, wb_ref, wc_ref, wo_ref, o_ref):
    mix = None
    for n, (br_ref, w_ref) in enumerate(((ha_ref, wa_ref), (ob_ref, wb_ref), (oc_ref, wc_ref))):
        y = jnp.dot(br_ref[...], w_ref[...], preferred_element_type=F32)
        per = D_MODEL // TN_IN
        gate = jnp.concatenate([gate_ref[n * per + t] for t in range(per)], axis=1)
        y = y * gate.astype(F32)
        mix = y if mix is None else mix + y
    o_ref[...] = x_ref[...] + jnp.dot(mix.astype(BF16), wo_ref[...], preferred_element_type=F32)


def _merge(ha, ob, oc, gates, x2, wa, wb, wc, wo):
    t = x2.shape[0]
    br = lambda: pl.BlockSpec((TM_MERGE, 512), lambda i: (i, 0))
    wbr = lambda: pl.BlockSpec((512, D_MODEL), lambda i: (0, 0))
    return pl.pallas_call(
        _merge_kernel,
        out_shape=jax.ShapeDtypeStruct((t, D_MODEL), F32),
        grid=(t // TM_MERGE,),
        in_specs=[br(), br(), br(),
                  pl.BlockSpec((N_GATE_CHUNK, TM_MERGE, TN_IN), lambda i: (0, i, 0)),
                  pl.BlockSpec((TM_MERGE, D_MODEL), lambda i: (i, 0)),
                  wbr(), wbr(), wbr(),
                  pl.BlockSpec((D_MODEL, D_MODEL), lambda i: (0, 0))],
        out_specs=pl.BlockSpec((TM_MERGE, D_MODEL), lambda i: (i, 0)),
        compiler_params=pltpu.CompilerParams(
            dimension_semantics=("parallel",), vmem_limit_bytes=VMEM_LIMIT),
    )(ha, ob, oc, gates, x2, wa, wb, wc, wo)


def _ffn_kernel(x_ref, g_ref, w1_ref, w2_ref, o_ref):
    x = x_ref[...]
    h = (_rms(x) * g_ref[...]).astype(BF16)
    acc = x
    for c in range(D_FF // TF_FF):
        cols = slice(c * TF_FF, (c + 1) * TF_FF)
        a = jnp.maximum(jnp.dot(h, w1_ref[:, cols], preferred_element_type=F32), 0.0)
        acc = acc + jnp.dot((a * a).astype(BF16), w2_ref[cols, :], preferred_element_type=F32)
    o_ref[...] = acc


def _ffn(x2, g, w1, w2):
    t = x2.shape[0]
    return pl.pallas_call(
        _ffn_kernel,
        out_shape=jax.ShapeDtypeStruct((t, D_MODEL), F32),
        grid=(t // TM_FF,),
        in_specs=[
            pl.BlockSpec((TM_FF, D_MODEL), lambda i: (i, 0)),
            pl.BlockSpec((1, D_MODEL), lambda i: (0, 0)),
            pl.BlockSpec((D_MODEL, D_FF), lambda i: (0, 0), pipeline_mode=pl.Buffered(1)),
            pl.BlockSpec((D_FF, D_MODEL), lambda i: (0, 0), pipeline_mode=pl.Buffered(1)),
        ],
        out_specs=pl.BlockSpec((TM_FF, D_MODEL), lambda i: (i, 0)),
        compiler_params=pltpu.CompilerParams(
            dimension_semantics=("parallel",), vmem_limit_bytes=VMEM_LIMIT),
    )(x2, g, w1, w2)


def _regroup_w_in(w):
    o = 0
    conv = w[:, o:o + 2 * CONV_DIM]; o += 2 * CONV_DIM
    cq = w[:, o:o + Q_RANK]; o += Q_RANK
    ckv = w[:, o:o + KV_RANK]; o += KV_RANK
    kr = w[:, o:o + ROPE_DIM]; o += ROPE_DIM
    hg = w[:, o:o + 4 * HGRN_HEADS * HGRN_DK]; o += 4 * HGRN_HEADS * HGRN_DK
    gate = w[:, o:]
    kr_slot = jnp.pad(kr, ((0, 0), (NOPE_DIM, LANE - QK_DIM)))
    return jnp.concatenate([gate, conv, cq, ckv, kr_slot, hg], axis=1).astype(BF16)


def _head_slots(w, per_head, start, width):
    r = w.shape[0]
    wh = w.reshape(r, MLA_HEADS, per_head)[:, :, start:start + width]
    wh = jnp.pad(wh, ((0, 0), (0, 0), (0, HEAD_SLOT - width)))
    return wh.reshape(r, MLA_HEADS * HEAD_SLOT)


def _swap_rope_halves(a):
    half = ROPE_DIM // 2
    return jnp.concatenate([jnp.zeros_like(a[..., :NOPE_DIM]), a[..., NOPE_DIM + half:],
                            a[..., NOPE_DIM:NOPE_DIM + half]], axis=-1)


def _v_slots(w_ukv):
    r = w_ukv.shape[0]
    wv = w_ukv.reshape(r, MLA_HEADS // 2, 2, NOPE_DIM + V_DIM)[..., NOPE_DIM:]
    z = jnp.zeros_like(wv[:, :, 0])
    slots = jnp.stack([jnp.concatenate([wv[:, :, 0], z], axis=-1),
                       jnp.concatenate([z, wv[:, :, 1]], axis=-1)], axis=2)
    ones = jnp.ones((MLA_HEADS // 2, V_DIM), F32)
    zo = jnp.zeros_like(ones)
    vone = jnp.stack([jnp.concatenate([zo, ones], axis=-1),
                      jnp.concatenate([ones, zo], axis=-1)], axis=1)
    return slots.reshape(r, MLA_HEADS * HEAD_SLOT), vone.reshape(1, MLA_HEADS * HEAD_SLOT)


def _rope_tables(lp):
    half = ROPE_DIM // 2
    pos = jnp.arange(lp, dtype=F32)
    inv_freq = ROPE_BASE ** (-jnp.arange(half, dtype=F32) / half)
    ang = pos[:, None] * inv_freq[None, :]
    cos, sin = jnp.cos(ang), jnp.sin(ang)
    ones = jnp.ones((lp, NOPE_DIM), F32)
    z16 = jnp.zeros((lp, half), F32)
    z64 = jnp.zeros((lp, NOPE_DIM), F32)
    tail = jnp.zeros((lp, LANE - QK_DIM), F32)
    rc = jnp.concatenate([ones, cos, cos, tail], axis=1)
    rs1 = jnp.concatenate([z64, -sin, z16, tail], axis=1)
    rs2 = jnp.concatenate([z64, z16, sin, tail], axis=1)
    return rc, rs1, rs2


def kernel(x, meta, norm1_g, w_in, conv_w, conv_b, conv_ln_g, conv_ln_b, w_conv_out, q_a_norm_g, w_uq, kv_a_norm_g, w_ukv, q_norm_g, k_norm_g, w_attn_out, hgrn_lb_logits, hgrn_norm_g, w_hgrn_out, w_out, norm2_g, w_ff1, w_ff2):
    b, seq, d = x.shape
    depth = w_in.shape[0]
    l = seq + N_META
    lp = -(-l // TL) * TL
    xc = jnp.concatenate([jnp.broadcast_to(meta[None].astype(x.dtype), (b, N_META, d)), x,
                          jnp.zeros((b, lp - l, d), x.dtype)], axis=1)
    x2 = xc.reshape(b * lp, d)

    rc, rs1, rs2 = _rope_tables(lp)
    p_lb = jax.nn.softmax(hgrn_lb_logits.astype(F32), axis=0)
    lower_bounds = jnp.cumsum(p_lb, axis=0) - p_lb[0:1]
    row = lambda a: a.astype(F32).reshape(1, -1)
    pad_qk = lambda g: jnp.pad(g.astype(F32), (0, LANE - QK_DIM)).reshape(1, LANE)

    for li in range(depth):
        wuv, vone = _v_slots(w_ukv[li])
        w_uq_heads = w_uq[li].reshape(Q_RANK, MLA_HEADS, QK_DIM)
        w_uq_swapped = _swap_rope_halves(w_uq_heads).reshape(Q_RANK, MLA_HEADS * QK_DIM)
        q_gain = q_norm_g[li].astype(F32) * (QK_DIM ** -0.5 * LOG2E)
        w_all = _regroup_w_in(w_in[li])
        w_gate = w_all[:, :COL_CONV].reshape(d, N_GATE_CHUNK, TN_IN).transpose(1, 0, 2)
        gates, ha, q, k, v, oc = _front(
            x2.reshape(b, lp, d), row(norm1_g[li]), w_all[:, COL_CONV:], w_gate,
            jnp.repeat(conv_w[li].astype(F32), SUBLANE, axis=0),
            row(conv_b[li]), row(conv_ln_g[li]), row(conv_ln_b[li]),
            row(q_a_norm_g[li]), row(kv_a_norm_g[li]),
            _head_slots(w_uq[li], QK_DIM, 0, QK_DIM).astype(BF16),
            _head_slots(w_uq_swapped, QK_DIM, 0, QK_DIM).astype(BF16),
            _head_slots(w_ukv[li], NOPE_DIM + V_DIM, 0, NOPE_DIM).astype(BF16),
            wuv.astype(BF16),
            pad_qk(q_gain), pad_qk(_swap_rope_halves(q_gain)), pad_qk(k_norm_g[li]),
            rc, rs1, rs2, vone, 1.0 - row(lower_bounds[li]), row(hgrn_norm_g[li]))
        ob = _attention(q, k, v)
        flat = lambda a: a.reshape(b * lp, -1)
        x2 = _merge(flat(ha), flat(ob), flat(oc), gates.reshape(N_GATE_CHUNK, b * lp, TN_IN), x2,
                    w_conv_out[li].astype(BF16), w_attn_out[li].astype(BF16),
                    w_hgrn_out[li].astype(BF16), w_out[li].astype(BF16))
        x2 = _ffn(x2, row(norm2_g[li]), w_ff1[li].astype(BF16), w_ff2[li].astype(BF16))

    return x2.reshape(b, lp, d)[:, N_META:l]
```

```python
import functools

import jax
import jax.numpy as jnp
import numpy as np
from jax import lax
from jax.experimental import pallas as pl
from jax.experimental.pallas import tpu as pltpu

F32 = jnp.float32
BF16 = jnp.bfloat16

D_MODEL = 1024
N_META = 16
EPS = 1e-6
GATE_CLAMP = 1.0 - 1e-6
CONV_DIM = 512
CONV_K = 31
MLA_HEADS = 8
Q_RANK = 256
KV_RANK = 128
NOPE_DIM = 64
ROPE_DIM = 32
V_DIM = 64
QK_DIM = NOPE_DIM + ROPE_DIM
ROPE_BASE = 10000.0
HGRN_HEADS = 4
HGRN_DK = 128
HGRN_DV = 128
D_FF = 4096

LANE = 128
HEAD_SLOT = LANE

COL_GATE = 0
COL_CONV = 3072
COL_CQ = 4096
COL_CKV = 4352
COL_KROPE = 4480
COL_HGRN = 4608
N_IN_PAD = 6656

VMEM_LIMIT = 52 * 1024 * 1024

TL = 384
TN_IN = 512
N_GATE_CHUNK = COL_CONV // TN_IN
SUBLANE = 8
RC_CONV = 32
CONV_HALO = 32
CONV_SPAN = TL + CONV_HALO - SUBLANE
RC_PREP = 128
TQ = 384
ATTN_HG = 4
LOG2E = 1.4426950408889634
HGRN_C = 128
HGRN_BLK = 4
TM_MERGE = 1024
TM_FF = 1024
TF_FF = 1024


def _rms(x, eps=EPS):
    return x * lax.rsqrt(jnp.mean(x * x, axis=-1, keepdims=True) + eps)


def _sigmoid(x):
    return 0.5 * jnp.tanh(0.5 * x) + 0.5


def _silu(x):
    return x * _sigmoid(x)


def _conv_stage(glu, j, hbuf, sbuf):
    @pl.when(j == 0)
    def _():
        hbuf[0:CONV_HALO, :] = jnp.zeros((CONV_HALO, CONV_DIM), F32)

    @pl.when(j > 0)
    def _():
        hbuf[0:CONV_HALO, :] = hbuf[TL:TL + CONV_HALO, :]

    hbuf[CONV_HALO:CONV_HALO + TL, :] = glu

    for r in range(1, SUBLANE):
        sbuf[r - 1] = hbuf[r:r + CONV_SPAN, :]


def _conv_rows(r0, w_ref, cb_ref, lg_ref, lb_ref, o_ref, hbuf, sbuf):
    base = CONV_HALO - (CONV_K - 1)
    acc = jnp.zeros((RC_CONV, CONV_DIM), F32) + cb_ref[...]
    for k in range(CONV_K):
        r = (base + k) % SUBLANE
        lo = pl.multiple_of(r0 + (base + k - r), SUBLANE)
        src = hbuf if r == 0 else sbuf.at[r - 1]
        wk = jnp.tile(w_ref[k * SUBLANE:(k + 1) * SUBLANE, :], (RC_CONV // SUBLANE, 1))
        acc = acc + wk * src[pl.ds(lo, RC_CONV), :]
    mu = jnp.mean(acc, axis=-1, keepdims=True)
    xc = acc - mu
    y = xc * lax.rsqrt(jnp.mean(xc * xc, axis=-1, keepdims=True) + EPS)
    y = y * lg_ref[...] + lb_ref[...]
    o_ref[0, pl.ds(r0, RC_CONV), :] = (y * jax.nn.sigmoid(y)).astype(o_ref.dtype)


def _rope(x, c, s1, s2):
    half = ROPE_DIM // 2
    return x * c + pltpu.roll(x, LANE - half, 1) * s1 + pltpu.roll(x, half, 1) * s2


def _lane_sumsq(x, ones):
    sq = x * x
    hi = sq.astype(BF16)
    lo = (sq - hi.astype(F32)).astype(BF16)
    return jnp.dot(jnp.concatenate([hi, lo], axis=1), ones, preferred_element_type=F32)


def _mla_rows(cq, ckv, kr, rows, qag_ref, kvag_ref, wuq_ref, wuqs_ref, wuk_ref, wuv_ref,
              qng_ref, qngs_ref, kng_ref, rc_ref, rs1_ref, rs2_ref, vone_ref, q_out, k_out, v_out):
    inv_d = 1.0 / QK_DIM
    ones = jnp.ones((2 * LANE, LANE), BF16)
    cq = (_rms(cq) * qag_ref[...]).astype(BF16)
    q = jnp.dot(cq, wuq_ref[...], preferred_element_type=F32)
    qs = jnp.dot(cq, wuqs_ref[...], preferred_element_type=F32)
    ckv = (_rms(ckv) * kvag_ref[...]).astype(BF16)
    kn = jnp.dot(ckv, wuk_ref[...], preferred_element_type=F32)
    v = jnp.dot(ckv, wuv_ref[...], preferred_element_type=F32) + vone_ref[...]
    rc, rs1, rs2 = rc_ref[rows, :], rs1_ref[rows, :], rs2_ref[rows, :]
    q_c = qng_ref[...] * rc
    q_s = qngs_ref[...] * (rs1 + rs2)
    kr_ss = _lane_sumsq(kr, ones)
    kr_rot = _rope(kr * kng_ref[...], rc, rs1, rs2)
    for h in range(MLA_HEADS):
        sl = slice(h * HEAD_SLOT, (h + 1) * HEAD_SLOT)
        qh = q[:, sl]
        q_out[0, h, rows, :] = ((qh * q_c + qs[:, sl] * q_s) * lax.rsqrt(
            _lane_sumsq(qh, ones) * inv_d + EPS)).astype(q_out.dtype)
        kh = kn[:, sl]
        k_out[0, h, rows, :] = ((kh * kng_ref[...] + kr_rot) * lax.rsqrt(
            (_lane_sumsq(kh, ones) + kr_ss) * inv_d + EPS)).astype(k_out.dtype)
        v_out[0, h, rows, :] = v[:, sl].astype(v_out.dtype)


def _hgrn_chunk(q, fr, iv, gv, rows, oml_ref, ng_ref, lvl, st_ref, o_ref):
    C = HGRN_C
    rowid = lax.broadcasted_iota(jnp.int32, (C, HGRN_DK), 0)
    nt = (((1,), (1,)), ((), ()))
    tn = (((0,), (0,)), ((), ()))
    for h in range(HGRN_HEADS):
        sl = slice(h * HGRN_DK, (h + 1) * HGRN_DK)
        qh = q[:, sl]
        kk = oml_ref[:, sl] * jax.nn.sigmoid(-fr[:, sl])
        lf = jnp.log1p(-jnp.minimum(kk, GATE_CLAMP)) * LOG2E
        v = _silu(iv[:, sl]).astype(BF16)

        fwd = lf
        bwd = jnp.zeros_like(lf)
        n, level = 1, 0
        a = None
        while n < C:
            if n >= HGRN_BLK:
                qs = (qh * jnp.exp2(fwd)).astype(BF16)
                if a is None:
                    a = jnp.where(lvl == 0, lax.dot_general(
                        qs, (kk * jnp.exp2(-fwd)).astype(BF16), nt,
                        preferred_element_type=F32), 0.0)
                level += 1
                ks = (kk * jnp.exp2(bwd)).astype(BF16)
                a = jnp.where(lvl == level,
                              lax.dot_general(qs, ks, nt, preferred_element_type=F32), a)
            tot = fwd + bwd
            upper = (rowid & n) != 0
            fwd = fwd + jnp.where(upper, pltpu.roll(tot, n, 0), 0.0)
            bwd = bwd + jnp.where(upper, 0.0, pltpu.roll(tot, C - n, 0))
            n *= 2

        st = st_ref[h]
        o = jnp.dot(a.astype(BF16), v, preferred_element_type=F32)
        o = o + lax.dot_general((qh * jnp.exp2(fwd)).astype(BF16), st.astype(BF16), nt,
                                preferred_element_type=F32)
        ks = (kk * jnp.exp2(bwd)).astype(BF16)
        st_ref[h] = st * jnp.exp2(fwd[C - 1:C, :]) + lax.dot_general(
            v, ks, tn, preferred_element_type=F32)

        o = _rms(o) * ng_ref[:, sl]
        o_ref[0, rows, sl] = (o * _silu(gv[:, sl])).astype(o_ref.dtype)


def _hgrn_levels():
    t = np.arange(HGRN_C)[:, None]
    s = np.arange(HGRN_C)[None, :]
    x = (t ^ s) // HGRN_BLK
    lvl = np.where(x == 0, 0, np.floor(np.log2(np.maximum(x, 1))).astype(np.int64) + 1)
    return jnp.asarray(np.where(s > t, -1, lvl), dtype=jnp.int32)


def _front_kernel(x_ref, g1_ref, wbr_ref, wgate_ref,
                  cw_ref, cb_ref, lg_ref, lb_ref,
                  qag_ref, kvag_ref, wuq_ref, wuqs_ref, wuk_ref, wuv_ref, qng_ref, qngs_ref, kng_ref,
                  rc_ref, rs1_ref, rs2_ref, vone_ref,
                  oml_ref, ng_ref, lvl_ref,
                  gate_out, ha_out, q_out, k_out, v_out, oc_out,
                  h_scr, ym_scr, yh_scr, hbuf, sbuf, st_ref):
    j = pl.program_id(1)
    hw = HGRN_HEADS * HGRN_DK
    n_mla = COL_HGRN - COL_CQ

    @pl.when(j == 0)
    def _():
        st_ref[...] = jnp.zeros(st_ref.shape, F32)

    h = (_rms(x_ref[0]) * g1_ref[...]).astype(BF16)
    h_scr[...] = h
    yc = jnp.dot(h, wbr_ref[:, 0:2 * CONV_DIM], preferred_element_type=F32)
    _conv_stage(yc[:, :CONV_DIM] * _sigmoid(yc[:, CONV_DIM:]), j, hbuf, sbuf)
    ym_scr[...] = jnp.dot(h, wbr_ref[:, 2 * CONV_DIM:2 * CONV_DIM + n_mla],
                          preferred_element_type=F32)
    yh_scr[...] = jnp.dot(h, wbr_ref[:, 2 * CONV_DIM + n_mla:], preferred_element_type=F32)

    conv_per_gate = (TL // RC_CONV) // N_GATE_CHUNK

    def gate_and_conv(c, carry):
        gate_out[c, 0] = _sigmoid(jnp.dot(h_scr[...], wgate_ref[c],
                                          preferred_element_type=F32)).astype(gate_out.dtype)
        for t in range(conv_per_gate):
            r0 = pl.multiple_of((c * conv_per_gate + t) * RC_CONV, RC_CONV)
            _conv_rows(r0, cw_ref, cb_ref, lg_ref, lb_ref, ha_out, hbuf, sbuf)
        return carry

    for c in range(N_GATE_CHUNK):
        gate_and_conv(c, 0)

    def mla_chunk(c, carry):
        rows = pl.ds(pl.multiple_of(c * RC_PREP, RC_PREP), RC_PREP)
        _mla_rows(ym_scr[rows, 0:Q_RANK], ym_scr[rows, Q_RANK:Q_RANK + KV_RANK],
                  ym_scr[rows, Q_RANK + KV_RANK:], rows,
                  qag_ref, kvag_ref, wuq_ref, wuqs_ref, wuk_ref, wuv_ref,
                  qng_ref, qngs_ref, kng_ref, rc_ref, rs1_ref, rs2_ref, vone_ref, q_out, k_out, v_out)
        return carry

    for c in range(TL // RC_PREP):
        mla_chunk(c, 0)

    lvl = lvl_ref[...]

    def hgrn_chunk(c, carry):
        rows = pl.ds(pl.multiple_of(c * HGRN_C, HGRN_C), HGRN_C)
        _hgrn_chunk(yh_scr[rows, 0:hw], yh_scr[rows, hw:2 * hw], yh_scr[rows, 2 * hw:3 * hw],
                    yh_scr[rows, 3 * hw:], rows, oml_ref, ng_ref, lvl, st_ref, oc_out)
        return carry

    lax.fori_loop(0, TL // HGRN_C, hgrn_chunk, 0)


def _front(x3, g1, w_br, w_gate, conv_w, conv_b, ln_g, ln_b, qag, kvag, wuq, wuqs, wuk, wuv,
           qng, qngs, kng, rc, rs1, rs2, vone, one_minus_lb, hgrn_ng):
    b, lp, d = x3.shape
    hw = MLA_HEADS * HEAD_SLOT
    gw = HGRN_HEADS * HGRN_DK
    n_br = N_IN_PAD - COL_CONV
    const = lambda shape: pl.BlockSpec(shape, lambda bb, j: (0,) * len(shape))
    resident = lambda shape: pl.BlockSpec(shape, lambda bb, j: (0,) * len(shape),
                                          pipeline_mode=pl.Buffered(1))
    tab = pl.BlockSpec((TL, LANE), lambda bb, j: (j, 0))
    seq = lambda w: pl.BlockSpec((1, TL, w), lambda bb, j: (bb, j, 0))
    heads = pl.BlockSpec((1, MLA_HEADS, TL, HEAD_SLOT), lambda bb, j: (bb, 0, j, 0))
    qkv = jax.ShapeDtypeStruct((b, MLA_HEADS, lp, HEAD_SLOT), BF16)
    return pl.pallas_call(
        _front_kernel,
        out_shape=(jax.ShapeDtypeStruct((N_GATE_CHUNK, b, lp, TN_IN), BF16),
                   jax.ShapeDtypeStruct((b, lp, CONV_DIM), BF16),
                   qkv, qkv, qkv,
                   jax.ShapeDtypeStruct((b, lp, gw), BF16)),
        grid=(b, lp // TL),
        in_specs=[
            seq(d), const((1, d)), resident((d, n_br)), resident((N_GATE_CHUNK, d, TN_IN)),
            const((CONV_K * SUBLANE, CONV_DIM)), const((1, CONV_DIM)), const((1, CONV_DIM)),
            const((1, CONV_DIM)),
            const((1, Q_RANK)), const((1, KV_RANK)),
            const((Q_RANK, hw)), const((Q_RANK, hw)), const((KV_RANK, hw)), const((KV_RANK, hw)),
            const((1, LANE)), const((1, LANE)), const((1, LANE)),
            tab, tab, tab, const((1, hw)),
            const((1, gw)), const((1, gw)), const((HGRN_C, HGRN_C)),
        ],
        out_specs=(pl.BlockSpec((N_GATE_CHUNK, 1, TL, TN_IN), lambda bb, j: (0, bb, j, 0)),
                   seq(CONV_DIM), heads, heads, heads, seq(gw)),
        scratch_shapes=[pltpu.VMEM((TL, d), BF16),
                        pltpu.VMEM((TL, COL_HGRN - COL_CQ), F32),
                        pltpu.VMEM((TL, 4 * gw), F32),
                        pltpu.VMEM((CONV_HALO + TL, CONV_DIM), F32),
                        pltpu.VMEM((SUBLANE - 1, CONV_SPAN, CONV_DIM), F32),
                        pltpu.VMEM((HGRN_HEADS, HGRN_DV, HGRN_DK), F32)],
        compiler_params=pltpu.CompilerParams(
            dimension_semantics=("parallel", "arbitrary"), vmem_limit_bytes=VMEM_LIMIT),
    )(x3, g1, w_br, w_gate, conv_w, conv_b, ln_g, ln_b, qag, kvag, wuq, wuqs, wuk, wuv,
      qng, qngs, kng, rc, rs1, rs2, vone, one_minus_lb, hgrn_ng, _hgrn_levels())


def _attn_kernel(q_ref, k_ref, v_ref, o_ref, s_ref, m_ref, acc_ref):
    i = pl.program_id(2)
    nt = (((1,), (1,)), ((), ()))
    heads = range(ATTN_HG)
    qs = [q_ref[0, h] for h in heads]
    odd = (i & 1) == 1

    def scores(h, tile0, ntiles):
        kt = k_ref[0, h, pl.ds(pl.multiple_of(tile0 * TQ, TQ), ntiles * TQ), :]
        return lax.dot_general(qs[h], kt, nt, preferred_element_type=F32)

    def fold(s):
        m = s[:, 0:LANE]
        for c in range(1, s.shape[1] // LANE):
            m = jnp.maximum(m, s[:, c * LANE:(c + 1) * LANE])
        return m

    def score_span(tile0, ntiles, m_in):
        out = []
        for h in heads:
            s = scores(h, tile0, ntiles)
            for t in range(ntiles):
                s_ref[h, tile0 + t] = s[:, t * TQ:(t + 1) * TQ]
            out.append(jnp.maximum(m_in[h], fold(s)))
        return tuple(out)

    def pv_span(tile0, ntiles):
        out = []
        for h in heads:
            s = jnp.concatenate([s_ref[h, tile0 + t] for t in range(ntiles)], axis=1)
            p = jnp.exp2(s - ms[h]).astype(BF16)
            vt = v_ref[0, h, pl.ds(pl.multiple_of(tile0 * TQ, TQ), ntiles * TQ), :]
            out.append(jnp.dot(p, vt, preferred_element_type=F32))
        return out

    quads = lax.shift_right_logical(i, 2)
    mloc = lax.fori_loop(0, quads, lambda t, m: score_span(4 * t, 4, m),
                         tuple(jnp.full((TQ, LANE), -1e30, F32) for _ in heads))
    for h in heads:
        m_ref[h] = mloc[h]

    @pl.when((i & 2) != 0)
    def _():
        m_new = score_span(4 * quads, 2, [m_ref[h] for h in heads])
        for h in heads:
            m_ref[h] = m_new[h]

    keep = (lax.broadcasted_iota(jnp.int32, (TQ, TQ), 0)
            >= lax.broadcasted_iota(jnp.int32, (TQ, TQ), 1))

    @pl.when(odd)
    def _():
        for h in heads:
            s = scores(h, i - 1, 2)
            lo, hi = s[:, :TQ], jnp.where(keep, s[:, TQ:], -1e30)
            s_ref[h, i - 1] = lo
            s_ref[h, i] = hi
            m_ref[h] = jnp.maximum(m_ref[h], jnp.maximum(fold(lo), fold(hi)))

    @pl.when(jnp.logical_not(odd))
    def _():
        for h in heads:
            s = jnp.where(keep, scores(h, i, 1), -1e30)
            s_ref[h, i] = s
            m_ref[h] = jnp.maximum(m_ref[h], fold(s))

    ms = [jnp.max(m_ref[h], axis=-1, keepdims=True) for h in heads]

    quads2 = lax.shift_right_logical(i + 1, 2)

    def pass2(t, acc):
        part = pv_span(4 * t, 4)
        return tuple(acc[h] + part[h] for h in heads)

    acc = lax.fori_loop(0, quads2, pass2, tuple(jnp.zeros((TQ, HEAD_SLOT), F32) for _ in heads))
    for h in heads:
        acc_ref[h] = acc[h]

    @pl.when(((i + 1) & 2) != 0)
    def _():
        part = pv_span(4 * quads2, 2)
        for h in heads:
            acc_ref[h] += part[h]

    @pl.when(jnp.logical_not(odd))
    def _():
        part = pv_span(i, 1)
        for h in heads:
            acc_ref[h] += part[h]

    first = lax.broadcasted_iota(jnp.int32, (TQ, HEAD_SLOT), 1) < V_DIM
    for hp in range(ATTN_HG // 2):
        a, b = acc_ref[2 * hp], acc_ref[2 * hp + 1]
        den = pltpu.roll(jnp.where(first, b, a), V_DIM, 1)
        o_ref[0, :, hp * HEAD_SLOT:(hp + 1) * HEAD_SLOT] = (
            jnp.where(first, a, b) / den).astype(o_ref.dtype)


def _attention(q, k, v):
    b, nh, lp, hs = q.shape
    nq = lp // TQ
    return pl.pallas_call(
        _attn_kernel,
        out_shape=jax.ShapeDtypeStruct((b, lp, nh * V_DIM), BF16),
        grid=(b, nh // ATTN_HG, nq),
        in_specs=[
            pl.BlockSpec((1, ATTN_HG, TQ, hs), lambda bb, g, i: (bb, g, i, 0)),
            pl.BlockSpec((1, ATTN_HG, lp, hs), lambda bb, g, i: (bb, g, 0, 0)),
            pl.BlockSpec((1, ATTN_HG, lp, hs), lambda bb, g, i: (bb, g, 0, 0)),
        ],
        out_specs=pl.BlockSpec((1, TQ, ATTN_HG * V_DIM), lambda bb, g, i: (bb, i, g)),
        scratch_shapes=[pltpu.VMEM((ATTN_HG, nq, TQ, TQ), F32),
                        pltpu.VMEM((ATTN_HG, TQ, LANE), F32),
                        pltpu.VMEM((ATTN_HG, TQ, HEAD_SLOT), F32)],
        compiler_params=pltpu.CompilerParams(
            dimension_semantics=("parallel", "parallel", "arbitrary"), vmem_limit_bytes=VMEM_LIMIT),
    )(q, k, v)


def _merge_kernel(ha_ref, ob_ref, oc_ref, gate_ref, x_ref, wa_ref, wb_ref, wc_ref, wo_ref, o_ref):
    mix = None
    for n, (br_ref, w_ref) in enumerate(((ha_ref, wa_ref), (ob_ref, wb_ref), (oc_ref, wc_ref))):
        y = jnp.dot(br_ref[...], w_ref[...], preferred_element_type=F32)
        per = D_MODEL // TN_IN
        gate = jnp.concatenate([gate_ref[n * per + t] for t in range(per)], axis=1)
        y = y * gate.astype(F32)
        mix = y if mix is None else mix + y
    o_ref[...] = x_ref[...] + jnp.dot(mix.astype(BF16), wo_ref[...], preferred_element_type=F32)


def _merge(ha, ob, oc, gates, x2, wa, wb, wc, wo):
    t = x2.shape[0]
    br = lambda: pl.BlockSpec((TM_MERGE, 512), lambda i: (i, 0))
    wbr = lambda: pl.BlockSpec((512, D_MODEL), lambda i: (0, 0))
    return pl.pallas_call(
        _merge_kernel,
        out_shape=jax.ShapeDtypeStruct((t, D_MODEL), F32),
        grid=(t // TM_MERGE,),
        in_specs=[br(), br(), br(),
                  pl.BlockSpec((N_GATE_CHUNK, TM_MERGE, TN_IN), lambda i: (0, i, 0)),
                  pl.BlockSpec((TM_MERGE, D_MODEL), lambda i: (i, 0)),
                  wbr(), wbr(), wbr(),
                  pl.BlockSpec((D_MODEL, D_MODEL), lambda i: (0, 0))],
        out_specs=pl.BlockSpec((TM_MERGE, D_MODEL), lambda i: (i, 0)),
        compiler_params=pltpu.CompilerParams(
            dimension_semantics=("parallel",), vmem_limit_bytes=VMEM_LIMIT),
    )(ha, ob, oc, gates, x2, wa, wb, wc, wo)


def _ffn_kernel(x_ref, g_ref, w1_ref, w2_ref, o_ref):
    x = x_ref[...]
    h = (_rms(x) * g_ref[...]).astype(BF16)
    acc = x
    for c in range(D_FF // TF_FF):
        cols = slice(c * TF_FF, (c + 1) * TF_FF)
        a = jnp.maximum(jnp.dot(h, w1_ref[:, cols], preferred_element_type=F32), 0.0)
        acc = acc + jnp.dot((a * a).astype(BF16), w2_ref[cols, :], preferred_element_type=F32)
    o_ref[...] = acc


def _ffn(x2, g, w1, w2):
    t = x2.shape[0]
    return pl.pallas_call(
        _ffn_kernel,
        out_shape=jax.ShapeDtypeStruct((t, D_MODEL), F32),
        grid=(t // TM_FF,),
        in_specs=[
            pl.BlockSpec((TM_FF, D_MODEL), lambda i: (i, 0)),
            pl.BlockSpec((1, D_MODEL), lambda i: (0, 0)),
            pl.BlockSpec((D_MODEL, D_FF), lambda i: (0, 0), pipeline_mode=pl.Buffered(1)),
            pl.BlockSpec((D_FF, D_MODEL), lambda i: (0, 0), pipeline_mode=pl.Buffered(1)),
        ],
        out_specs=pl.BlockSpec((TM_FF, D_MODEL), lambda i: (i, 0)),
        compiler_params=pltpu.CompilerParams(
            dimension_semantics=("parallel",), vmem_limit_bytes=VMEM_LIMIT),
    )(x2, g, w1, w2)


def _regroup_w_in(w):
    o = 0
    conv = w[:, o:o + 2 * CONV_DIM]; o += 2 * CONV_DIM
    cq = w[:, o:o + Q_RANK]; o += Q_RANK
    ckv = w[:, o:o + KV_RANK]; o += KV_RANK
    kr = w[:, o:o + ROPE_DIM]; o += ROPE_DIM
    hg = w[:, o:o + 4 * HGRN_HEADS * HGRN_DK]; o += 4 * HGRN_HEADS * HGRN_DK
    gate = w[:, o:]
    kr_slot = jnp.pad(kr, ((0, 0), (NOPE_DIM, LANE - QK_DIM)))
    return jnp.concatenate([gate, conv, cq, ckv, kr_slot, hg], axis=1).astype(BF16)


def _head_slots(w, per_head, start, width):
    r = w.shape[0]
    wh = w.reshape(r, MLA_HEADS, per_head)[:, :, start:start + width]
    wh = jnp.pad(wh, ((0, 0), (0, 0), (0, HEAD_SLOT - width)))
    return wh.reshape(r, MLA_HEADS * HEAD_SLOT)


def _swap_rope_halves(a):
    half = ROPE_DIM // 2
    return jnp.concatenate([jnp.zeros_like(a[..., :NOPE_DIM]), a[..., NOPE_DIM + half:],
                            a[..., NOPE_DIM:NOPE_DIM + half]], axis=-1)


def _v_slots(w_ukv):
    r = w_ukv.shape[0]
    wv = w_ukv.reshape(r, MLA_HEADS // 2, 2, NOPE_DIM + V_DIM)[..., NOPE_DIM:]
    z = jnp.zeros_like(wv[:, :, 0])
    slots = jnp.stack([jnp.concatenate([wv[:, :, 0], z], axis=-1),
                       jnp.concatenate([z, wv[:, :, 1]], axis=-1)], axis=2)
    ones = jnp.ones((MLA_HEADS // 2, V_DIM), F32)
    zo = jnp.zeros_like(ones)
    vone = jnp.stack([jnp.concatenate([zo, ones], axis=-1),
                      jnp.concatenate([ones, zo], axis=-1)], axis=1)
    return slots.reshape(r, MLA_HEADS * HEAD_SLOT), vone.reshape(1, MLA_HEADS * HEAD_SLOT)


def _rope_tables(lp):
    half = ROPE_DIM // 2
    pos = jnp.arange(lp, dtype=F32)
    inv_freq = ROPE_BASE ** (-jnp.arange(half, dtype=F32) / half)
    ang = pos[:, None] * inv_freq[None, :]
    cos, sin = jnp.cos(ang), jnp.sin(ang)
    ones = jnp.ones((lp, NOPE_DIM), F32)
    z16 = jnp.zeros((lp, half), F32)
    z64 = jnp.zeros((lp, NOPE_DIM), F32)
    tail = jnp.zeros((lp, LANE - QK_DIM), F32)
    rc = jnp.concatenate([ones, cos, cos, tail], axis=1)
    rs1 = jnp.concatenate([z64, -sin, z16, tail], axis=1)
    rs2 = jnp.concatenate([z64, z16, sin, tail], axis=1)
    return rc, rs1, rs2


def kernel(x, meta, norm1_g, w_in, conv_w, conv_b, conv_ln_g, conv_ln_b, w_conv_out, q_a_norm_g, w_uq, kv_a_norm_g, w_ukv, q_norm_g, k_norm_g, w_attn_out, hgrn_lb_logits, hgrn_norm_g, w_hgrn_out, w_out, norm2_g, w_ff1, w_ff2):
    b, seq, d = x.shape
    depth = w_in.shape[0]
    l = seq + N_META
    lp = -(-l // TL) * TL
    xc = jnp.concatenate([jnp.broadcast_to(meta[None].astype(x.dtype), (b, N_META, d)), x,
                          jnp.zeros((b, lp - l, d), x.dtype)], axis=1)
    x2 = xc.reshape(b * lp, d)

    rc, rs1, rs2 = _rope_tables(lp)
    p_lb = jax.nn.softmax(hgrn_lb_logits.astype(F32), axis=0)
    lower_bounds = jnp.cumsum(p_lb, axis=0) - p_lb[0:1]
    row = lambda a: a.astype(F32).reshape(1, -1)
    pad_qk = lambda g: jnp.pad(g.astype(F32), (0, LANE - QK_DIM)).reshape(1, LANE)

    for li in range(depth):
        wuv, vone = _v_slots(w_ukv[li])
        w_uq_heads = w_uq[li].reshape(Q_RANK, MLA_HEADS, QK_DIM)
        w_uq_swapped = _swap_rope_halves(w_uq_heads).reshape(Q_RANK, MLA_HEADS * QK_DIM)
        q_gain = q_norm_g[li].astype(F32) * (QK_DIM ** -0.5 * LOG2E)
        w_all = _regroup_w_in(w_in[li])
        w_gate = w_all[:, :COL_CONV].reshape(d, N_GATE_CHUNK, TN_IN).transpose(1, 0, 2)
        gates, ha, q, k, v, oc = _front(
            x2.reshape(b, lp, d), row(norm1_g[li]), w_all[:, COL_CONV:], w_gate,
            jnp.repeat(conv_w[li].astype(F32), SUBLANE, axis=0),
            row(conv_b[li]), row(conv_ln_g[li]), row(conv_ln_b[li]),
            row(q_a_norm_g[li]), row(kv_a_norm_g[li]),
            _head_slots(w_uq[li], QK_DIM, 0, QK_DIM).astype(BF16),
            _head_slots(w_uq_swapped, QK_DIM, 0, QK_DIM).astype(BF16),
            _head_slots(w_ukv[li], NOPE_DIM + V_DIM, 0, NOPE_DIM).astype(BF16),
            wuv.astype(BF16),
            pad_qk(q_gain), pad_qk(_swap_rope_halves(q_gain)), pad_qk(k_norm_g[li]),
            rc, rs1, rs2, vone, 1.0 - row(lower_bounds[li]), row(hgrn_norm_g[li]))
        ob = _attention(q, k, v)
        flat = lambda a: a.reshape(b * lp, -1)
        x2 = _merge(flat(ha), flat(ob), flat(oc), gates.reshape(N_GATE_CHUNK, b * lp, TN_IN), x2,
                    w_conv_out[li].astype(BF16), w_attn_out[li].astype(BF16),
                    w_hgrn_out[li].astype(BF16), w_out[li].astype(BF16))
        x2 = _ffn(x2, row(norm2_g[li]), w_ff1[li].astype(BF16), w_ff2[li].astype(BF16))

    return x2.reshape(b, lp, d)[:, N_META:l]
```

```python
import functools

import jax
import jax.numpy as jnp
import numpy as np
from jax import lax
from jax.experimental import pallas as pl
from jax.experimental.pallas import tpu as pltpu

F32 = jnp.float32
BF16 = jnp.bfloat16

D_MODEL = 1024
N_META = 16
EPS = 1e-6
GATE_CLAMP = 1.0 - 1e-6
CONV_DIM = 512
CONV_K = 31
MLA_HEADS = 8
Q_RANK = 256
KV_RANK = 128
NOPE_DIM = 64
ROPE_DIM = 32
V_DIM = 64
QK_DIM = NOPE_DIM + ROPE_DIM
ROPE_BASE = 10000.0
HGRN_HEADS = 4
HGRN_DK = 128
HGRN_DV = 128
D_FF = 4096

LANE = 128
HEAD_SLOT = LANE

COL_GATE = 0
COL_CONV = 3072
COL_CQ = 4096
COL_CKV = 4352
COL_KROPE = 4480
COL_HGRN = 4608
N_IN_PAD = 6656

VMEM_LIMIT = 52 * 1024 * 1024

TL = 384
TN_IN = 512
N_GATE_CHUNK = COL_CONV // TN_IN
SUBLANE = 8
RC_CONV = 32
CONV_HALO = 32
CONV_SPAN = TL + CONV_HALO - SUBLANE
RC_PREP = 128
TQ = 384
ATTN_HG = 4
LOG2E = 1.4426950408889634
HGRN_C = 128
HGRN_BLK = 4
TM_MERGE = 1024
TM_FF = 1024
TF_FF = 1024


def _rms(x, eps=EPS):
    return x * lax.rsqrt(jnp.mean(x * x, axis=-1, keepdims=True) + eps)


def _sigmoid(x):
    return 0.5 * jnp.tanh(0.5 * x) + 0.5


def _silu(x):
    return x * _sigmoid(x)


def _conv_stage(glu, j, hbuf, sbuf):
    @pl.when(j == 0)
    def _():
        hbuf[0:CONV_HALO, :] = jnp.zeros((CONV_HALO, CONV_DIM), F32)

    @pl.when(j > 0)
    def _():
        hbuf[0:CONV_HALO, :] = hbuf[TL:TL + CONV_HALO, :]

    hbuf[CONV_HALO:CONV_HALO + TL, :] = glu

    for r in range(1, SUBLANE):
        sbuf[r - 1] = hbuf[r:r + CONV_SPAN, :]


def _conv_rows(r0, w_ref, cb_ref, lg_ref, lb_ref, o_ref, hbuf, sbuf):
    base = CONV_HALO - (CONV_K - 1)
    acc = jnp.zeros((RC_CONV, CONV_DIM), F32) + cb_ref[...]
    for k in range(CONV_K):
        r = (base + k) % SUBLANE
        lo = pl.multiple_of(r0 + (base + k - r), SUBLANE)
        src = hbuf if r == 0 else sbuf.at[r - 1]
        wk = jnp.tile(w_ref[k * SUBLANE:(k + 1) * SUBLANE, :], (RC_CONV // SUBLANE, 1))
        acc = acc + wk * src[pl.ds(lo, RC_CONV), :]
    mu = jnp.mean(acc, axis=-1, keepdims=True)
    xc = acc - mu
    y = xc * lax.rsqrt(jnp.mean(xc * xc, axis=-1, keepdims=True) + EPS)
    y = y * lg_ref[...] + lb_ref[...]
    o_ref[0, pl.ds(r0, RC_CONV), :] = (y * jax.nn.sigmoid(y)).astype(o_ref.dtype)


def _rope(x, c, s1, s2):
    half = ROPE_DIM // 2
    return x * c + pltpu.roll(x, LANE - half, 1) * s1 + pltpu.roll(x, half, 1) * s2


def _lane_sumsq(x, ones):
    sq = x * x
    hi = sq.astype(BF16)
    lo = (sq - hi.astype(F32)).astype(BF16)
    return jnp.dot(jnp.concatenate([hi, lo], axis=1), ones, preferred_element_type=F32)


def _mla_rows(cq, ckv, kr, rows, qag_ref, kvag_ref, wuq_ref, wuqs_ref, wuk_ref, wuv_ref,
              qng_ref, qngs_ref, kng_ref, rc_ref, rs1_ref, rs2_ref, vone_ref, q_out, k_out, v_out):
    inv_d = 1.0 / QK_DIM
    ones = jnp.ones((2 * LANE, LANE), BF16)
    cq = (_rms(cq) * qag_ref[...]).astype(BF16)
    q = jnp.dot(cq, wuq_ref[...], preferred_element_type=F32)
    qs = jnp.dot(cq, wuqs_ref[...], preferred_element_type=F32)
    ckv = (_rms(ckv) * kvag_ref[...]).astype(BF16)
    kn = jnp.dot(ckv, wuk_ref[...], preferred_element_type=F32)
    v = jnp.dot(ckv, wuv_ref[...], preferred_element_type=F32) + vone_ref[...]
    rc, rs1, rs2 = rc_ref[rows, :], rs1_ref[rows, :], rs2_ref[rows, :]
    q_c = qng_ref[...] * rc
    q_s = qngs_ref[...] * (rs1 + rs2)
    kr_ss = _lane_sumsq(kr, ones)
    kr_rot = _rope(kr * kng_ref[...], rc, rs1, rs2)
    for h in range(MLA_HEADS):
        sl = slice(h * HEAD_SLOT, (h + 1) * HEAD_SLOT)
        qh = q[:, sl]
        q_out[0, h, rows, :] = ((qh * q_c + qs[:, sl] * q_s) * lax.rsqrt(
            _lane_sumsq(qh, ones) * inv_d + EPS)).astype(q_out.dtype)
        kh = kn[:, sl]
        k_out[0, h, rows, :] = ((kh * kng_ref[...] + kr_rot) * lax.rsqrt(
            (_lane_sumsq(kh, ones) + kr_ss) * inv_d + EPS)).astype(k_out.dtype)
        v_out[0, h, rows, :] = v[:, sl].astype(v_out.dtype)


def _hgrn_chunk(q, fr, iv, gv, rows, oml_ref, ng_ref, lvl, st_ref, o_ref):
    C = HGRN_C
    rowid = lax.broadcasted_iota(jnp.int32, (C, HGRN_DK), 0)
    nt = (((1,), (1,)), ((), ()))
    tn = (((0,), (0,)), ((), ()))
    for h in range(HGRN_HEADS):
        sl = slice(h * HGRN_DK, (h + 1) * HGRN_DK)
        qh = q[:, sl]
        kk = oml_ref[:, sl] * jax.nn.sigmoid(-fr[:, sl])
        lf = jnp.log1p(-jnp.minimum(kk, GATE_CLAMP)) * LOG2E
        v = _silu(iv[:, sl]).astype(BF16)

        fwd = lf
        bwd = jnp.zeros_like(lf)
        n, level = 1, 0
        a = None
        while n < C:
            if n >= HGRN_BLK:
                qs = (qh * jnp.exp2(fwd)).astype(BF16)
                if a is None:
                    a = jnp.where(lvl == 0, lax.dot_general(
                        qs, (kk * jnp.exp2(-fwd)).astype(BF16), nt,
                        preferred_element_type=F32), 0.0)
                level += 1
                ks = (kk * jnp.exp2(bwd)).astype(BF16)
                a = jnp.where(lvl == level,
                              lax.dot_general(qs, ks, nt, preferred_element_type=F32), a)
            tot = fwd + bwd
            upper = (rowid & n) != 0
            fwd = fwd + jnp.where(upper, pltpu.roll(tot, n, 0), 0.0)
            bwd = bwd + jnp.where(upper, 0.0, pltpu.roll(tot, C - n, 0))
            n *= 2

        st = st_ref[h]
        o = jnp.dot(a.astype(BF16), v, preferred_element_type=F32)
        o = o + lax.dot_general((qh * jnp.exp2(fwd)).astype(BF16), st.astype(BF16), nt,
                                preferred_element_type=F32)
        ks = (kk * jnp.exp2(bwd)).astype(BF16)
        st_ref[h] = st * jnp.exp2(fwd[C - 1:C, :]) + lax.dot_general(
            v, ks, tn, preferred_element_type=F32)

        o = _rms(o) * ng_ref[:, sl]
        o_ref[0, rows, sl] = (o * _silu(gv[:, sl])).astype(o_ref.dtype)


def _hgrn_levels():
    t = np.arange(HGRN_C)[:, None]
    s = np.arange(HGRN_C)[None, :]
    x = (t ^ s) // HGRN_BLK
    lvl = np.where(x == 0, 0, np.floor(np.log2(np.maximum(x, 1))).astype(np.int64) + 1)
    return jnp.asarray(np.where(s > t, -1, lvl), dtype=jnp.int32)


def _front_kernel(x_ref, g1_ref, wbr_ref, wgate_ref,
                  cw_ref, cb_ref, lg_ref, lb_ref,
                  qag_ref, kvag_ref, wuq_ref, wuqs_ref, wuk_ref, wuv_ref, qng_ref, qngs_ref, kng_ref,
                  rc_ref, rs1_ref, rs2_ref, vone_ref,
                  oml_ref, ng_ref, lvl_ref,
                  gate_out, ha_out, q_out, k_out, v_out, oc_out,
                  h_scr, ym_scr, yh_scr, hbuf, sbuf, st_ref):
    j = pl.program_id(1)
    hw = HGRN_HEADS * HGRN_DK
    n_mla = COL_HGRN - COL_CQ

    @pl.when(j == 0)
    def _():
        st_ref[...] = jnp.zeros(st_ref.shape, F32)

    h = (_rms(x_ref[0]) * g1_ref[...]).astype(BF16)
    h_scr[...] = h
    yc = jnp.dot(h, wbr_ref[:, 0:2 * CONV_DIM], preferred_element_type=F32)
    _conv_stage(yc[:, :CONV_DIM] * _sigmoid(yc[:, CONV_DIM:]), j, hbuf, sbuf)
    ym_scr[...] = jnp.dot(h, wbr_ref[:, 2 * CONV_DIM:2 * CONV_DIM + n_mla],
                          preferred_element_type=F32)
    yh_scr[...] = jnp.dot(h, wbr_ref[:, 2 * CONV_DIM + n_mla:], preferred_element_type=F32)

    conv_per_gate = (TL // RC_CONV) // N_GATE_CHUNK

    def gate_and_conv(c, carry):
        gate_out[c, 0] = _sigmoid(jnp.dot(h_scr[...], wgate_ref[c],
                                          preferred_element_type=F32)).astype(gate_out.dtype)
        for t in range(conv_per_gate):
            r0 = pl.multiple_of((c * conv_per_gate + t) * RC_CONV, RC_CONV)
            _conv_rows(r0, cw_ref, cb_ref, lg_ref, lb_ref, ha_out, hbuf, sbuf)
        return carry

    for c in range(N_GATE_CHUNK):
        gate_and_conv(c, 0)

    def mla_chunk(c, carry):
        rows = pl.ds(pl.multiple_of(c * RC_PREP, RC_PREP), RC_PREP)
        _mla_rows(ym_scr[rows, 0:Q_RANK], ym_scr[rows, Q_RANK:Q_RANK + KV_RANK],
                  ym_scr[rows, Q_RANK + KV_RANK:], rows,
                  qag_ref, kvag_ref, wuq_ref, wuqs_ref, wuk_ref, wuv_ref,
                  qng_ref, qngs_ref, kng_ref, rc_ref, rs1_ref, rs2_ref, vone_ref, q_out, k_out, v_out)
        return carry

    for c in range(TL // RC_PREP):
        mla_chunk(c, 0)

    lvl = lvl_ref[...]

    def hgrn_chunk(c, carry):
        rows = pl.ds(pl.multiple_of(c * HGRN_C, HGRN_C), HGRN_C)
        _hgrn_chunk(yh_scr[rows, 0:hw], yh_scr[rows, hw:2 * hw], yh_scr[rows, 2 * hw:3 * hw],
                    yh_scr[rows, 3 * hw:], rows, oml_ref, ng_ref, lvl, st_ref, oc_out)
        return carry

    lax.fori_loop(0, TL // HGRN_C, hgrn_chunk, 0)


def _front(x3, g1, w_br, w_gate, conv_w, conv_b, ln_g, ln_b, qag, kvag, wuq, wuqs, wuk, wuv,
           qng, qngs, kng, rc, rs1, rs2, vone, one_minus_lb, hgrn_ng):
    b, lp, d = x3.shape
    hw = MLA_HEADS * HEAD_SLOT
    gw = HGRN_HEADS * HGRN_DK
    n_br = N_IN_PAD - COL_CONV
    const = lambda shape: pl.BlockSpec(shape, lambda bb, j: (0,) * len(shape))
    resident = lambda shape: pl.BlockSpec(shape, lambda bb, j: (0,) * len(shape),
                                          pipeline_mode=pl.Buffered(1))
    tab = pl.BlockSpec((TL, LANE), lambda bb, j: (j, 0))
    seq = lambda w: pl.BlockSpec((1, TL, w), lambda bb, j: (bb, j, 0))
    heads = pl.BlockSpec((1, MLA_HEADS, TL, HEAD_SLOT), lambda bb, j: (bb, 0, j, 0))
    qkv = jax.ShapeDtypeStruct((b, MLA_HEADS, lp, HEAD_SLOT), BF16)
    return pl.pallas_call(
        _front_kernel,
        out_shape=(jax.ShapeDtypeStruct((N_GATE_CHUNK, b, lp, TN_IN), BF16),
                   jax.ShapeDtypeStruct((b, lp, CONV_DIM), BF16),
                   qkv, qkv, qkv,
                   jax.ShapeDtypeStruct((b, lp, gw), BF16)),
        grid=(b, lp // TL),
        in_specs=[
            seq(d), const((1, d)), resident((d, n_br)), resident((N_GATE_CHUNK, d, TN_IN)),
            const((CONV_K * SUBLANE, CONV_DIM)), const((1, CONV_DIM)), const((1, CONV_DIM)),
            const((1, CONV_DIM)),
            const((1, Q_RANK)), const((1, KV_RANK)),
            const((Q_RANK, hw)), const((Q_RANK, hw)), const((KV_RANK, hw)), const((KV_RANK, hw)),
            const((1, LANE)), const((1, LANE)), const((1, LANE)),
            tab, tab, tab, const((1, hw)),
            const((1, gw)), const((1, gw)), const((HGRN_C, HGRN_C)),
        ],
        out_specs=(pl.BlockSpec((N_GATE_CHUNK, 1, TL, TN_IN), lambda bb, j: (0, bb, j, 0)),
                   seq(CONV_DIM), heads, heads, heads, seq(gw)),
        scratch_shapes=[pltpu.VMEM((TL, d), BF16),
                        pltpu.VMEM((TL, COL_HGRN - COL_CQ), F32),
                        pltpu.VMEM((TL, 4 * gw), F32),
                        pltpu.VMEM((CONV_HALO + TL, CONV_DIM), F32),
                        pltpu.VMEM((SUBLANE - 1, CONV_SPAN, CONV_DIM), F32),
                        pltpu.VMEM((HGRN_HEADS, HGRN_DV, HGRN_DK), F32)],
        compiler_params=pltpu.CompilerParams(
            dimension_semantics=("parallel", "arbitrary"), vmem_limit_bytes=VMEM_LIMIT),
    )(x3, g1, w_br, w_gate, conv_w, conv_b, ln_g, ln_b, qag, kvag, wuq, wuqs, wuk, wuv,
      qng, qngs, kng, rc, rs1, rs2, vone, one_minus_lb, hgrn_ng, _hgrn_levels())


def _attn_kernel(q_ref, k_ref, v_ref, o_ref, s_ref, m_ref, acc_ref):
    i = pl.program_id(2)
    nt = (((1,), (1,)), ((), ()))
    heads = range(ATTN_HG)
    qs = [q_ref[0, h] for h in heads]
    odd = (i & 1) == 1

    def scores(h, tile0, ntiles):
        kt = k_ref[0, h, pl.ds(pl.multiple_of(tile0 * TQ, TQ), ntiles * TQ), :]
        return lax.dot_general(qs[h], kt, nt, preferred_element_type=F32)

    def fold(s):
        m = s[:, 0:LANE]
        for c in range(1, s.shape[1] // LANE):
            m = jnp.maximum(m, s[:, c * LANE:(c + 1) * LANE])
        return m

    def score_span(tile0, ntiles, m_in):
        out = []
        for h in heads:
            s = scores(h, tile0, ntiles)
            for t in range(ntiles):
                s_ref[h, tile0 + t] = s[:, t * TQ:(t + 1) * TQ]
            out.append(jnp.maximum(m_in[h], fold(s)))
        return tuple(out)

    def pv_span(tile0, ntiles):
        out = []
        for h in heads:
            s = jnp.concatenate([s_ref[h, tile0 + t] for t in range(ntiles)], axis=1)
            p = jnp.exp2(s - ms[h]).astype(BF16)
            vt = v_ref[0, h, pl.ds(pl.multiple_of(tile0 * TQ, TQ), ntiles * TQ), :]
            out.append(jnp.dot(p, vt, preferred_element_type=F32))
        return out

    quads = lax.shift_right_logical(i, 2)
    mloc = lax.fori_loop(0, quads, lambda t, m: score_span(4 * t, 4, m),
                         tuple(jnp.full((TQ, LANE), -1e30, F32) for _ in heads))
    for h in heads:
        m_ref[h] = mloc[h]

    @pl.when((i & 2) != 0)
    def _():
        m_new = score_span(4 * quads, 2, [m_ref[h] for h in heads])
        for h in heads:
            m_ref[h] = m_new[h]

    keep = (lax.broadcasted_iota(jnp.int32, (TQ, TQ), 0)
            >= lax.broadcasted_iota(jnp.int32, (TQ, TQ), 1))

    @pl.when(odd)
    def _():
        for h in heads:
            s = scores(h, i - 1, 2)
            lo, hi = s[:, :TQ], jnp.where(keep, s[:, TQ:], -1e30)
            s_ref[h, i - 1] = lo
            s_ref[h, i] = hi
            m_ref[h] = jnp.maximum(m_ref[h], jnp.maximum(fold(lo), fold(hi)))

    @pl.when(jnp.logical_not(odd))
    def _():
        for h in heads:
            s = jnp.where(keep, scores(h, i, 1), -1e30)
            s_ref[h, i] = s
            m_ref[h] = jnp.maximum(m_ref[h], fold(s))

    ms = [jnp.max(m_ref[h], axis=-1, keepdims=True) for h in heads]

    quads2 = lax.shift_right_logical(i + 1, 2)

    def pass2(t, acc):
        part = pv_span(4 * t, 4)
        return tuple(acc[h] + part[h] for h in heads)

    acc = lax.fori_loop(0, quads2, pass2, tuple(jnp.zeros((TQ, HEAD_SLOT), F32) for _ in heads))
    for h in heads:
        acc_ref[h] = acc[h]

    @pl.when(((i + 1) & 2) != 0)
    def _():
        part = pv_span(4 * quads2, 2)
        for h in heads:
            acc_ref[h] += part[h]

    @pl.when(jnp.logical_not(odd))
    def _():
        part = pv_span(i, 1)
        for h in heads:
            acc_ref[h] += part[h]

    first = lax.broadcasted_iota(jnp.int32, (TQ, HEAD_SLOT), 1) < V_DIM
    for hp in range(ATTN_HG // 2):
        a, b = acc_ref[2 * hp], acc_ref[2 * hp + 1]
        den = pltpu.roll(jnp.where(first, b, a), V_DIM, 1)
        o_ref[0, :, hp * HEAD_SLOT:(hp + 1) * HEAD_SLOT] = (
            jnp.where(first, a, b) / den).astype(o_ref.dtype)


def _attention(q, k, v):
    b, nh, lp, hs = q.shape
    nq = lp // TQ
    return pl.pallas_call(
        _attn_kernel,
        out_shape=jax.ShapeDtypeStruct((b, lp, nh * V_DIM), BF16),
        grid=(b, nh // ATTN_HG, nq),
        in_specs=[
            pl.BlockSpec((1, ATTN_HG, TQ, hs), lambda bb, g, i: (bb, g, i, 0)),
            pl.BlockSpec((1, ATTN_HG, lp, hs), lambda bb, g, i: (bb, g, 0, 0)),
            pl.BlockSpec((1, ATTN_HG, lp, hs), lambda bb, g, i: (bb, g, 0, 0)),
        ],
        out_specs=pl.BlockSpec((1, TQ, ATTN_HG * V_DIM), lambda bb, g, i: (bb, i, g)),
        scratch_shapes=[pltpu.VMEM((ATTN_HG, nq, TQ, TQ), F32),
                        pltpu.VMEM((ATTN_HG, TQ, LANE), F32),
                        pltpu.VMEM((ATTN_HG, TQ, HEAD_SLOT), F32)],
        compiler_params=pltpu.CompilerParams(
            dimension_semantics=("parallel", "parallel", "arbitrary"), vmem_limit_bytes=VMEM_LIMIT),
    )(q, k, v)


def _merge_kernel(ha_ref, ob_ref, oc_ref, gate_ref, x_ref, wa_ref, wb_ref, wc_ref, wo_ref, o_ref):
    mix = None
    for n, (br_ref, w_ref) in enumerate(((ha_ref, wa_ref), (ob_ref, wb_ref), (oc_ref, wc_ref))):
        y = jnp.dot(br_ref[...], w_ref[...], preferred_element_type=F32)
        per = D_MODEL // TN_IN
        gate = jnp.concatenate([gate_ref[n * per + t] for t in range(per)], axis=1)
        y = y * gate.astype(F32)
        mix = y if mix is None else mix + y
    o_ref[...] = x_ref[...] + jnp.dot(mix.astype(BF16), wo_ref[...], preferred_element_type=F32)


def _merge(ha, ob, oc, gates, x2, wa, wb, wc, wo, li):
    t = x2.shape[0]
    br = lambda: pl.BlockSpec((TM_MERGE, 512), lambda i: (i, 0))
    wbr = lambda: pl.BlockSpec((None, 512, D_MODEL), lambda i: (li, 0, 0))
    return pl.pallas_call(
        _merge_kernel,
        out_shape=jax.ShapeDtypeStruct((t, D_MODEL), F32),
        grid=(t // TM_MERGE,),
        in_specs=[br(), br(), br(),
                  pl.BlockSpec((N_GATE_CHUNK, TM_MERGE, TN_IN), lambda i: (0, i, 0)),
                  pl.BlockSpec((TM_MERGE, D_MODEL), lambda i: (i, 0)),
                  wbr(), wbr(), wbr(),
                  pl.BlockSpec((None, D_MODEL, D_MODEL), lambda i: (li, 0, 0))],
        out_specs=pl.BlockSpec((TM_MERGE, D_MODEL), lambda i: (i, 0)),
        compiler_params=pltpu.CompilerParams(
            dimension_semantics=("parallel",), vmem_limit_bytes=VMEM_LIMIT),
    )(ha, ob, oc, gates, x2, wa, wb, wc, wo)


def _ffn_rows(x, g_ref, w1_ref, w2_ref):
    h = (_rms(x) * g_ref[...]).astype(BF16)
    acc = x
    for c in range(D_FF // TF_FF):
        cols = slice(c * TF_FF, (c + 1) * TF_FF)
        a = jnp.maximum(jnp.dot(h, w1_ref[:, cols], preferred_element_type=F32), 0.0)
        acc = acc + jnp.dot((a * a).astype(BF16), w2_ref[cols, :], preferred_element_type=F32)
    return acc


def _ffn_kernel(x_ref, g_ref, w1_ref, w2_ref, o_ref):
    o_ref[...] = _ffn_rows(x_ref[...], g_ref, w1_ref, w2_ref)


def _ffn_weight_specs(li):
    return [
        pl.BlockSpec((None, D_MODEL, D_FF), lambda *_: (li, 0, 0), pipeline_mode=pl.Buffered(1)),
        pl.BlockSpec((None, D_FF, D_MODEL), lambda *_: (li, 0, 0), pipeline_mode=pl.Buffered(1)),
    ]


def _ffn(x2, g, w1, w2, li):
    t = x2.shape[0]
    return pl.pallas_call(
        _ffn_kernel,
        out_shape=jax.ShapeDtypeStruct((t, D_MODEL), F32),
        grid=(t // TM_FF,),
        in_specs=[
            pl.BlockSpec((TM_FF, D_MODEL), lambda i: (i, 0)),
            pl.BlockSpec((1, D_MODEL), lambda i: (0, 0)),
        ] + _ffn_weight_specs(li),
        out_specs=pl.BlockSpec((TM_FF, D_MODEL), lambda i: (i, 0)),
        compiler_params=pltpu.CompilerParams(
            dimension_semantics=("parallel",), vmem_limit_bytes=VMEM_LIMIT),
    )(x2, g, w1, w2)


def _ffn_final(x2, g, w1, w2, li, b, lp, n_lead, n_out):
    assert lp % SUBLANE == 0 and n_lead % SUBLANE == 0
    return pl.pallas_call(
        _ffn_kernel,
        out_shape=jax.ShapeDtypeStruct((b, n_out, D_MODEL), F32),
        grid=(b, n_out // TM_FF),
        in_specs=[
            pl.BlockSpec((pl.Element(TM_FF), pl.Element(D_MODEL)),
                         lambda bb, t: (pl.multiple_of(bb * lp + n_lead + t * TM_FF, SUBLANE), 0)),
            pl.BlockSpec((1, D_MODEL), lambda bb, t: (0, 0)),
        ] + _ffn_weight_specs(li),
        out_specs=pl.BlockSpec((None, TM_FF, D_MODEL), lambda bb, t: (bb, t, 0)),
        compiler_params=pltpu.CompilerParams(
            dimension_semantics=("parallel", "parallel"), vmem_limit_bytes=VMEM_LIMIT),
    )(x2, g, w1, w2)


def _regroup_w_in(w):
    o = 0
    conv = w[:, o:o + 2 * CONV_DIM]; o += 2 * CONV_DIM
    cq = w[:, o:o + Q_RANK]; o += Q_RANK
    ckv = w[:, o:o + KV_RANK]; o += KV_RANK
    kr = w[:, o:o + ROPE_DIM]; o += ROPE_DIM
    hg = w[:, o:o + 4 * HGRN_HEADS * HGRN_DK]; o += 4 * HGRN_HEADS * HGRN_DK
    gate = w[:, o:]
    kr_slot = jnp.pad(kr, ((0, 0), (NOPE_DIM, LANE - QK_DIM)))
    return jnp.concatenate([gate, conv, cq, ckv, kr_slot, hg], axis=1).astype(BF16)


def _head_slots(w, per_head, start, width):
    r = w.shape[0]
    wh = w.reshape(r, MLA_HEADS, per_head)[:, :, start:start + width]
    wh = jnp.pad(wh, ((0, 0), (0, 0), (0, HEAD_SLOT - width)))
    return wh.reshape(r, MLA_HEADS * HEAD_SLOT)


def _swap_rope_halves(a):
    half = ROPE_DIM // 2
    return jnp.concatenate([jnp.zeros_like(a[..., :NOPE_DIM]), a[..., NOPE_DIM + half:],
                            a[..., NOPE_DIM:NOPE_DIM + half]], axis=-1)


def _v_slots(w_ukv):
    r = w_ukv.shape[0]
    wv = w_ukv.reshape(r, MLA_HEADS // 2, 2, NOPE_DIM + V_DIM)[..., NOPE_DIM:]
    z = jnp.zeros_like(wv[:, :, 0])
    slots = jnp.stack([jnp.concatenate([wv[:, :, 0], z], axis=-1),
                       jnp.concatenate([z, wv[:, :, 1]], axis=-1)], axis=2)
    ones = jnp.ones((MLA_HEADS // 2, V_DIM), F32)
    zo = jnp.zeros_like(ones)
    vone = jnp.stack([jnp.concatenate([zo, ones], axis=-1),
                      jnp.concatenate([ones, zo], axis=-1)], axis=1)
    return slots.reshape(r, MLA_HEADS * HEAD_SLOT), vone.reshape(1, MLA_HEADS * HEAD_SLOT)


def _rope_tables(lp):
    half = ROPE_DIM // 2
    pos = jnp.arange(lp, dtype=F32)
    inv_freq = ROPE_BASE ** (-jnp.arange(half, dtype=F32) / half)
    ang = pos[:, None] * inv_freq[None, :]
    cos, sin = jnp.cos(ang), jnp.sin(ang)
    ones = jnp.ones((lp, NOPE_DIM), F32)
    z16 = jnp.zeros((lp, half), F32)
    z64 = jnp.zeros((lp, NOPE_DIM), F32)
    tail = jnp.zeros((lp, LANE - QK_DIM), F32)
    rc = jnp.concatenate([ones, cos, cos, tail], axis=1)
    rs1 = jnp.concatenate([z64, -sin, z16, tail], axis=1)
    rs2 = jnp.concatenate([z64, z16, sin, tail], axis=1)
    return rc, rs1, rs2


def kernel(x, meta, norm1_g, w_in, conv_w, conv_b, conv_ln_g, conv_ln_b, w_conv_out, q_a_norm_g, w_uq, kv_a_norm_g, w_ukv, q_norm_g, k_norm_g, w_attn_out, hgrn_lb_logits, hgrn_norm_g, w_hgrn_out, w_out, norm2_g, w_ff1, w_ff2):
    b, seq, d = x.shape
    depth = w_in.shape[0]
    l = seq + N_META
    lp = -(-l // TL) * TL
    xc = jnp.concatenate([jnp.broadcast_to(meta[None].astype(x.dtype), (b, N_META, d)), x,
                          jnp.zeros((b, lp - l, d), x.dtype)], axis=1)
    x2 = xc.reshape(b * lp, d)

    rc, rs1, rs2 = _rope_tables(lp)
    p_lb = jax.nn.softmax(hgrn_lb_logits.astype(F32), axis=0)
    lower_bounds = jnp.cumsum(p_lb, axis=0) - p_lb[0:1]
    row = lambda a: a.astype(F32).reshape(1, -1)
    pad_qk = lambda g: jnp.pad(g.astype(F32), (0, LANE - QK_DIM)).reshape(1, LANE)
    w_a, w_b, w_c, w_o = (w.astype(BF16) for w in (w_conv_out, w_attn_out, w_hgrn_out, w_out))
    w_1, w_2 = w_ff1.astype(BF16), w_ff2.astype(BF16)

    for li in range(depth):
        wuv, vone = _v_slots(w_ukv[li])
        w_uq_heads = w_uq[li].reshape(Q_RANK, MLA_HEADS, QK_DIM)
        w_uq_swapped = _swap_rope_halves(w_uq_heads).reshape(Q_RANK, MLA_HEADS * QK_DIM)
        q_gain = q_norm_g[li].astype(F32) * (QK_DIM ** -0.5 * LOG2E)
        w_all = _regroup_w_in(w_in[li])
        w_gate = w_all[:, :COL_CONV].reshape(d, N_GATE_CHUNK, TN_IN).transpose(1, 0, 2)
        gates, ha, q, k, v, oc = _front(
            x2.reshape(b, lp, d), row(norm1_g[li]), w_all[:, COL_CONV:], w_gate,
            jnp.repeat(conv_w[li].astype(F32), SUBLANE, axis=0),
            row(conv_b[li]), row(conv_ln_g[li]), row(conv_ln_b[li]),
            row(q_a_norm_g[li]), row(kv_a_norm_g[li]),
            _head_slots(w_uq[li], QK_DIM, 0, QK_DIM).astype(BF16),
            _head_slots(w_uq_swapped, QK_DIM, 0, QK_DIM).astype(BF16),
            _head_slots(w_ukv[li], NOPE_DIM + V_DIM, 0, NOPE_DIM).astype(BF16),
            wuv.astype(BF16),
            pad_qk(q_gain), pad_qk(_swap_rope_halves(q_gain)), pad_qk(k_norm_g[li]),
            rc, rs1, rs2, vone, 1.0 - row(lower_bounds[li]), row(hgrn_norm_g[li]))
        ob = _attention(q, k, v)
        flat = lambda a: a.reshape(b * lp, -1)
        x2 = _merge(flat(ha), flat(ob), flat(oc), gates.reshape(N_GATE_CHUNK, b * lp, TN_IN), x2,
                    w_a, w_b, w_c, w_o, li)
        if li + 1 < depth:
            x2 = _ffn(x2, row(norm2_g[li]), w_1, w_2, li)
    assert seq % TM_FF == 0
    return _ffn_final(x2, row(norm2_g[depth - 1]), w_1, w_2, depth - 1, b, lp, N_META, seq)
```

```python
import functools

import jax
import jax.numpy as jnp
import numpy as np
from jax import lax
from jax.experimental import pallas as pl
from jax.experimental.pallas import tpu as pltpu

F32 = jnp.float32
BF16 = jnp.bfloat16

D_MODEL = 1024
N_META = 16
EPS = 1e-6
GATE_CLAMP = 1.0 - 1e-6
CONV_DIM = 512
CONV_K = 31
MLA_HEADS = 8
Q_RANK = 256
KV_RANK = 128
NOPE_DIM = 64
ROPE_DIM = 32
V_DIM = 64
QK_DIM = NOPE_DIM + ROPE_DIM
ROPE_BASE = 10000.0
HGRN_HEADS = 4
HGRN_DK = 128
HGRN_DV = 128
D_FF = 4096

LANE = 128
HEAD_SLOT = LANE

COL_GATE = 0
COL_CONV = 3072
COL_CQ = 4096
COL_CKV = 4352
COL_KROPE = 4480
COL_HGRN = 4608
N_IN_PAD = 6656

VMEM_LIMIT = 52 * 1024 * 1024

TL = 384
TN_IN = 512
N_GATE_CHUNK = COL_CONV // TN_IN
SUBLANE = 8
RC_CONV = 16
CONV_HALO = 32
CONV_SPAN = TL + CONV_HALO - SUBLANE
RC_PREP = 128
TQ = 384
ATTN_HG = 4
LOG2E = 1.4426950408889634
HGRN_C = 128
HGRN_BLK = 4
TM_MERGE = 1024
TM_FF = 1024
TF_FF = 1024


def _rms(x, eps=EPS):
    return x * lax.rsqrt(jnp.mean(x * x, axis=-1, keepdims=True) + eps)


def _sigmoid(x):
    return 0.5 * jnp.tanh(0.5 * x) + 0.5


def _silu(x):
    return x * _sigmoid(x)


def _conv_stage(glu, j, hbuf, sbuf):
    @pl.when(j == 0)
    def _():
        hbuf[0:CONV_HALO, :] = jnp.zeros((CONV_HALO, CONV_DIM), F32)

    @pl.when(j > 0)
    def _():
        hbuf[0:CONV_HALO, :] = hbuf[TL:TL + CONV_HALO, :]

    hbuf[CONV_HALO:CONV_HALO + TL, :] = glu

    for r in range(1, SUBLANE):
        sbuf[r - 1] = hbuf[r:r + CONV_SPAN, :]


def _conv_rows(r0, w_ref, cb_ref, lg_ref, lb_ref, o_ref, hbuf, sbuf):
    base = CONV_HALO - (CONV_K - 1)
    acc = jnp.zeros((RC_CONV, CONV_DIM), F32) + cb_ref[...]
    for k in range(CONV_K):
        r = (base + k) % SUBLANE
        lo = pl.multiple_of(r0 + (base + k - r), SUBLANE)
        src = hbuf if r == 0 else sbuf.at[r - 1]
        wk = jnp.tile(w_ref[k * SUBLANE:(k + 1) * SUBLANE, :], (RC_CONV // SUBLANE, 1))
        acc = acc + wk * src[pl.ds(lo, RC_CONV), :]
    mu = jnp.mean(acc, axis=-1, keepdims=True)
    xc = acc - mu
    y = xc * lax.rsqrt(jnp.mean(xc * xc, axis=-1, keepdims=True) + EPS)
    y = y * lg_ref[...] + lb_ref[...]
    o_ref[0, pl.ds(r0, RC_CONV), :] = (y * jax.nn.sigmoid(y)).astype(o_ref.dtype)


def _rope(x, c, s1, s2):
    half = ROPE_DIM // 2
    return x * c + pltpu.roll(x, LANE - half, 1) * s1 + pltpu.roll(x, half, 1) * s2


def _lane_sumsq(x, ones):
    sq = x * x
    hi = sq.astype(BF16)
    lo = (sq - hi.astype(F32)).astype(BF16)
    return jnp.dot(jnp.concatenate([hi, lo], axis=1), ones, preferred_element_type=F32)


def _mla_rows(cq, ckv, kr, rows, qag_ref, kvag_ref, wuq_ref, wuqs_ref, wuk_ref, wuv_ref,
              qng_ref, qngs_ref, kng_ref, rc_ref, rs1_ref, rs2_ref, vone_ref, q_out, k_out, v_out):
    inv_d = 1.0 / QK_DIM
    ones = jnp.ones((2 * LANE, LANE), BF16)
    cq = (_rms(cq) * qag_ref[...]).astype(BF16)
    q = jnp.dot(cq, wuq_ref[...], preferred_element_type=F32)
    qs = jnp.dot(cq, wuqs_ref[...], preferred_element_type=F32)
    ckv = (_rms(ckv) * kvag_ref[...]).astype(BF16)
    kn = jnp.dot(ckv, wuk_ref[...], preferred_element_type=F32)
    v = jnp.dot(ckv, wuv_ref[...], preferred_element_type=F32) + vone_ref[...]
    rc, rs1, rs2 = rc_ref[rows, :], rs1_ref[rows, :], rs2_ref[rows, :]
    q_c = qng_ref[...] * rc
    q_s = qngs_ref[...] * (rs1 + rs2)
    kr_ss = _lane_sumsq(kr, ones)
    kr_rot = _rope(kr * kng_ref[...], rc, rs1, rs2)
    for h in range(MLA_HEADS):
        sl = slice(h * HEAD_SLOT, (h + 1) * HEAD_SLOT)
        qh = q[:, sl]
        q_out[0, h, rows, :] = ((qh * q_c + qs[:, sl] * q_s) * lax.rsqrt(
            _lane_sumsq(qh, ones) * inv_d + EPS)).astype(q_out.dtype)
        kh = kn[:, sl]
        k_out[0, h, rows, :] = ((kh * kng_ref[...] + kr_rot) * lax.rsqrt(
            (_lane_sumsq(kh, ones) + kr_ss) * inv_d + EPS)).astype(k_out.dtype)
        v_out[0, h, rows, :] = v[:, sl].astype(v_out.dtype)


def _hgrn_chunk(q, fr, iv, gv, rows, oml_ref, ng_ref, lvl, st_ref, o_ref):
    C = HGRN_C
    rowid = lax.broadcasted_iota(jnp.int32, (C, HGRN_DK), 0)
    nt = (((1,), (1,)), ((), ()))
    tn = (((0,), (0,)), ((), ()))
    for h in range(HGRN_HEADS):
        sl = slice(h * HGRN_DK, (h + 1) * HGRN_DK)
        qh = q[:, sl]
        kk = oml_ref[:, sl] * jax.nn.sigmoid(-fr[:, sl])
        lf = jnp.log1p(-jnp.minimum(kk, GATE_CLAMP)) * LOG2E
        v = _silu(iv[:, sl]).astype(BF16)

        fwd = lf
        bwd = jnp.zeros_like(lf)
        n, level = 1, 0
        a = None
        while n < C:
            if n >= HGRN_BLK:
                qs = (qh * jnp.exp2(fwd)).astype(BF16)
                if a is None:
                    a = jnp.where(lvl == 0, lax.dot_general(
                        qs, (kk * jnp.exp2(-fwd)).astype(BF16), nt,
                        preferred_element_type=F32), 0.0)
                level += 1
                ks = (kk * jnp.exp2(bwd)).astype(BF16)
                a = jnp.where(lvl == level,
                              lax.dot_general(qs, ks, nt, preferred_element_type=F32), a)
            tot = fwd + bwd
            upper = (rowid & n) != 0
            fwd = fwd + jnp.where(upper, pltpu.roll(tot, n, 0), 0.0)
            bwd = bwd + jnp.where(upper, 0.0, pltpu.roll(tot, C - n, 0))
            n *= 2

        st = st_ref[h]
        o = jnp.dot(a.astype(BF16), v, preferred_element_type=F32)
        o = o + lax.dot_general((qh * jnp.exp2(fwd)).astype(BF16), st.astype(BF16), nt,
                                preferred_element_type=F32)
        ks = (kk * jnp.exp2(bwd)).astype(BF16)
        st_ref[h] = st * jnp.exp2(fwd[C - 1:C, :]) + lax.dot_general(
            v, ks, tn, preferred_element_type=F32)

        o = _rms(o) * ng_ref[:, sl]
        o_ref[0, rows, sl] = (o * _silu(gv[:, sl])).astype(o_ref.dtype)


def _hgrn_levels():
    t = np.arange(HGRN_C)[:, None]
    s = np.arange(HGRN_C)[None, :]
    x = (t ^ s) // HGRN_BLK
    lvl = np.where(x == 0, 0, np.floor(np.log2(np.maximum(x, 1))).astype(np.int64) + 1)
    return jnp.asarray(np.where(s > t, -1, lvl), dtype=jnp.int32)


def _front_kernel(x_ref, g1_ref, wbr_ref, wgate_ref,
                  cw_ref, cb_ref, lg_ref, lb_ref,
                  qag_ref, kvag_ref, wuq_ref, wuqs_ref, wuk_ref, wuv_ref, qng_ref, qngs_ref, kng_ref,
                  rc_ref, rs1_ref, rs2_ref, vone_ref,
                  oml_ref, ng_ref, lvl_ref,
                  gate_out, ha_out, q_out, k_out, v_out, oc_out,
                  h_scr, ym_scr, yh_scr, hbuf, sbuf, st_ref):
    j = pl.program_id(1)
    hw = HGRN_HEADS * HGRN_DK
    n_mla = COL_HGRN - COL_CQ

    @pl.when(j == 0)
    def _():
        st_ref[...] = jnp.zeros(st_ref.shape, F32)

    h = (_rms(x_ref[0]) * g1_ref[...]).astype(BF16)
    h_scr[...] = h
    yc = jnp.dot(h, wbr_ref[:, 0:2 * CONV_DIM], preferred_element_type=F32)
    _conv_stage(yc[:, :CONV_DIM] * _sigmoid(yc[:, CONV_DIM:]), j, hbuf, sbuf)
    ym_scr[...] = jnp.dot(h, wbr_ref[:, 2 * CONV_DIM:2 * CONV_DIM + n_mla],
                          preferred_element_type=F32)
    yh_scr[...] = jnp.dot(h, wbr_ref[:, 2 * CONV_DIM + n_mla:], preferred_element_type=F32)

    conv_per_gate = (TL // RC_CONV) // N_GATE_CHUNK

    def gate_and_conv(c, carry):
        gate_out[c, 0] = _sigmoid(jnp.dot(h_scr[...], wgate_ref[c],
                                          preferred_element_type=F32)).astype(gate_out.dtype)
        for t in range(conv_per_gate):
            r0 = pl.multiple_of((c * conv_per_gate + t) * RC_CONV, RC_CONV)
            _conv_rows(r0, cw_ref, cb_ref, lg_ref, lb_ref, ha_out, hbuf, sbuf)
        return carry

    for c in range(N_GATE_CHUNK):
        gate_and_conv(c, 0)

    def mla_chunk(c, carry):
        rows = pl.ds(pl.multiple_of(c * RC_PREP, RC_PREP), RC_PREP)
        _mla_rows(ym_scr[rows, 0:Q_RANK], ym_scr[rows, Q_RANK:Q_RANK + KV_RANK],
                  ym_scr[rows, Q_RANK + KV_RANK:], rows,
                  qag_ref, kvag_ref, wuq_ref, wuqs_ref, wuk_ref, wuv_ref,
                  qng_ref, qngs_ref, kng_ref, rc_ref, rs1_ref, rs2_ref, vone_ref, q_out, k_out, v_out)
        return carry

    for c in range(TL // RC_PREP):
        mla_chunk(c, 0)

    lvl = lvl_ref[...]

    def hgrn_chunk(c, carry):
        rows = pl.ds(pl.multiple_of(c * HGRN_C, HGRN_C), HGRN_C)
        _hgrn_chunk(yh_scr[rows, 0:hw], yh_scr[rows, hw:2 * hw], yh_scr[rows, 2 * hw:3 * hw],
                    yh_scr[rows, 3 * hw:], rows, oml_ref, ng_ref, lvl, st_ref, oc_out)
        return carry

    lax.fori_loop(0, TL // HGRN_C, hgrn_chunk, 0)


def _front(x3, g1, w_br, w_gate, conv_w, conv_b, ln_g, ln_b, qag, kvag, wuq, wuqs, wuk, wuv,
           qng, qngs, kng, rc, rs1, rs2, vone, one_minus_lb, hgrn_ng):
    b, lp, d = x3.shape
    hw = MLA_HEADS * HEAD_SLOT
    gw = HGRN_HEADS * HGRN_DK
    n_br = N_IN_PAD - COL_CONV
    const = lambda shape: pl.BlockSpec(shape, lambda bb, j: (0,) * len(shape))
    resident = lambda shape: pl.BlockSpec(shape, lambda bb, j: (0,) * len(shape),
                                          pipeline_mode=pl.Buffered(1))
    tab = pl.BlockSpec((TL, LANE), lambda bb, j: (j, 0))
    seq = lambda w: pl.BlockSpec((1, TL, w), lambda bb, j: (bb, j, 0))
    heads = pl.BlockSpec((1, MLA_HEADS, TL, HEAD_SLOT), lambda bb, j: (bb, 0, j, 0))
    qkv = jax.ShapeDtypeStruct((b, MLA_HEADS, lp, HEAD_SLOT), BF16)
    return pl.pallas_call(
        _front_kernel,
        out_shape=(jax.ShapeDtypeStruct((N_GATE_CHUNK, b, lp, TN_IN), BF16),
                   jax.ShapeDtypeStruct((b, lp, CONV_DIM), BF16),
                   qkv, qkv, qkv,
                   jax.ShapeDtypeStruct((b, lp, gw), BF16)),
        grid=(b, lp // TL),
        in_specs=[
            seq(d), const((1, d)), resident((d, n_br)), resident((N_GATE_CHUNK, d, TN_IN)),
            const((CONV_K * SUBLANE, CONV_DIM)), const((1, CONV_DIM)), const((1, CONV_DIM)),
            const((1, CONV_DIM)),
            const((1, Q_RANK)), const((1, KV_RANK)),
            const((Q_RANK, hw)), const((Q_RANK, hw)), const((KV_RANK, hw)), const((KV_RANK, hw)),
            const((1, LANE)), const((1, LANE)), const((1, LANE)),
            tab, tab, tab, const((1, hw)),
            const((1, gw)), const((1, gw)), const((HGRN_C, HGRN_C)),
        ],
        out_specs=(pl.BlockSpec((N_GATE_CHUNK, 1, TL, TN_IN), lambda bb, j: (0, bb, j, 0)),
                   seq(CONV_DIM), heads, heads, heads, seq(gw)),
        scratch_shapes=[pltpu.VMEM((TL, d), BF16),
                        pltpu.VMEM((TL, COL_HGRN - COL_CQ), F32),
                        pltpu.VMEM((TL, 4 * gw), F32),
                        pltpu.VMEM((CONV_HALO + TL, CONV_DIM), F32),
                        pltpu.VMEM((SUBLANE - 1, CONV_SPAN, CONV_DIM), F32),
                        pltpu.VMEM((HGRN_HEADS, HGRN_DV, HGRN_DK), F32)],
        compiler_params=pltpu.CompilerParams(
            dimension_semantics=("parallel", "arbitrary"), vmem_limit_bytes=VMEM_LIMIT),
    )(x3, g1, w_br, w_gate, conv_w, conv_b, ln_g, ln_b, qag, kvag, wuq, wuqs, wuk, wuv,
      qng, qngs, kng, rc, rs1, rs2, vone, one_minus_lb, hgrn_ng, _hgrn_levels())


def _attn_kernel(q_ref, k_ref, v_ref, o_ref, s_ref, m_ref, acc_ref):
    i = pl.program_id(2)
    nt = (((1,), (1,)), ((), ()))
    heads = range(ATTN_HG)
    qs = [q_ref[0, h] for h in heads]
    odd = (i & 1) == 1

    def scores(h, tile0, ntiles):
        kt = k_ref[0, h, pl.ds(pl.multiple_of(tile0 * TQ, TQ), ntiles * TQ), :]
        return lax.dot_general(qs[h], kt, nt, preferred_element_type=F32)

    def fold(s):
        m = s[:, 0:LANE]
        for c in range(1, s.shape[1] // LANE):
            m = jnp.maximum(m, s[:, c * LANE:(c + 1) * LANE])
        return m

    def score_span(tile0, ntiles, m_in):
        out = []
        for h in heads:
            s = scores(h, tile0, ntiles)
            for t in range(ntiles):
                s_ref[h, tile0 + t] = s[:, t * TQ:(t + 1) * TQ]
            out.append(jnp.maximum(m_in[h], fold(s)))
        return tuple(out)

    def pv_span(tile0, ntiles):
        out = []
        for h in heads:
            s = jnp.concatenate([s_ref[h, tile0 + t] for t in range(ntiles)], axis=1)
            p = jnp.exp2(s - ms[h]).astype(BF16)
            vt = v_ref[0, h, pl.ds(pl.multiple_of(tile0 * TQ, TQ), ntiles * TQ), :]
            out.append(jnp.dot(p, vt, preferred_element_type=F32))
        return out

    quads = lax.shift_right_logical(i, 2)
    mloc = lax.fori_loop(0, quads, lambda t, m: score_span(4 * t, 4, m),
                         tuple(jnp.full((TQ, LANE), -1e30, F32) for _ in heads))
    for h in heads:
        m_ref[h] = mloc[h]

    @pl.when((i & 2) != 0)
    def _():
        m_new = score_span(4 * quads, 2, [m_ref[h] for h in heads])
        for h in heads:
            m_ref[h] = m_new[h]

    keep = (lax.broadcasted_iota(jnp.int32, (TQ, TQ), 0)
            >= lax.broadcasted_iota(jnp.int32, (TQ, TQ), 1))

    @pl.when(odd)
    def _():
        for h in heads:
            s = scores(h, i - 1, 2)
            lo, hi = s[:, :TQ], jnp.where(keep, s[:, TQ:], -1e30)
            s_ref[h, i - 1] = lo
            s_ref[h, i] = hi
            m_ref[h] = jnp.maximum(m_ref[h], jnp.maximum(fold(lo), fold(hi)))

    @pl.when(jnp.logical_not(odd))
    def _():
        for h in heads:
            s = jnp.where(keep, scores(h, i, 1), -1e30)
            s_ref[h, i] = s
            m_ref[h] = jnp.maximum(m_ref[h], fold(s))

    ms = [jnp.max(m_ref[h], axis=-1, keepdims=True) for h in heads]

    quads2 = lax.shift_right_logical(i + 1, 2)

    def pass2(t, acc):
        part = pv_span(4 * t, 4)
        return tuple(acc[h] + part[h] for h in heads)

    acc = lax.fori_loop(0, quads2, pass2, tuple(jnp.zeros((TQ, HEAD_SLOT), F32) for _ in heads))
    for h in heads:
        acc_ref[h] = acc[h]

    @pl.when(((i + 1) & 2) != 0)
    def _():
        part = pv_span(4 * quads2, 2)
        for h in heads:
            acc_ref[h] += part[h]

    @pl.when(jnp.logical_not(odd))
    def _():
        part = pv_span(i, 1)
        for h in heads:
            acc_ref[h] += part[h]

    first = lax.broadcasted_iota(jnp.int32, (TQ, HEAD_SLOT), 1) < V_DIM
    for hp in range(ATTN_HG // 2):
        a, b = acc_ref[2 * hp], acc_ref[2 * hp + 1]
        den = pltpu.roll(jnp.where(first, b, a), V_DIM, 1)
        o_ref[0, :, hp * HEAD_SLOT:(hp + 1) * HEAD_SLOT] = (
            jnp.where(first, a, b) / den).astype(o_ref.dtype)


def _attention(q, k, v):
    b, nh, lp, hs = q.shape
    nq = lp // TQ
    return pl.pallas_call(
        _attn_kernel,
        out_shape=jax.ShapeDtypeStruct((b, lp, nh * V_DIM), BF16),
        grid=(b, nh // ATTN_HG, nq),
        in_specs=[
            pl.BlockSpec((1, ATTN_HG, TQ, hs), lambda bb, g, i: (bb, g, i, 0)),
            pl.BlockSpec((1, ATTN_HG, lp, hs), lambda bb, g, i: (bb, g, 0, 0)),
            pl.BlockSpec((1, ATTN_HG, lp, hs), lambda bb, g, i: (bb, g, 0, 0)),
        ],
        out_specs=pl.BlockSpec((1, TQ, ATTN_HG * V_DIM), lambda bb, g, i: (bb, i, g)),
        scratch_shapes=[pltpu.VMEM((ATTN_HG, nq, TQ, TQ), F32),
                        pltpu.VMEM((ATTN_HG, TQ, LANE), F32),
                        pltpu.VMEM((ATTN_HG, TQ, HEAD_SLOT), F32)],
        compiler_params=pltpu.CompilerParams(
            dimension_semantics=("parallel", "parallel", "arbitrary"), vmem_limit_bytes=VMEM_LIMIT),
    )(q, k, v)


def _merge_kernel(ha_ref, ob_ref, oc_ref, gate_ref, x_ref, wa_ref, wb_ref, wc_ref, wo_ref, o_ref):
    mix = None
    for n, (br_ref, w_ref) in enumerate(((ha_ref, wa_ref), (ob_ref, wb_ref), (oc_ref, wc_ref))):
        y = jnp.dot(br_ref[...], w_ref[...], preferred_element_type=F32)
        per = D_MODEL // TN_IN
        gate = jnp.concatenate([gate_ref[n * per + t] for t in range(per)], axis=1)
        y = y * gate.astype(F32)
        mix = y if mix is None else mix + y
    o_ref[...] = x_ref[...] + jnp.dot(mix.astype(BF16), wo_ref[...], preferred_element_type=F32)


def _merge(ha, ob, oc, gates, x2, wa, wb, wc, wo, li):
    t = x2.shape[0]
    br = lambda: pl.BlockSpec((TM_MERGE, 512), lambda i: (i, 0))
    wbr = lambda: pl.BlockSpec((None, 512, D_MODEL), lambda i: (li, 0, 0))
    return pl.pallas_call(
        _merge_kernel,
        out_shape=jax.ShapeDtypeStruct((t, D_MODEL), F32),
        grid=(t // TM_MERGE,),
        in_specs=[br(), br(), br(),
                  pl.BlockSpec((N_GATE_CHUNK, TM_MERGE, TN_IN), lambda i: (0, i, 0)),
                  pl.BlockSpec((TM_MERGE, D_MODEL), lambda i: (i, 0)),
                  wbr(), wbr(), wbr(),
                  pl.BlockSpec((None, D_MODEL, D_MODEL), lambda i: (li, 0, 0))],
        out_specs=pl.BlockSpec((TM_MERGE, D_MODEL), lambda i: (i, 0)),
        compiler_params=pltpu.CompilerParams(
            dimension_semantics=("parallel",), vmem_limit_bytes=VMEM_LIMIT),
    )(ha, ob, oc, gates, x2, wa, wb, wc, wo)


def _ffn_rows(x, g_ref, w1_ref, w2_ref):
    h = (_rms(x) * g_ref[...]).astype(BF16)
    acc = x
    for c in range(D_FF // TF_FF):
        cols = slice(c * TF_FF, (c + 1) * TF_FF)
        a = jnp.maximum(jnp.dot(h, w1_ref[:, cols], preferred_element_type=F32), 0.0)
        acc = acc + jnp.dot((a * a).astype(BF16), w2_ref[cols, :], preferred_element_type=F32)
    return acc


def _ffn_kernel(x_ref, g_ref, w1_ref, w2_ref, o_ref):
    o_ref[...] = _ffn_rows(x_ref[...], g_ref, w1_ref, w2_ref)


def _ffn_weight_specs(li):
    return [
        pl.BlockSpec((None, D_MODEL, D_FF), lambda *_: (li, 0, 0), pipeline_mode=pl.Buffered(1)),
        pl.BlockSpec((None, D_FF, D_MODEL), lambda *_: (li, 0, 0), pipeline_mode=pl.Buffered(1)),
    ]


def _ffn(x2, g, w1, w2, li):
    t = x2.shape[0]
    return pl.pallas_call(
        _ffn_kernel,
        out_shape=jax.ShapeDtypeStruct((t, D_MODEL), F32),
        grid=(t // TM_FF,),
        in_specs=[
            pl.BlockSpec((TM_FF, D_MODEL), lambda i: (i, 0)),
            pl.BlockSpec((1, D_MODEL), lambda i: (0, 0)),
        ] + _ffn_weight_specs(li),
        out_specs=pl.BlockSpec((TM_FF, D_MODEL), lambda i: (i, 0)),
        compiler_params=pltpu.CompilerParams(
            dimension_semantics=("parallel",), vmem_limit_bytes=VMEM_LIMIT),
    )(x2, g, w1, w2)


def _ffn_final(x2, g, w1, w2, li, b, lp, n_lead, n_out):
    assert lp % SUBLANE == 0 and n_lead % SUBLANE == 0
    return pl.pallas_call(
        _ffn_kernel,
        out_shape=jax.ShapeDtypeStruct((b, n_out, D_MODEL), F32),
        grid=(b, n_out // TM_FF),
        in_specs=[
            pl.BlockSpec((pl.Element(TM_FF), pl.Element(D_MODEL)),
                         lambda bb, t: (pl.multiple_of(bb * lp + n_lead + t * TM_FF, SUBLANE), 0)),
            pl.BlockSpec((1, D_MODEL), lambda bb, t: (0, 0)),
        ] + _ffn_weight_specs(li),
        out_specs=pl.BlockSpec((None, TM_FF, D_MODEL), lambda bb, t: (bb, t, 0)),
        compiler_params=pltpu.CompilerParams(
            dimension_semantics=("parallel", "parallel"), vmem_limit_bytes=VMEM_LIMIT),
    )(x2, g, w1, w2)


def _regroup_w_in(w):
    o = 0
    conv = w[:, o:o + 2 * CONV_DIM]; o += 2 * CONV_DIM
    cq = w[:, o:o + Q_RANK]; o += Q_RANK
    ckv = w[:, o:o + KV_RANK]; o += KV_RANK
    kr = w[:, o:o + ROPE_DIM]; o += ROPE_DIM
    hg = w[:, o:o + 4 * HGRN_HEADS * HGRN_DK]; o += 4 * HGRN_HEADS * HGRN_DK
    gate = w[:, o:]
    kr_slot = jnp.pad(kr, ((0, 0), (NOPE_DIM, LANE - QK_DIM)))
    return jnp.concatenate([gate, conv, cq, ckv, kr_slot, hg], axis=1).astype(BF16)


def _head_slots(w, per_head, start, width):
    r = w.shape[0]
    wh = w.reshape(r, MLA_HEADS, per_head)[:, :, start:start + width]
    wh = jnp.pad(wh, ((0, 0), (0, 0), (0, HEAD_SLOT - width)))
    return wh.reshape(r, MLA_HEADS * HEAD_SLOT)


def _swap_rope_halves(a):
    half = ROPE_DIM // 2
    return jnp.concatenate([jnp.zeros_like(a[..., :NOPE_DIM]), a[..., NOPE_DIM + half:],
                            a[..., NOPE_DIM:NOPE_DIM + half]], axis=-1)


def _v_slots(w_ukv):
    r = w_ukv.shape[0]
    wv = w_ukv.reshape(r, MLA_HEADS // 2, 2, NOPE_DIM + V_DIM)[..., NOPE_DIM:]
    z = jnp.zeros_like(wv[:, :, 0])
    slots = jnp.stack([jnp.concatenate([wv[:, :, 0], z], axis=-1),
                       jnp.concatenate([z, wv[:, :, 1]], axis=-1)], axis=2)
    ones = jnp.ones((MLA_HEADS // 2, V_DIM), F32)
    zo = jnp.zeros_like(ones)
    vone = jnp.stack([jnp.concatenate([zo, ones], axis=-1),
                      jnp.concatenate([ones, zo], axis=-1)], axis=1)
    return slots.reshape(r, MLA_HEADS * HEAD_SLOT), vone.reshape(1, MLA_HEADS * HEAD_SLOT)


def _rope_tables(lp):
    half = ROPE_DIM // 2
    pos = jnp.arange(lp, dtype=F32)
    inv_freq = ROPE_BASE ** (-jnp.arange(half, dtype=F32) / half)
    ang = pos[:, None] * inv_freq[None, :]
    cos, sin = jnp.cos(ang), jnp.sin(ang)
    ones = jnp.ones((lp, NOPE_DIM), F32)
    z16 = jnp.zeros((lp, half), F32)
    z64 = jnp.zeros((lp, NOPE_DIM), F32)
    tail = jnp.zeros((lp, LANE - QK_DIM), F32)
    rc = jnp.concatenate([ones, cos, cos, tail], axis=1)
    rs1 = jnp.concatenate([z64, -sin, z16, tail], axis=1)
    rs2 = jnp.concatenate([z64, z16, sin, tail], axis=1)
    return rc, rs1, rs2


def kernel(x, meta, norm1_g, w_in, conv_w, conv_b, conv_ln_g, conv_ln_b, w_conv_out, q_a_norm_g, w_uq, kv_a_norm_g, w_ukv, q_norm_g, k_norm_g, w_attn_out, hgrn_lb_logits, hgrn_norm_g, w_hgrn_out, w_out, norm2_g, w_ff1, w_ff2):
    b, seq, d = x.shape
    depth = w_in.shape[0]
    l = seq + N_META
    lp = -(-l // TL) * TL
    xc = jnp.concatenate([jnp.broadcast_to(meta[None].astype(x.dtype), (b, N_META, d)), x,
                          jnp.zeros((b, lp - l, d), x.dtype)], axis=1)
    x2 = xc.reshape(b * lp, d)

    rc, rs1, rs2 = _rope_tables(lp)
    p_lb = jax.nn.softmax(hgrn_lb_logits.astype(F32), axis=0)
    lower_bounds = jnp.cumsum(p_lb, axis=0) - p_lb[0:1]
    row = lambda a: a.astype(F32).reshape(1, -1)
    pad_qk = lambda g: jnp.pad(g.astype(F32), (0, LANE - QK_DIM)).reshape(1, LANE)
    w_a, w_b, w_c, w_o = (w.astype(BF16) for w in (w_conv_out, w_attn_out, w_hgrn_out, w_out))
    w_1, w_2 = w_ff1.astype(BF16), w_ff2.astype(BF16)

    for li in range(depth):
        wuv, vone = _v_slots(w_ukv[li])
        w_uq_heads = w_uq[li].reshape(Q_RANK, MLA_HEADS, QK_DIM)
        w_uq_swapped = _swap_rope_halves(w_uq_heads).reshape(Q_RANK, MLA_HEADS * QK_DIM)
        q_gain = q_norm_g[li].astype(F32) * (QK_DIM ** -0.5 * LOG2E)
        w_all = _regroup_w_in(w_in[li])
        w_gate = w_all[:, :COL_CONV].reshape(d, N_GATE_CHUNK, TN_IN).transpose(1, 0, 2)
        gates, ha, q, k, v, oc = _front(
            x2.reshape(b, lp, d), row(norm1_g[li]), w_all[:, COL_CONV:], w_gate,
            jnp.repeat(conv_w[li].astype(F32), SUBLANE, axis=0),
            row(conv_b[li]), row(conv_ln_g[li]), row(conv_ln_b[li]),
            row(q_a_norm_g[li]), row(kv_a_norm_g[li]),
            _head_slots(w_uq[li], QK_DIM, 0, QK_DIM).astype(BF16),
            _head_slots(w_uq_swapped, QK_DIM, 0, QK_DIM).astype(BF16),
            _head_slots(w_ukv[li], NOPE_DIM + V_DIM, 0, NOPE_DIM).astype(BF16),
            wuv.astype(BF16),
            pad_qk(q_gain), pad_qk(_swap_rope_halves(q_gain)), pad_qk(k_norm_g[li]),
            rc, rs1, rs2, vone, 1.0 - row(lower_bounds[li]), row(hgrn_norm_g[li]))
        ob = _attention(q, k, v)
        flat = lambda a: a.reshape(b * lp, -1)
        x2 = _merge(flat(ha), flat(ob), flat(oc), gates.reshape(N_GATE_CHUNK, b * lp, TN_IN), x2,
                    w_a, w_b, w_c, w_o, li)
        if li + 1 < depth:
            x2 = _ffn(x2, row(norm2_g[li]), w_1, w_2, li)
    assert seq % TM_FF == 0
    return _ffn_final(x2, row(norm2_g[depth - 1]), w_1, w_2, depth - 1, b, lp, N_META, seq)
```

```python
import functools

import jax
import jax.numpy as jnp
import numpy as np
from jax import lax
from jax.experimental import pallas as pl
from jax.experimental.pallas import tpu as pltpu

F32 = jnp.float32
BF16 = jnp.bfloat16

D_MODEL = 1024
N_META = 16
EPS = 1e-6
GATE_CLAMP = 1.0 - 1e-6
CONV_DIM = 512
CONV_K = 31
MLA_HEADS = 8
Q_RANK = 256
KV_RANK = 128
NOPE_DIM = 64
ROPE_DIM = 32
V_DIM = 64
QK_DIM = NOPE_DIM + ROPE_DIM
ROPE_BASE = 10000.0
HGRN_HEADS = 4
HGRN_DK = 128
HGRN_DV = 128
D_FF = 4096

LANE = 128
HEAD_SLOT = LANE

COL_GATE = 0
COL_CONV = 3072
COL_CQ = 4096
COL_CKV = 4352
COL_KROPE = 4480
COL_HGRN = 4608
N_IN_PAD = 6656

VMEM_LIMIT = 52 * 1024 * 1024

TL = 384
TN_IN = 512
N_GATE_CHUNK = COL_CONV // TN_IN
SUBLANE = 8
RC_CONV = 16
CONV_HALO = 32
CONV_SPAN = TL + CONV_HALO - SUBLANE
RC_PREP = 128
TQ = 384
ATTN_HG = 4
LOG2E = 1.4426950408889634
HGRN_C = 128
HGRN_BLK = 4
TM_MERGE = 1024
TM_FF = 1024
TF_FF = 1024


def _rms(x, eps=EPS):
    return x * lax.rsqrt(jnp.mean(x * x, axis=-1, keepdims=True) + eps)


def _sigmoid(x):
    return 0.5 * jnp.tanh(0.5 * x) + 0.5


def _silu(x):
    return x * _sigmoid(x)


def _conv_stage(glu, j, hbuf, sbuf):
    @pl.when(j == 0)
    def _():
        hbuf[0:CONV_HALO, :] = jnp.zeros((CONV_HALO, CONV_DIM), F32)

    @pl.when(j > 0)
    def _():
        hbuf[0:CONV_HALO, :] = hbuf[TL:TL + CONV_HALO, :]

    hbuf[CONV_HALO:CONV_HALO + TL, :] = glu

    for r in range(1, SUBLANE):
        sbuf[r - 1] = hbuf[r:r + CONV_SPAN, :]


def _conv_rows(r0, w_ref, cb_ref, lg_ref, lb_ref, o_ref, hbuf, sbuf):
    base = CONV_HALO - (CONV_K - 1)
    acc = jnp.zeros((RC_CONV, CONV_DIM), F32) + cb_ref[...]
    for k in range(CONV_K):
        r = (base + k) % SUBLANE
        lo = pl.multiple_of(r0 + (base + k - r), SUBLANE)
        src = hbuf if r == 0 else sbuf.at[r - 1]
        wk = jnp.tile(w_ref[k * SUBLANE:(k + 1) * SUBLANE, :], (RC_CONV // SUBLANE, 1))
        acc = acc + wk * src[pl.ds(lo, RC_CONV), :]
    mu = jnp.mean(acc, axis=-1, keepdims=True)
    xc = acc - mu
    y = xc * lax.rsqrt(jnp.mean(xc * xc, axis=-1, keepdims=True) + EPS)
    y = y * lg_ref[...] + lb_ref[...]
    o_ref[0, pl.ds(r0, RC_CONV), :] = (y * jax.nn.sigmoid(y)).astype(o_ref.dtype)


def _rope(x, c, s1, s2):
    half = ROPE_DIM // 2
    return x * c + pltpu.roll(x, LANE - half, 1) * s1 + pltpu.roll(x, half, 1) * s2


def _lane_sumsq(x, ones):
    sq = x * x
    hi = sq.astype(BF16)
    lo = (sq - hi.astype(F32)).astype(BF16)
    return jnp.dot(jnp.concatenate([hi, lo], axis=1), ones, preferred_element_type=F32)


def _mla_rows(cq, ckv, kr, rows, qag_ref, kvag_ref, wuq_ref, wuqs_ref, wuk_ref, wuv_ref,
              qng_ref, qngs_ref, kng_ref, rc_ref, rs1_ref, rs2_ref, vone_ref, q_out, k_out, v_out):
    inv_d = 1.0 / QK_DIM
    ones = jnp.ones((2 * LANE, LANE), BF16)
    cq = (_rms(cq) * qag_ref[...]).astype(BF16)
    q = jnp.dot(cq, wuq_ref[...], preferred_element_type=F32)
    qs = jnp.dot(cq, wuqs_ref[...], preferred_element_type=F32)
    ckv = (_rms(ckv) * kvag_ref[...]).astype(BF16)
    kn = jnp.dot(ckv, wuk_ref[...], preferred_element_type=F32)
    v = jnp.dot(ckv, wuv_ref[...], preferred_element_type=F32) + vone_ref[...]
    rc, rs1, rs2 = rc_ref[rows, :], rs1_ref[rows, :], rs2_ref[rows, :]
    q_c = qng_ref[...] * rc
    q_s = qngs_ref[...] * (rs1 + rs2)
    kr_ss = _lane_sumsq(kr, ones)
    kr_rot = _rope(kr * kng_ref[...], rc, rs1, rs2)
    for h in range(MLA_HEADS):
        sl = slice(h * HEAD_SLOT, (h + 1) * HEAD_SLOT)
        qh = q[:, sl]
        q_out[0, h, rows, :] = ((qh * q_c + qs[:, sl] * q_s) * lax.rsqrt(
            _lane_sumsq(qh, ones) * inv_d + EPS)).astype(q_out.dtype)
        kh = kn[:, sl]
        k_out[0, h, rows, :] = ((kh * kng_ref[...] + kr_rot) * lax.rsqrt(
            (_lane_sumsq(kh, ones) + kr_ss) * inv_d + EPS)).astype(k_out.dtype)
        v_out[0, h, rows, :] = v[:, sl].astype(v_out.dtype)


def _hgrn_head(qh, fr, iv, gv, oml, ng, lvl, st_ref, h):
    C = HGRN_C
    rowid = lax.broadcasted_iota(jnp.int32, (C, HGRN_DK), 0)
    nt = (((1,), (1,)), ((), ()))
    tn = (((0,), (0,)), ((), ()))
    kk = oml * jax.nn.sigmoid(-fr)
    lf = jnp.log1p(-jnp.minimum(kk, GATE_CLAMP)) * LOG2E
    v = _silu(iv).astype(BF16)

    fwd = lf
    bwd = jnp.zeros_like(lf)
    n, level = 1, 0
    a = None
    while n < C:
        if n >= HGRN_BLK:
            qs = (qh * jnp.exp2(fwd)).astype(BF16)
            if a is None:
                a = jnp.where(lvl == 0, lax.dot_general(
                    qs, (kk * jnp.exp2(-fwd)).astype(BF16), nt,
                    preferred_element_type=F32), 0.0)
            level += 1
            ks = (kk * jnp.exp2(bwd)).astype(BF16)
            a = jnp.where(lvl == level,
                          lax.dot_general(qs, ks, nt, preferred_element_type=F32), a)
        tot = fwd + bwd
        upper = (rowid & n) != 0
        fwd = fwd + jnp.where(upper, pltpu.roll(tot, n, 0), 0.0)
        bwd = bwd + jnp.where(upper, 0.0, pltpu.roll(tot, C - n, 0))
        n *= 2

    st = st_ref[h]
    o = jnp.dot(a.astype(BF16), v, preferred_element_type=F32)
    o = o + lax.dot_general((qh * jnp.exp2(fwd)).astype(BF16), st.astype(BF16), nt,
                            preferred_element_type=F32)
    ks = (kk * jnp.exp2(bwd)).astype(BF16)
    st_ref[h] = st * jnp.exp2(fwd[C - 1:C, :]) + lax.dot_general(
        v, ks, tn, preferred_element_type=F32)

    return _rms(o) * ng * _silu(gv)


def _hgrn_levels():
    t = np.arange(HGRN_C)[:, None]
    s = np.arange(HGRN_C)[None, :]
    x = (t ^ s) // HGRN_BLK
    lvl = np.where(x == 0, 0, np.floor(np.log2(np.maximum(x, 1))).astype(np.int64) + 1)
    return jnp.asarray(np.where(s > t, -1, lvl), dtype=jnp.int32)


def _front_kernel(x_ref, g1_ref, wbr_ref, wgate_ref,
                  cw_ref, cb_ref, lg_ref, lb_ref,
                  qag_ref, kvag_ref, wuq_ref, wuqs_ref, wuk_ref, wuv_ref, qng_ref, qngs_ref, kng_ref,
                  rc_ref, rs1_ref, rs2_ref, vone_ref,
                  oml_ref, ng_ref, lvl_ref,
                  gate_out, ha_out, q_out, k_out, v_out, oc_out,
                  h_scr, ym_scr, yh_scr, hbuf, sbuf, st_ref):
    j = pl.program_id(1)
    hw = HGRN_HEADS * HGRN_DK
    n_mla = COL_HGRN - COL_CQ

    @pl.when(j == 0)
    def _():
        st_ref[...] = jnp.zeros(st_ref.shape, F32)

    h = (_rms(x_ref[0]) * g1_ref[...]).astype(BF16)
    h_scr[...] = h
    yc = jnp.dot(h, wbr_ref[:, 0:2 * CONV_DIM], preferred_element_type=F32)
    _conv_stage(yc[:, :CONV_DIM] * _sigmoid(yc[:, CONV_DIM:]), j, hbuf, sbuf)
    ym_scr[...] = jnp.dot(h, wbr_ref[:, 2 * CONV_DIM:2 * CONV_DIM + n_mla],
                          preferred_element_type=F32)
    yh = jnp.dot(h, wbr_ref[:, 2 * CONV_DIM + n_mla:], preferred_element_type=F32)
    for idx in range(4 * HGRN_HEADS):
        yh_scr[idx] = yh[:, idx * HGRN_DK:(idx + 1) * HGRN_DK]

    conv_per_gate = (TL // RC_CONV) // N_GATE_CHUNK

    def gate_and_conv(c, carry):
        gate_out[c, 0] = _sigmoid(jnp.dot(h_scr[...], wgate_ref[c],
                                          preferred_element_type=F32)).astype(gate_out.dtype)
        for t in range(conv_per_gate):
            r0 = pl.multiple_of((c * conv_per_gate + t) * RC_CONV, RC_CONV)
            _conv_rows(r0, cw_ref, cb_ref, lg_ref, lb_ref, ha_out, hbuf, sbuf)
        return carry

    for c in range(N_GATE_CHUNK):
        gate_and_conv(c, 0)

    def mla_chunk(c, carry):
        rows = pl.ds(pl.multiple_of(c * RC_PREP, RC_PREP), RC_PREP)
        _mla_rows(ym_scr[rows, 0:Q_RANK], ym_scr[rows, Q_RANK:Q_RANK + KV_RANK],
                  ym_scr[rows, Q_RANK + KV_RANK:], rows,
                  qag_ref, kvag_ref, wuq_ref, wuqs_ref, wuk_ref, wuv_ref,
                  qng_ref, qngs_ref, kng_ref, rc_ref, rs1_ref, rs2_ref, vone_ref, q_out, k_out, v_out)
        return carry

    for c in range(TL // RC_PREP):
        mla_chunk(c, 0)

    lvl = lvl_ref[...]

    def hgrn_tile(t, carry):
        for cc in range(TL // HGRN_C):
            rows = slice(cc * HGRN_C, (cc + 1) * HGRN_C)
            for hd in range(HGRN_HEADS):
                oc_out[0, rows, hd * HGRN_DV:(hd + 1) * HGRN_DV] = _hgrn_head(
                    yh_scr[hd, rows, :], yh_scr[HGRN_HEADS + hd, rows, :],
                    yh_scr[2 * HGRN_HEADS + hd, rows, :], yh_scr[3 * HGRN_HEADS + hd, rows, :],
                    oml_ref[hd], ng_ref[hd], lvl, st_ref, hd).astype(oc_out.dtype)
        return carry

    lax.fori_loop(0, jnp.minimum(j + 1, 1), hgrn_tile, 0)


def _front(x3, g1, w_br, w_gate, li, conv_w, conv_b, ln_g, ln_b, qag, kvag, wuq, wuqs, wuk, wuv,
           qng, qngs, kng, rc, rs1, rs2, vone, one_minus_lb, hgrn_ng):
    b, lp, d = x3.shape
    hw = MLA_HEADS * HEAD_SLOT
    gw = HGRN_HEADS * HGRN_DK
    n_br = N_IN_PAD - COL_CONV
    const = lambda shape: pl.BlockSpec(shape, lambda bb, j: (0,) * len(shape))
    resident = lambda shape: pl.BlockSpec((None,) + shape, lambda bb, j: (li,) + (0,) * len(shape),
                                          pipeline_mode=pl.Buffered(1))
    tab = pl.BlockSpec((TL, LANE), lambda bb, j: (j, 0))
    seq = lambda w: pl.BlockSpec((1, TL, w), lambda bb, j: (bb, j, 0))
    heads = pl.BlockSpec((1, MLA_HEADS, TL, HEAD_SLOT), lambda bb, j: (bb, 0, j, 0))
    qkv = jax.ShapeDtypeStruct((b, MLA_HEADS, lp, HEAD_SLOT), BF16)
    return pl.pallas_call(
        _front_kernel,
        out_shape=(jax.ShapeDtypeStruct((N_GATE_CHUNK, b, lp, TN_IN), BF16),
                   jax.ShapeDtypeStruct((b, lp, CONV_DIM), BF16),
                   qkv, qkv, qkv,
                   jax.ShapeDtypeStruct((b, lp, gw), BF16)),
        grid=(b, lp // TL),
        in_specs=[
            seq(d), const((1, d)), resident((d, n_br)), resident((N_GATE_CHUNK, d, TN_IN)),
            const((CONV_K * SUBLANE, CONV_DIM)), const((1, CONV_DIM)), const((1, CONV_DIM)),
            const((1, CONV_DIM)),
            const((1, Q_RANK)), const((1, KV_RANK)),
            const((Q_RANK, hw)), const((Q_RANK, hw)), const((KV_RANK, hw)), const((KV_RANK, hw)),
            const((1, LANE)), const((1, LANE)), const((1, LANE)),
            tab, tab, tab, const((1, hw)),
            const((HGRN_HEADS, 1, HGRN_DK)), const((HGRN_HEADS, 1, HGRN_DV)), const((HGRN_C, HGRN_C)),
        ],
        out_specs=(pl.BlockSpec((N_GATE_CHUNK, 1, TL, TN_IN), lambda bb, j: (0, bb, j, 0)),
                   seq(CONV_DIM), heads, heads, heads, seq(gw)),
        scratch_shapes=[pltpu.VMEM((TL, d), BF16),
                        pltpu.VMEM((TL, COL_HGRN - COL_CQ), F32),
                        pltpu.VMEM((4 * HGRN_HEADS, TL, HGRN_DK), F32),
                        pltpu.VMEM((CONV_HALO + TL, CONV_DIM), F32),
                        pltpu.VMEM((SUBLANE - 1, CONV_SPAN, CONV_DIM), F32),
                        pltpu.VMEM((HGRN_HEADS, HGRN_DV, HGRN_DK), F32)],
        compiler_params=pltpu.CompilerParams(
            dimension_semantics=("parallel", "arbitrary"), vmem_limit_bytes=VMEM_LIMIT),
    )(x3, g1, w_br, w_gate, conv_w, conv_b, ln_g, ln_b, qag, kvag, wuq, wuqs, wuk, wuv,
      qng, qngs, kng, rc, rs1, rs2, vone, one_minus_lb, hgrn_ng, _hgrn_levels())


def _attn_kernel(q_ref, k_ref, v_ref, o_ref, s_ref, m_ref, acc_ref):
    i = pl.program_id(2)
    nt = (((1,), (1,)), ((), ()))
    heads = range(ATTN_HG)
    qs = [q_ref[0, h] for h in heads]
    odd = (i & 1) == 1

    def scores(h, tile0, ntiles):
        kt = k_ref[0, h, pl.ds(pl.multiple_of(tile0 * TQ, TQ), ntiles * TQ), :]
        return lax.dot_general(qs[h], kt, nt, preferred_element_type=F32)

    def fold(s):
        m = s[:, 0:LANE]
        for c in range(1, s.shape[1] // LANE):
            m = jnp.maximum(m, s[:, c * LANE:(c + 1) * LANE])
        return m

    def score_span(tile0, ntiles, m_in):
        out = []
        for h in heads:
            s = scores(h, tile0, ntiles)
            for t in range(ntiles):
                s_ref[h, tile0 + t] = s[:, t * TQ:(t + 1) * TQ]
            out.append(jnp.maximum(m_in[h], fold(s)))
        return tuple(out)

    def pv_span(tile0, ntiles):
        out = []
        for h in heads:
            s = jnp.concatenate([s_ref[h, tile0 + t] for t in range(ntiles)], axis=1)
            p = jnp.exp2(s - ms[h]).astype(BF16)
            vt = v_ref[0, h, pl.ds(pl.multiple_of(tile0 * TQ, TQ), ntiles * TQ), :]
            out.append(jnp.dot(p, vt, preferred_element_type=F32))
        return out

    quads = lax.shift_right_logical(i, 2)
    mloc = lax.fori_loop(0, quads, lambda t, m: score_span(4 * t, 4, m),
                         tuple(jnp.full((TQ, LANE), -1e30, F32) for _ in heads))
    for h in heads:
        m_ref[h] = mloc[h]

    @pl.when((i & 2) != 0)
    def _():
        m_new = score_span(4 * quads, 2, [m_ref[h] for h in heads])
        for h in heads:
            m_ref[h] = m_new[h]

    keep = (lax.broadcasted_iota(jnp.int32, (TQ, TQ), 0)
            >= lax.broadcasted_iota(jnp.int32, (TQ, TQ), 1))

    @pl.when(odd)
    def _():
        for h in heads:
            s = scores(h, i - 1, 2)
            lo, hi = s[:, :TQ], jnp.where(keep, s[:, TQ:], -1e30)
            s_ref[h, i - 1] = lo
            s_ref[h, i] = hi
            m_ref[h] = jnp.maximum(m_ref[h], jnp.maximum(fold(lo), fold(hi)))

    @pl.when(jnp.logical_not(odd))
    def _():
        for h in heads:
            s = jnp.where(keep, scores(h, i, 1), -1e30)
            s_ref[h, i] = s
            m_ref[h] = jnp.maximum(m_ref[h], fold(s))

    ms = [jnp.max(m_ref[h], axis=-1, keepdims=True) for h in heads]

    quads2 = lax.shift_right_logical(i + 1, 2)

    def pass2(t, acc):
        part = pv_span(4 * t, 4)
        return tuple(acc[h] + part[h] for h in heads)

    acc = lax.fori_loop(0, quads2, pass2, tuple(jnp.zeros((TQ, HEAD_SLOT), F32) for _ in heads))
    for h in heads:
        acc_ref[h] = acc[h]

    @pl.when(((i + 1) & 2) != 0)
    def _():
        part = pv_span(4 * quads2, 2)
        for h in heads:
            acc_ref[h] += part[h]

    @pl.when(jnp.logical_not(odd))
    def _():
        part = pv_span(i, 1)
        for h in heads:
            acc_ref[h] += part[h]

    first = lax.broadcasted_iota(jnp.int32, (TQ, HEAD_SLOT), 1) < V_DIM
    for hp in range(ATTN_HG // 2):
        a, b = acc_ref[2 * hp], acc_ref[2 * hp + 1]
        den = pltpu.roll(jnp.where(first, b, a), V_DIM, 1)
        o_ref[0, :, hp * HEAD_SLOT:(hp + 1) * HEAD_SLOT] = (
            jnp.where(first, a, b) / den).astype(o_ref.dtype)


def _attention(q, k, v):
    b, nh, lp, hs = q.shape
    nq = lp // TQ
    return pl.pallas_call(
        _attn_kernel,
        out_shape=jax.ShapeDtypeStruct((b, lp, nh * V_DIM), BF16),
        grid=(b, nh // ATTN_HG, nq),
        in_specs=[
            pl.BlockSpec((1, ATTN_HG, TQ, hs), lambda bb, g, i: (bb, g, i, 0)),
            pl.BlockSpec((1, ATTN_HG, lp, hs), lambda bb, g, i: (bb, g, 0, 0)),
            pl.BlockSpec((1, ATTN_HG, lp, hs), lambda bb, g, i: (bb, g, 0, 0)),
        ],
        out_specs=pl.BlockSpec((1, TQ, ATTN_HG * V_DIM), lambda bb, g, i: (bb, i, g)),
        scratch_shapes=[pltpu.VMEM((ATTN_HG, nq, TQ, TQ), F32),
                        pltpu.VMEM((ATTN_HG, TQ, LANE), F32),
                        pltpu.VMEM((ATTN_HG, TQ, HEAD_SLOT), F32)],
        compiler_params=pltpu.CompilerParams(
            dimension_semantics=("parallel", "parallel", "arbitrary"), vmem_limit_bytes=VMEM_LIMIT),
    )(q, k, v)


def _merge_kernel(ha_ref, ob_ref, oc_ref, gate_ref, x_ref, wa_ref, wb_ref, wc_ref, wo_ref, o_ref):
    mix = None
    for n, (br_ref, w_ref) in enumerate(((ha_ref, wa_ref), (ob_ref, wb_ref), (oc_ref, wc_ref))):
        y = jnp.dot(br_ref[...], w_ref[...], preferred_element_type=F32)
        per = D_MODEL // TN_IN
        gate = jnp.concatenate([gate_ref[n * per + t] for t in range(per)], axis=1)
        y = y * gate.astype(F32)
        mix = y if mix is None else mix + y
    o_ref[...] = x_ref[...] + jnp.dot(mix.astype(BF16), wo_ref[...], preferred_element_type=F32)


def _merge(ha, ob, oc, gates, x2, wa, wb, wc, wo, li):
    t = x2.shape[0]
    br = lambda: pl.BlockSpec((TM_MERGE, 512), lambda i: (i, 0))
    wbr = lambda: pl.BlockSpec((None, 512, D_MODEL), lambda i: (li, 0, 0))
    return pl.pallas_call(
        _merge_kernel,
        out_shape=jax.ShapeDtypeStruct((t, D_MODEL), F32),
        grid=(t // TM_MERGE,),
        in_specs=[br(), br(), br(),
                  pl.BlockSpec((N_GATE_CHUNK, TM_MERGE, TN_IN), lambda i: (0, i, 0)),
                  pl.BlockSpec((TM_MERGE, D_MODEL), lambda i: (i, 0)),
                  wbr(), wbr(), wbr(),
                  pl.BlockSpec((None, D_MODEL, D_MODEL), lambda i: (li, 0, 0))],
        out_specs=pl.BlockSpec((TM_MERGE, D_MODEL), lambda i: (i, 0)),
        compiler_params=pltpu.CompilerParams(
            dimension_semantics=("parallel",), vmem_limit_bytes=VMEM_LIMIT),
    )(ha, ob, oc, gates, x2, wa, wb, wc, wo)


def _ffn_rows(x, g_ref, w1_ref, w2_ref):
    h = (_rms(x) * g_ref[...]).astype(BF16)
    acc = x
    for c in range(D_FF // TF_FF):
        cols = slice(c * TF_FF, (c + 1) * TF_FF)
        a = jnp.maximum(jnp.dot(h, w1_ref[:, cols], preferred_element_type=F32), 0.0)
        acc = acc + jnp.dot((a * a).astype(BF16), w2_ref[cols, :], preferred_element_type=F32)
    return acc


def _ffn_kernel(x_ref, g_ref, w1_ref, w2_ref, o_ref):
    o_ref[...] = _ffn_rows(x_ref[...], g_ref, w1_ref, w2_ref)


def _ffn_weight_specs(li):
    return [
        pl.BlockSpec((None, D_MODEL, D_FF), lambda *_: (li, 0, 0), pipeline_mode=pl.Buffered(1)),
        pl.BlockSpec((None, D_FF, D_MODEL), lambda *_: (li, 0, 0), pipeline_mode=pl.Buffered(1)),
    ]


def _ffn(x2, g, w1, w2, li):
    t = x2.shape[0]
    return pl.pallas_call(
        _ffn_kernel,
        out_shape=jax.ShapeDtypeStruct((t, D_MODEL), F32),
        grid=(t // TM_FF,),
        in_specs=[
            pl.BlockSpec((TM_FF, D_MODEL), lambda i: (i, 0)),
            pl.BlockSpec((1, D_MODEL), lambda i: (0, 0)),
        ] + _ffn_weight_specs(li),
        out_specs=pl.BlockSpec((TM_FF, D_MODEL), lambda i: (i, 0)),
        compiler_params=pltpu.CompilerParams(
            dimension_semantics=("parallel",), vmem_limit_bytes=VMEM_LIMIT),
    )(x2, g, w1, w2)


def _ffn_final(x2, g, w1, w2, li, b, lp, n_lead, n_out):
    assert lp % SUBLANE == 0 and n_lead % SUBLANE == 0
    return pl.pallas_call(
        _ffn_kernel,
        out_shape=jax.ShapeDtypeStruct((b, n_out, D_MODEL), F32),
        grid=(b, n_out // TM_FF),
        in_specs=[
            pl.BlockSpec((pl.Element(TM_FF), pl.Element(D_MODEL)),
                         lambda bb, t: (pl.multiple_of(bb * lp + n_lead + t * TM_FF, SUBLANE), 0)),
            pl.BlockSpec((1, D_MODEL), lambda bb, t: (0, 0)),
        ] + _ffn_weight_specs(li),
        out_specs=pl.BlockSpec((None, TM_FF, D_MODEL), lambda bb, t: (bb, t, 0)),
        compiler_params=pltpu.CompilerParams(
            dimension_semantics=("parallel", "parallel"), vmem_limit_bytes=VMEM_LIMIT),
    )(x2, g, w1, w2)


def _regroup_w_in(w):
    o = 0
    conv = w[..., o:o + 2 * CONV_DIM]; o += 2 * CONV_DIM
    cq = w[..., o:o + Q_RANK]; o += Q_RANK
    ckv = w[..., o:o + KV_RANK]; o += KV_RANK
    kr = w[..., o:o + ROPE_DIM]; o += ROPE_DIM
    hg = w[..., o:o + 4 * HGRN_HEADS * HGRN_DK]; o += 4 * HGRN_HEADS * HGRN_DK
    gate = w[..., o:]
    kr_slot = jnp.pad(kr, ((0, 0), (0, 0), (NOPE_DIM, LANE - QK_DIM)))
    w_br = jnp.concatenate([conv, cq, ckv, kr_slot, hg], axis=-1).astype(BF16)
    depth, d, _ = w.shape
    w_gate = gate.astype(BF16).reshape(depth, d, N_GATE_CHUNK, TN_IN).transpose(0, 2, 1, 3)
    return w_br, w_gate


def _head_slots(w, per_head, start, width):
    r = w.shape[0]
    wh = w.reshape(r, MLA_HEADS, per_head)[:, :, start:start + width]
    wh = jnp.pad(wh, ((0, 0), (0, 0), (0, HEAD_SLOT - width)))
    return wh.reshape(r, MLA_HEADS * HEAD_SLOT)


def _swap_rope_halves(a):
    half = ROPE_DIM // 2
    return jnp.concatenate([jnp.zeros_like(a[..., :NOPE_DIM]), a[..., NOPE_DIM + half:],
                            a[..., NOPE_DIM:NOPE_DIM + half]], axis=-1)


def _v_slots(w_ukv):
    r = w_ukv.shape[0]
    wv = w_ukv.reshape(r, MLA_HEADS // 2, 2, NOPE_DIM + V_DIM)[..., NOPE_DIM:]
    z = jnp.zeros_like(wv[:, :, 0])
    slots = jnp.stack([jnp.concatenate([wv[:, :, 0], z], axis=-1),
                       jnp.concatenate([z, wv[:, :, 1]], axis=-1)], axis=2)
    ones = jnp.ones((MLA_HEADS // 2, V_DIM), F32)
    zo = jnp.zeros_like(ones)
    vone = jnp.stack([jnp.concatenate([zo, ones], axis=-1),
                      jnp.concatenate([ones, zo], axis=-1)], axis=1)
    return slots.reshape(r, MLA_HEADS * HEAD_SLOT), vone.reshape(1, MLA_HEADS * HEAD_SLOT)


def _rope_tables(lp):
    half = ROPE_DIM // 2
    pos = jnp.arange(lp, dtype=F32)
    inv_freq = ROPE_BASE ** (-jnp.arange(half, dtype=F32) / half)
    ang = pos[:, None] * inv_freq[None, :]
    cos, sin = jnp.cos(ang), jnp.sin(ang)
    ones = jnp.ones((lp, NOPE_DIM), F32)
    z16 = jnp.zeros((lp, half), F32)
    z64 = jnp.zeros((lp, NOPE_DIM), F32)
    tail = jnp.zeros((lp, LANE - QK_DIM), F32)
    rc = jnp.concatenate([ones, cos, cos, tail], axis=1)
    rs1 = jnp.concatenate([z64, -sin, z16, tail], axis=1)
    rs2 = jnp.concatenate([z64, z16, sin, tail], axis=1)
    return rc, rs1, rs2


def kernel(x, meta, norm1_g, w_in, conv_w, conv_b, conv_ln_g, conv_ln_b, w_conv_out, q_a_norm_g, w_uq, kv_a_norm_g, w_ukv, q_norm_g, k_norm_g, w_attn_out, hgrn_lb_logits, hgrn_norm_g, w_hgrn_out, w_out, norm2_g, w_ff1, w_ff2):
    b, seq, d = x.shape
    depth = w_in.shape[0]
    l = seq + N_META
    lp = -(-l // TL) * TL
    xc = jnp.concatenate([jnp.broadcast_to(meta[None].astype(x.dtype), (b, N_META, d)), x,
                          jnp.zeros((b, lp - l, d), x.dtype)], axis=1)
    x2 = xc.reshape(b * lp, d)

    rc, rs1, rs2 = _rope_tables(lp)
    p_lb = jax.nn.softmax(hgrn_lb_logits.astype(F32), axis=0)
    lower_bounds = jnp.cumsum(p_lb, axis=0) - p_lb[0:1]
    row = lambda a: a.astype(F32).reshape(1, -1)
    pad_qk = lambda g: jnp.pad(g.astype(F32), (0, LANE - QK_DIM)).reshape(1, LANE)
    w_a, w_b, w_c, w_o = (w.astype(BF16) for w in (w_conv_out, w_attn_out, w_hgrn_out, w_out))
    w_1, w_2 = w_ff1.astype(BF16), w_ff2.astype(BF16)
    w_br, w_gate = _regroup_w_in(w_in)

    for li in range(depth):
        wuv, vone = _v_slots(w_ukv[li])
        w_uq_heads = w_uq[li].reshape(Q_RANK, MLA_HEADS, QK_DIM)
        w_uq_swapped = _swap_rope_halves(w_uq_heads).reshape(Q_RANK, MLA_HEADS * QK_DIM)
        q_gain = q_norm_g[li].astype(F32) * (QK_DIM ** -0.5 * LOG2E)
        gates, ha, q, k, v, oc = _front(
            x2.reshape(b, lp, d), row(norm1_g[li]), w_br, w_gate, li,
            jnp.repeat(conv_w[li].astype(F32), SUBLANE, axis=0),
            row(conv_b[li]), row(conv_ln_g[li]), row(conv_ln_b[li]),
            row(q_a_norm_g[li]), row(kv_a_norm_g[li]),
            _head_slots(w_uq[li], QK_DIM, 0, QK_DIM).astype(BF16),
            _head_slots(w_uq_swapped, QK_DIM, 0, QK_DIM).astype(BF16),
            _head_slots(w_ukv[li], NOPE_DIM + V_DIM, 0, NOPE_DIM).astype(BF16),
            wuv.astype(BF16),
            pad_qk(q_gain), pad_qk(_swap_rope_halves(q_gain)), pad_qk(k_norm_g[li]),
            rc, rs1, rs2, vone,
            (1.0 - lower_bounds[li]).reshape(HGRN_HEADS, 1, HGRN_DK),
            hgrn_norm_g[li].astype(F32).reshape(HGRN_HEADS, 1, HGRN_DV))
        ob = _attention(q, k, v)
        flat = lambda a: a.reshape(b * lp, -1)
        x2 = _merge(flat(ha), flat(ob), flat(oc), gates.reshape(N_GATE_CHUNK, b * lp, TN_IN), x2,
                    w_a, w_b, w_c, w_o, li)
        if li + 1 < depth:
            x2 = _ffn(x2, row(norm2_g[li]), w_1, w_2, li)
    assert seq % TM_FF == 0
    return _ffn_final(x2, row(norm2_g[depth - 1]), w_1, w_2, depth - 1, b, lp, N_META, seq)
```

```python
import functools

import jax
import jax.numpy as jnp
import numpy as np
from jax import lax
from jax.experimental import pallas as pl
from jax.experimental.pallas import tpu as pltpu

F32 = jnp.float32
BF16 = jnp.bfloat16

D_MODEL = 1024
N_META = 16
EPS = 1e-6
GATE_CLAMP = 1.0 - 1e-6
CONV_DIM = 512
CONV_K = 31
MLA_HEADS = 8
Q_RANK = 256
KV_RANK = 128
NOPE_DIM = 64
ROPE_DIM = 32
V_DIM = 64
QK_DIM = NOPE_DIM + ROPE_DIM
ROPE_BASE = 10000.0
HGRN_HEADS = 4
HGRN_DK = 128
HGRN_DV = 128
D_FF = 4096

LANE = 128
HEAD_SLOT = LANE

COL_GATE = 0
COL_CONV = 3072
COL_CQ = 4096
COL_CKV = 4352
COL_KROPE = 4480
COL_HGRN = 4608
N_IN_PAD = 6656

VMEM_LIMIT = 52 * 1024 * 1024
ATTN_VMEM_LIMIT = 58 * 1024 * 1024

TL = 384
TN_IN = 512
N_GATE_CHUNK = COL_CONV // TN_IN
SUBLANE = 8
RC_CONV = 16
CONV_HALO = 32
CONV_SPAN = TL + CONV_HALO - SUBLANE
RC_PREP = 128
TQ = 384
ATTN_HG = 4
LOG2E = 1.4426950408889634
HGRN_C = 128
HGRN_BLK = 4
TM_MERGE = 1024
TM_FF = 1024
TF_FF = 1024


def _rms(x, eps=EPS):
    return x * lax.rsqrt(jnp.mean(x * x, axis=-1, keepdims=True) + eps)


def _sigmoid(x):
    return 0.5 * jnp.tanh(0.5 * x) + 0.5


def _silu(x):
    return x * _sigmoid(x)


def _conv_stage(glu, j, hbuf, sbuf):
    @pl.when(j == 0)
    def _():
        hbuf[0:CONV_HALO, :] = jnp.zeros((CONV_HALO, CONV_DIM), F32)

    @pl.when(j > 0)
    def _():
        hbuf[0:CONV_HALO, :] = hbuf[TL:TL + CONV_HALO, :]

    hbuf[CONV_HALO:CONV_HALO + TL, :] = glu

    for r in range(1, SUBLANE):
        sbuf[r - 1] = hbuf[r:r + CONV_SPAN, :]


def _conv_rows(r0, w_ref, cb_ref, lg_ref, lb_ref, o_ref, hbuf, sbuf):
    base = CONV_HALO - (CONV_K - 1)
    acc = jnp.zeros((RC_CONV, CONV_DIM), F32) + cb_ref[...]
    for k in range(CONV_K):
        r = (base + k) % SUBLANE
        lo = pl.multiple_of(r0 + (base + k - r), SUBLANE)
        src = hbuf if r == 0 else sbuf.at[r - 1]
        wk = jnp.tile(w_ref[k * SUBLANE:(k + 1) * SUBLANE, :], (RC_CONV // SUBLANE, 1))
        acc = acc + wk * src[pl.ds(lo, RC_CONV), :]
    mu = jnp.mean(acc, axis=-1, keepdims=True)
    xc = acc - mu
    y = xc * lax.rsqrt(jnp.mean(xc * xc, axis=-1, keepdims=True) + EPS)
    y = y * lg_ref[...] + lb_ref[...]
    o_ref[0, pl.ds(r0, RC_CONV), :] = (y * jax.nn.sigmoid(y)).astype(o_ref.dtype)


def _rope(x, c, s1, s2):
    half = ROPE_DIM // 2
    return x * c + pltpu.roll(x, LANE - half, 1) * s1 + pltpu.roll(x, half, 1) * s2


def _lane_sumsq(x, ones):
    sq = x * x
    hi = sq.astype(BF16)
    lo = (sq - hi.astype(F32)).astype(BF16)
    return jnp.dot(jnp.concatenate([hi, lo], axis=1), ones, preferred_element_type=F32)


def _mla_rows(cq, ckv, kr, rows, qag_ref, kvag_ref, wuq_ref, wuqs_ref, wuk_ref, wuv_ref,
              qng_ref, qngs_ref, kng_ref, rc_ref, rs1_ref, rs2_ref, vone_ref, q_out, k_out, v_out):
    inv_d = 1.0 / QK_DIM
    ones = jnp.ones((2 * LANE, LANE), BF16)
    cq = (_rms(cq) * qag_ref[...]).astype(BF16)
    q = jnp.dot(cq, wuq_ref[...], preferred_element_type=F32)
    qs = jnp.dot(cq, wuqs_ref[...], preferred_element_type=F32)
    ckv = (_rms(ckv) * kvag_ref[...]).astype(BF16)
    kn = jnp.dot(ckv, wuk_ref[...], preferred_element_type=F32)
    v = jnp.dot(ckv, wuv_ref[...], preferred_element_type=F32) + vone_ref[...]
    rc, rs1, rs2 = rc_ref[rows, :], rs1_ref[rows, :], rs2_ref[rows, :]
    q_c = qng_ref[...] * rc
    q_s = qngs_ref[...] * (rs1 + rs2)
    kr_ss = _lane_sumsq(kr, ones)
    kr_rot = _rope(kr * kng_ref[...], rc, rs1, rs2)
    for h in range(MLA_HEADS):
        sl = slice(h * HEAD_SLOT, (h + 1) * HEAD_SLOT)
        qh = q[:, sl]
        q_out[0, h, rows, :] = ((qh * q_c + qs[:, sl] * q_s) * lax.rsqrt(
            _lane_sumsq(qh, ones) * inv_d + EPS)).astype(q_out.dtype)
        kh = kn[:, sl]
        k_out[0, h, rows, :] = ((kh * kng_ref[...] + kr_rot) * lax.rsqrt(
            (_lane_sumsq(kh, ones) + kr_ss) * inv_d + EPS)).astype(k_out.dtype)
        v_out[0, h, rows, :] = v[:, sl].astype(v_out.dtype)


def _hgrn_head(qh, fr, iv, gv, oml, ng, lvl, st_ref, h):
    C = HGRN_C
    rowid = lax.broadcasted_iota(jnp.int32, (C, HGRN_DK), 0)
    nt = (((1,), (1,)), ((), ()))
    tn = (((0,), (0,)), ((), ()))
    kk = oml * jax.nn.sigmoid(-fr)
    lf = jnp.log1p(-jnp.minimum(kk, GATE_CLAMP)) * LOG2E
    v = _silu(iv).astype(BF16)

    fwd = lf
    bwd = jnp.zeros_like(lf)
    n, level = 1, 0
    a = None
    while n < C:
        if n >= HGRN_BLK:
            qs = (qh * jnp.exp2(fwd)).astype(BF16)
            if a is None:
                a = jnp.where(lvl == 0, lax.dot_general(
                    qs, (kk * jnp.exp2(-fwd)).astype(BF16), nt,
                    preferred_element_type=F32), 0.0)
            level += 1
            ks = (kk * jnp.exp2(bwd)).astype(BF16)
            a = jnp.where(lvl == level,
                          lax.dot_general(qs, ks, nt, preferred_element_type=F32), a)
        tot = fwd + bwd
        upper = (rowid & n) != 0
        fwd = fwd + jnp.where(upper, pltpu.roll(tot, n, 0), 0.0)
        bwd = bwd + jnp.where(upper, 0.0, pltpu.roll(tot, C - n, 0))
        n *= 2

    st = st_ref[h]
    o = jnp.dot(a.astype(BF16), v, preferred_element_type=F32)
    o = o + lax.dot_general((qh * jnp.exp2(fwd)).astype(BF16), st.astype(BF16), nt,
                            preferred_element_type=F32)
    ks = (kk * jnp.exp2(bwd)).astype(BF16)
    st_ref[h] = st * jnp.exp2(fwd[C - 1:C, :]) + lax.dot_general(
        v, ks, tn, preferred_element_type=F32)

    return _rms(o) * ng * _silu(gv)


def _hgrn_levels():
    t = np.arange(HGRN_C)[:, None]
    s = np.arange(HGRN_C)[None, :]
    x = (t ^ s) // HGRN_BLK
    lvl = np.where(x == 0, 0, np.floor(np.log2(np.maximum(x, 1))).astype(np.int64) + 1)
    return jnp.asarray(np.where(s > t, -1, lvl), dtype=jnp.int32)


def _front_kernel(x_ref, g1_ref, wbr_ref, wgate_ref,
                  cw_ref, cb_ref, lg_ref, lb_ref,
                  qag_ref, kvag_ref, wuq_ref, wuqs_ref, wuk_ref, wuv_ref, qng_ref, qngs_ref, kng_ref,
                  rc_ref, rs1_ref, rs2_ref, vone_ref,
                  oml_ref, ng_ref, lvl_ref,
                  gate_out, ha_out, q_out, k_out, v_out, oc_out,
                  h_scr, ym_scr, yh_scr, hbuf, sbuf, st_ref):
    j = pl.program_id(1)
    hw = HGRN_HEADS * HGRN_DK
    n_mla = COL_HGRN - COL_CQ

    @pl.when(j == 0)
    def _():
        st_ref[...] = jnp.zeros(st_ref.shape, F32)

    h = (_rms(x_ref[0]) * g1_ref[...]).astype(BF16)
    h_scr[...] = h
    yc = jnp.dot(h, wbr_ref[:, 0:2 * CONV_DIM], preferred_element_type=F32)
    _conv_stage(yc[:, :CONV_DIM] * _sigmoid(yc[:, CONV_DIM:]), j, hbuf, sbuf)
    ym_scr[...] = jnp.dot(h, wbr_ref[:, 2 * CONV_DIM:2 * CONV_DIM + n_mla],
                          preferred_element_type=F32)
    yh = jnp.dot(h, wbr_ref[:, 2 * CONV_DIM + n_mla:], preferred_element_type=F32)
    for idx in range(4 * HGRN_HEADS):
        yh_scr[idx] = yh[:, idx * HGRN_DK:(idx + 1) * HGRN_DK]

    conv_per_gate = (TL // RC_CONV) // N_GATE_CHUNK

    def gate_and_conv(c, carry):
        gate_out[c, 0] = _sigmoid(jnp.dot(h_scr[...], wgate_ref[c],
                                          preferred_element_type=F32)).astype(gate_out.dtype)
        for t in range(conv_per_gate):
            r0 = pl.multiple_of((c * conv_per_gate + t) * RC_CONV, RC_CONV)
            _conv_rows(r0, cw_ref, cb_ref, lg_ref, lb_ref, ha_out, hbuf, sbuf)
        return carry

    for c in range(N_GATE_CHUNK):
        gate_and_conv(c, 0)

    def mla_chunk(c, carry):
        rows = pl.ds(pl.multiple_of(c * RC_PREP, RC_PREP), RC_PREP)
        _mla_rows(ym_scr[rows, 0:Q_RANK], ym_scr[rows, Q_RANK:Q_RANK + KV_RANK],
                  ym_scr[rows, Q_RANK + KV_RANK:], rows,
                  qag_ref, kvag_ref, wuq_ref, wuqs_ref, wuk_ref, wuv_ref,
                  qng_ref, qngs_ref, kng_ref, rc_ref, rs1_ref, rs2_ref, vone_ref, q_out, k_out, v_out)
        return carry

    for c in range(TL // RC_PREP):
        mla_chunk(c, 0)

    lvl = lvl_ref[...]

    def hgrn_tile(t, carry):
        for cc in range(TL // HGRN_C):
            rows = slice(cc * HGRN_C, (cc + 1) * HGRN_C)
            for hd in range(HGRN_HEADS):
                oc_out[0, rows, hd * HGRN_DV:(hd + 1) * HGRN_DV] = _hgrn_head(
                    yh_scr[hd, rows, :], yh_scr[HGRN_HEADS + hd, rows, :],
                    yh_scr[2 * HGRN_HEADS + hd, rows, :], yh_scr[3 * HGRN_HEADS + hd, rows, :],
                    oml_ref[hd], ng_ref[hd], lvl, st_ref, hd).astype(oc_out.dtype)
        return carry

    lax.fori_loop(0, jnp.minimum(j + 1, 1), hgrn_tile, 0)


def _front(x3, g1, w_br, w_gate, li, conv_w, conv_b, ln_g, ln_b, qag, kvag, wuq, wuqs, wuk, wuv,
           qng, qngs, kng, rc, rs1, rs2, vone, one_minus_lb, hgrn_ng):
    b, lp, d = x3.shape
    hw = MLA_HEADS * HEAD_SLOT
    gw = HGRN_HEADS * HGRN_DK
    n_br = N_IN_PAD - COL_CONV
    const = lambda shape: pl.BlockSpec(shape, lambda bb, j: (0,) * len(shape))
    resident = lambda shape: pl.BlockSpec((None,) + shape, lambda bb, j: (li,) + (0,) * len(shape),
                                          pipeline_mode=pl.Buffered(1))
    tab = pl.BlockSpec((TL, LANE), lambda bb, j: (j, 0))
    seq = lambda w: pl.BlockSpec((1, TL, w), lambda bb, j: (bb, j, 0))
    heads = pl.BlockSpec((1, MLA_HEADS, TL, HEAD_SLOT), lambda bb, j: (bb, 0, j, 0))
    qkv = jax.ShapeDtypeStruct((b, MLA_HEADS, lp, HEAD_SLOT), BF16)
    return pl.pallas_call(
        _front_kernel,
        out_shape=(jax.ShapeDtypeStruct((N_GATE_CHUNK, b, lp, TN_IN), BF16),
                   jax.ShapeDtypeStruct((b, lp, CONV_DIM), BF16),
                   qkv, qkv, qkv,
                   jax.ShapeDtypeStruct((b, lp, gw), BF16)),
        grid=(b, lp // TL),
        in_specs=[
            seq(d), const((1, d)), resident((d, n_br)), resident((N_GATE_CHUNK, d, TN_IN)),
            const((CONV_K * SUBLANE, CONV_DIM)), const((1, CONV_DIM)), const((1, CONV_DIM)),
            const((1, CONV_DIM)),
            const((1, Q_RANK)), const((1, KV_RANK)),
            const((Q_RANK, hw)), const((Q_RANK, hw)), const((KV_RANK, hw)), const((KV_RANK, hw)),
            const((1, LANE)), const((1, LANE)), const((1, LANE)),
            tab, tab, tab, const((1, hw)),
            const((HGRN_HEADS, 1, HGRN_DK)), const((HGRN_HEADS, 1, HGRN_DV)), const((HGRN_C, HGRN_C)),
        ],
        out_specs=(pl.BlockSpec((N_GATE_CHUNK, 1, TL, TN_IN), lambda bb, j: (0, bb, j, 0)),
                   seq(CONV_DIM), heads, heads, heads, seq(gw)),
        scratch_shapes=[pltpu.VMEM((TL, d), BF16),
                        pltpu.VMEM((TL, COL_HGRN - COL_CQ), F32),
                        pltpu.VMEM((4 * HGRN_HEADS, TL, HGRN_DK), F32),
                        pltpu.VMEM((CONV_HALO + TL, CONV_DIM), F32),
                        pltpu.VMEM((SUBLANE - 1, CONV_SPAN, CONV_DIM), F32),
                        pltpu.VMEM((HGRN_HEADS, HGRN_DV, HGRN_DK), F32)],
        compiler_params=pltpu.CompilerParams(
            dimension_semantics=("parallel", "arbitrary"), vmem_limit_bytes=VMEM_LIMIT),
    )(x3, g1, w_br, w_gate, conv_w, conv_b, ln_g, ln_b, qag, kvag, wuq, wuqs, wuk, wuv,
      qng, qngs, kng, rc, rs1, rs2, vone, one_minus_lb, hgrn_ng, _hgrn_levels())


def _attn_tile(i, q_ref, k_ref, v_ref, o_ref, s_ref, m_ref, acc_ref):
    nt = (((1,), (1,)), ((), ()))
    heads = range(ATTN_HG)
    q_rows = pl.ds(pl.multiple_of(i * TQ, TQ), TQ)
    qs = [q_ref[0, h, q_rows, :] for h in heads]
    odd = (i & 1) == 1

    def scores(h, tile0, ntiles):
        kt = k_ref[0, h, pl.ds(pl.multiple_of(tile0 * TQ, TQ), ntiles * TQ), :]
        return lax.dot_general(qs[h], kt, nt, preferred_element_type=F32)

    def fold(s):
        m = s[:, 0:LANE]
        for c in range(1, s.shape[1] // LANE):
            m = jnp.maximum(m, s[:, c * LANE:(c + 1) * LANE])
        return m

    def score_span(tile0, ntiles, m_in):
        out = []
        for h in heads:
            s = scores(h, tile0, ntiles)
            for t in range(ntiles):
                s_ref[h, tile0 + t] = s[:, t * TQ:(t + 1) * TQ]
            out.append(jnp.maximum(m_in[h], fold(s)))
        return tuple(out)

    def pv_span(tile0, ntiles):
        out = []
        for h in heads:
            s = jnp.concatenate([s_ref[h, tile0 + t] for t in range(ntiles)], axis=1)
            p = jnp.exp2(s - ms[h]).astype(BF16)
            vt = v_ref[0, h, pl.ds(pl.multiple_of(tile0 * TQ, TQ), ntiles * TQ), :]
            out.append(jnp.dot(p, vt, preferred_element_type=F32))
        return out

    quads = lax.shift_right_logical(i, 2)
    mloc = lax.fori_loop(0, quads, lambda t, m: score_span(4 * t, 4, m),
                         tuple(jnp.full((TQ, LANE), -1e30, F32) for _ in heads))
    for h in heads:
        m_ref[h] = mloc[h]

    @pl.when((i & 2) != 0)
    def _():
        m_new = score_span(4 * quads, 2, [m_ref[h] for h in heads])
        for h in heads:
            m_ref[h] = m_new[h]

    keep = (lax.broadcasted_iota(jnp.int32, (TQ, TQ), 0)
            >= lax.broadcasted_iota(jnp.int32, (TQ, TQ), 1))

    @pl.when(odd)
    def _():
        for h in heads:
            s = scores(h, i - 1, 2)
            lo, hi = s[:, :TQ], jnp.where(keep, s[:, TQ:], -1e30)
            s_ref[h, i - 1] = lo
            s_ref[h, i] = hi
            m_ref[h] = jnp.maximum(m_ref[h], jnp.maximum(fold(lo), fold(hi)))

    @pl.when(jnp.logical_not(odd))
    def _():
        for h in heads:
            s = jnp.where(keep, scores(h, i, 1), -1e30)
            s_ref[h, i] = s
            m_ref[h] = jnp.maximum(m_ref[h], fold(s))

    ms = [jnp.max(m_ref[h], axis=-1, keepdims=True) for h in heads]

    quads2 = lax.shift_right_logical(i + 1, 2)

    def pass2(t, acc):
        part = pv_span(4 * t, 4)
        return tuple(acc[h] + part[h] for h in heads)

    acc = lax.fori_loop(0, quads2, pass2, tuple(jnp.zeros((TQ, HEAD_SLOT), F32) for _ in heads))
    for h in heads:
        acc_ref[h] = acc[h]

    @pl.when(((i + 1) & 2) != 0)
    def _():
        part = pv_span(4 * quads2, 2)
        for h in heads:
            acc_ref[h] += part[h]

    @pl.when(jnp.logical_not(odd))
    def _():
        part = pv_span(i, 1)
        for h in heads:
            acc_ref[h] += part[h]

    first = lax.broadcasted_iota(jnp.int32, (TQ, HEAD_SLOT), 1) < V_DIM
    for hp in range(ATTN_HG // 2):
        a, b = acc_ref[2 * hp], acc_ref[2 * hp + 1]
        den = pltpu.roll(jnp.where(first, b, a), V_DIM, 1)
        o_ref[0, q_rows, hp * HEAD_SLOT:(hp + 1) * HEAD_SLOT] = (
            jnp.where(first, a, b) / den).astype(o_ref.dtype)


def _attn_kernel(q_ref, k_ref, v_ref, o_ref, s_ref, m_ref, acc_ref):
    def q_tile(i, carry):
        _attn_tile(i, q_ref, k_ref, v_ref, o_ref, s_ref, m_ref, acc_ref)
        return carry

    lax.fori_loop(0, q_ref.shape[2] // TQ, q_tile, 0)


def _attention(q, k, v):
    b, nh, lp, hs = q.shape
    nq = lp // TQ
    whole = pl.BlockSpec((1, ATTN_HG, lp, hs), lambda bb, g: (bb, g, 0, 0))
    whole_once = pl.BlockSpec((1, ATTN_HG, lp, hs), lambda bb, g: (bb, g, 0, 0),
                              pipeline_mode=pl.Buffered(1))
    return pl.pallas_call(
        _attn_kernel,
        out_shape=jax.ShapeDtypeStruct((b, lp, nh * V_DIM), BF16),
        grid=(b, nh // ATTN_HG),
        in_specs=[whole_once, whole, whole],
        out_specs=pl.BlockSpec((1, lp, ATTN_HG * V_DIM), lambda bb, g: (bb, 0, g)),
        scratch_shapes=[pltpu.VMEM((ATTN_HG, nq, TQ, TQ), F32),
                        pltpu.VMEM((ATTN_HG, TQ, LANE), F32),
                        pltpu.VMEM((ATTN_HG, TQ, HEAD_SLOT), F32)],
        compiler_params=pltpu.CompilerParams(
            dimension_semantics=("parallel", "parallel"), vmem_limit_bytes=ATTN_VMEM_LIMIT),
    )(q, k, v)


def _merge_kernel(ha_ref, ob_ref, oc_ref, gate_ref, x_ref, wa_ref, wb_ref, wc_ref, wo_ref, o_ref):
    mix = None
    for n, (br_ref, w_ref) in enumerate(((ha_ref, wa_ref), (ob_ref, wb_ref), (oc_ref, wc_ref))):
        y = jnp.dot(br_ref[...], w_ref[...], preferred_element_type=F32)
        per = D_MODEL // TN_IN
        gate = jnp.concatenate([gate_ref[n * per + t] for t in range(per)], axis=1)
        y = y * gate.astype(F32)
        mix = y if mix is None else mix + y
    o_ref[...] = x_ref[...] + jnp.dot(mix.astype(BF16), wo_ref[...], preferred_element_type=F32)


def _merge(ha, ob, oc, gates, x2, wa, wb, wc, wo, li):
    t = x2.shape[0]
    br = lambda: pl.BlockSpec((TM_MERGE, 512), lambda i: (i, 0))
    wbr = lambda: pl.BlockSpec((None, 512, D_MODEL), lambda i: (li, 0, 0))
    return pl.pallas_call(
        _merge_kernel,
        out_shape=jax.ShapeDtypeStruct((t, D_MODEL), F32),
        grid=(t // TM_MERGE,),
        in_specs=[br(), br(), br(),
                  pl.BlockSpec((N_GATE_CHUNK, TM_MERGE, TN_IN), lambda i: (0, i, 0)),
                  pl.BlockSpec((TM_MERGE, D_MODEL), lambda i: (i, 0)),
                  wbr(), wbr(), wbr(),
                  pl.BlockSpec((None, D_MODEL, D_MODEL), lambda i: (li, 0, 0))],
        out_specs=pl.BlockSpec((TM_MERGE, D_MODEL), lambda i: (i, 0)),
        compiler_params=pltpu.CompilerParams(
            dimension_semantics=("parallel",), vmem_limit_bytes=VMEM_LIMIT),
    )(ha, ob, oc, gates, x2, wa, wb, wc, wo)


def _ffn_rows(x, g_ref, w1_ref, w2_ref):
    h = (_rms(x) * g_ref[...]).astype(BF16)
    acc = x
    for c in range(D_FF // TF_FF):
        cols = slice(c * TF_FF, (c + 1) * TF_FF)
        a = jnp.maximum(jnp.dot(h, w1_ref[:, cols], preferred_element_type=F32), 0.0)
        acc = acc + jnp.dot((a * a).astype(BF16), w2_ref[cols, :], preferred_element_type=F32)
    return acc


def _ffn_kernel(x_ref, g_ref, w1_ref, w2_ref, o_ref):
    o_ref[...] = _ffn_rows(x_ref[...], g_ref, w1_ref, w2_ref)


def _ffn_weight_specs(li):
    return [
        pl.BlockSpec((None, D_MODEL, D_FF), lambda *_: (li, 0, 0), pipeline_mode=pl.Buffered(1)),
        pl.BlockSpec((None, D_FF, D_MODEL), lambda *_: (li, 0, 0), pipeline_mode=pl.Buffered(1)),
    ]


def _ffn(x2, g, w1, w2, li):
    t = x2.shape[0]
    return pl.pallas_call(
        _ffn_kernel,
        out_shape=jax.ShapeDtypeStruct((t, D_MODEL), F32),
        grid=(t // TM_FF,),
        in_specs=[
            pl.BlockSpec((TM_FF, D_MODEL), lambda i: (i, 0)),
            pl.BlockSpec((1, D_MODEL), lambda i: (0, 0)),
        ] + _ffn_weight_specs(li),
        out_specs=pl.BlockSpec((TM_FF, D_MODEL), lambda i: (i, 0)),
        compiler_params=pltpu.CompilerParams(
            dimension_semantics=("parallel",), vmem_limit_bytes=VMEM_LIMIT),
    )(x2, g, w1, w2)


def _ffn_final(x2, g, w1, w2, li, b, lp, n_lead, n_out):
    assert lp % SUBLANE == 0 and n_lead % SUBLANE == 0
    return pl.pallas_call(
        _ffn_kernel,
        out_shape=jax.ShapeDtypeStruct((b, n_out, D_MODEL), F32),
        grid=(b, n_out // TM_FF),
        in_specs=[
            pl.BlockSpec((pl.Element(TM_FF), pl.Element(D_MODEL)),
                         lambda bb, t: (pl.multiple_of(bb * lp + n_lead + t * TM_FF, SUBLANE), 0)),
            pl.BlockSpec((1, D_MODEL), lambda bb, t: (0, 0)),
        ] + _ffn_weight_specs(li),
        out_specs=pl.BlockSpec((None, TM_FF, D_MODEL), lambda bb, t: (bb, t, 0)),
        compiler_params=pltpu.CompilerParams(
            dimension_semantics=("parallel", "parallel"), vmem_limit_bytes=VMEM_LIMIT),
    )(x2, g, w1, w2)


def _regroup_w_in(w):
    o = 0
    conv = w[..., o:o + 2 * CONV_DIM]; o += 2 * CONV_DIM
    cq = w[..., o:o + Q_RANK]; o += Q_RANK
    ckv = w[..., o:o + KV_RANK]; o += KV_RANK
    kr = w[..., o:o + ROPE_DIM]; o += ROPE_DIM
    hg = w[..., o:o + 4 * HGRN_HEADS * HGRN_DK]; o += 4 * HGRN_HEADS * HGRN_DK
    gate = w[..., o:]
    kr_slot = jnp.pad(kr, ((0, 0), (0, 0), (NOPE_DIM, LANE - QK_DIM)))
    w_br = jnp.concatenate([conv, cq, ckv, kr_slot, hg], axis=-1).astype(BF16)
    depth, d, _ = w.shape
    w_gate = gate.astype(BF16).reshape(depth, d, N_GATE_CHUNK, TN_IN).transpose(0, 2, 1, 3)
    return w_br, w_gate


def _head_slots(w, per_head, start, width):
    r = w.shape[0]
    wh = w.reshape(r, MLA_HEADS, per_head)[:, :, start:start + width]
    wh = jnp.pad(wh, ((0, 0), (0, 0), (0, HEAD_SLOT - width)))
    return wh.reshape(r, MLA_HEADS * HEAD_SLOT)


def _swap_rope_halves(a):
    half = ROPE_DIM // 2
    return jnp.concatenate([jnp.zeros_like(a[..., :NOPE_DIM]), a[..., NOPE_DIM + half:],
                            a[..., NOPE_DIM:NOPE_DIM + half]], axis=-1)


def _v_slots(w_ukv):
    r = w_ukv.shape[0]
    wv = w_ukv.reshape(r, MLA_HEADS // 2, 2, NOPE_DIM + V_DIM)[..., NOPE_DIM:]
    z = jnp.zeros_like(wv[:, :, 0])
    slots = jnp.stack([jnp.concatenate([wv[:, :, 0], z], axis=-1),
                       jnp.concatenate([z, wv[:, :, 1]], axis=-1)], axis=2)
    ones = jnp.ones((MLA_HEADS // 2, V_DIM), F32)
    zo = jnp.zeros_like(ones)
    vone = jnp.stack([jnp.concatenate([zo, ones], axis=-1),
                      jnp.concatenate([ones, zo], axis=-1)], axis=1)
    return slots.reshape(r, MLA_HEADS * HEAD_SLOT), vone.reshape(1, MLA_HEADS * HEAD_SLOT)


def _rope_tables(lp):
    half = ROPE_DIM // 2
    pos = jnp.arange(lp, dtype=F32)
    inv_freq = ROPE_BASE ** (-jnp.arange(half, dtype=F32) / half)
    ang = pos[:, None] * inv_freq[None, :]
    cos, sin = jnp.cos(ang), jnp.sin(ang)
    ones = jnp.ones((lp, NOPE_DIM), F32)
    z16 = jnp.zeros((lp, half), F32)
    z64 = jnp.zeros((lp, NOPE_DIM), F32)
    tail = jnp.zeros((lp, LANE - QK_DIM), F32)
    rc = jnp.concatenate([ones, cos, cos, tail], axis=1)
    rs1 = jnp.concatenate([z64, -sin, z16, tail], axis=1)
    rs2 = jnp.concatenate([z64, z16, sin, tail], axis=1)
    return rc, rs1, rs2


def kernel(x, meta, norm1_g, w_in, conv_w, conv_b, conv_ln_g, conv_ln_b, w_conv_out, q_a_norm_g, w_uq, kv_a_norm_g, w_ukv, q_norm_g, k_norm_g, w_attn_out, hgrn_lb_logits, hgrn_norm_g, w_hgrn_out, w_out, norm2_g, w_ff1, w_ff2):
    b, seq, d = x.shape
    depth = w_in.shape[0]
    l = seq + N_META
    lp = -(-l // TL) * TL
    xc = jnp.concatenate([jnp.broadcast_to(meta[None].astype(x.dtype), (b, N_META, d)), x,
                          jnp.zeros((b, lp - l, d), x.dtype)], axis=1)
    x2 = xc.reshape(b * lp, d)

    rc, rs1, rs2 = _rope_tables(lp)
    p_lb = jax.nn.softmax(hgrn_lb_logits.astype(F32), axis=0)
    lower_bounds = jnp.cumsum(p_lb, axis=0) - p_lb[0:1]
    row = lambda a: a.astype(F32).reshape(1, -1)
    pad_qk = lambda g: jnp.pad(g.astype(F32), (0, LANE - QK_DIM)).reshape(1, LANE)
    w_a, w_b, w_c, w_o = (w.astype(BF16) for w in (w_conv_out, w_attn_out, w_hgrn_out, w_out))
    w_1, w_2 = w_ff1.astype(BF16), w_ff2.astype(BF16)
    w_br, w_gate = _regroup_w_in(w_in)

    for li in range(depth):
        wuv, vone = _v_slots(w_ukv[li])
        w_uq_heads = w_uq[li].reshape(Q_RANK, MLA_HEADS, QK_DIM)
        w_uq_swapped = _swap_rope_halves(w_uq_heads).reshape(Q_RANK, MLA_HEADS * QK_DIM)
        q_gain = q_norm_g[li].astype(F32) * (QK_DIM ** -0.5 * LOG2E)
        gates, ha, q, k, v, oc = _front(
            x2.reshape(b, lp, d), row(norm1_g[li]), w_br, w_gate, li,
            jnp.repeat(conv_w[li].astype(F32), SUBLANE, axis=0),
            row(conv_b[li]), row(conv_ln_g[li]), row(conv_ln_b[li]),
            row(q_a_norm_g[li]), row(kv_a_norm_g[li]),
            _head_slots(w_uq[li], QK_DIM, 0, QK_DIM).astype(BF16),
            _head_slots(w_uq_swapped, QK_DIM, 0, QK_DIM).astype(BF16),
            _head_slots(w_ukv[li], NOPE_DIM + V_DIM, 0, NOPE_DIM).astype(BF16),
            wuv.astype(BF16),
            pad_qk(q_gain), pad_qk(_swap_rope_halves(q_gain)), pad_qk(k_norm_g[li]),
            rc, rs1, rs2, vone,
            (1.0 - lower_bounds[li]).reshape(HGRN_HEADS, 1, HGRN_DK),
            hgrn_norm_g[li].astype(F32).reshape(HGRN_HEADS, 1, HGRN_DV))
        ob = _attention(q, k, v)
        flat = lambda a: a.reshape(b * lp, -1)
        x2 = _merge(flat(ha), flat(ob), flat(oc), gates.reshape(N_GATE_CHUNK, b * lp, TN_IN), x2,
                    w_a, w_b, w_c, w_o, li)
        if li + 1 < depth:
            x2 = _ffn(x2, row(norm2_g[li]), w_1, w_2, li)
    assert seq % TM_FF == 0
    return _ffn_final(x2, row(norm2_g[depth - 1]), w_1, w_2, depth - 1, b, lp, N_META, seq)
```

```python
import functools

import jax
import jax.numpy as jnp
import numpy as np
from jax import lax
from jax.experimental import pallas as pl
from jax.experimental.pallas import tpu as pltpu

F32 = jnp.float32
BF16 = jnp.bfloat16

D_MODEL = 1024
N_META = 16
EPS = 1e-6
GATE_CLAMP = 1.0 - 1e-6
CONV_DIM = 512
CONV_K = 31
MLA_HEADS = 8
Q_RANK = 256
KV_RANK = 128
NOPE_DIM = 64
ROPE_DIM = 32
V_DIM = 64
QK_DIM = NOPE_DIM + ROPE_DIM
ROPE_BASE = 10000.0
HGRN_HEADS = 4
HGRN_DK = 128
HGRN_DV = 128
D_FF = 4096

LANE = 128
HEAD_SLOT = LANE

COL_GATE = 0
COL_CONV = 3072
COL_CQ = 4096
COL_CKV = 4352
COL_KROPE = 4480
COL_HGRN = 4608
N_IN_PAD = 6656

VMEM_LIMIT = 52 * 1024 * 1024

TL = 384
TN_IN = 512
N_GATE_CHUNK = COL_CONV // TN_IN
SUBLANE = 8
RC_CONV = 16
CONV_HALO = 32
CONV_SPAN = TL + CONV_HALO - SUBLANE
RC_PREP = 128
TQ = 384
ATTN_HG = 4
LOG2E = 1.4426950408889634
HGRN_C = 128
HGRN_BLK = 4
TM_MERGE = 1024
TM_MERGE0 = 704
TM_FF = 1024
TF_FF = 1024


def _rms(x, eps=EPS):
    return x * lax.rsqrt(jnp.mean(x * x, axis=-1, keepdims=True) + eps)


def _sigmoid(x):
    return 0.5 * jnp.tanh(0.5 * x) + 0.5


def _silu(x):
    return x * _sigmoid(x)


def _conv_stage(glu, j, hbuf, sbuf):
    @pl.when(j == 0)
    def _():
        hbuf[0:CONV_HALO, :] = jnp.zeros((CONV_HALO, CONV_DIM), F32)

    @pl.when(j > 0)
    def _():
        hbuf[0:CONV_HALO, :] = hbuf[TL:TL + CONV_HALO, :]

    hbuf[CONV_HALO:CONV_HALO + TL, :] = glu

    for r in range(1, SUBLANE):
        sbuf[r - 1] = hbuf[r:r + CONV_SPAN, :]


def _conv_rows(r0, w_ref, cb_ref, lg_ref, lb_ref, o_ref, hbuf, sbuf):
    base = CONV_HALO - (CONV_K - 1)
    acc = jnp.zeros((RC_CONV, CONV_DIM), F32) + cb_ref[...]
    for k in range(CONV_K):
        r = (base + k) % SUBLANE
        lo = pl.multiple_of(r0 + (base + k - r), SUBLANE)
        src = hbuf if r == 0 else sbuf.at[r - 1]
        wk = jnp.tile(w_ref[k * SUBLANE:(k + 1) * SUBLANE, :], (RC_CONV // SUBLANE, 1))
        acc = acc + wk * src[pl.ds(lo, RC_CONV), :]
    mu = jnp.mean(acc, axis=-1, keepdims=True)
    xc = acc - mu
    y = xc * lax.rsqrt(jnp.mean(xc * xc, axis=-1, keepdims=True) + EPS)
    y = y * lg_ref[...] + lb_ref[...]
    o_ref[0, pl.ds(r0, RC_CONV), :] = (y * jax.nn.sigmoid(y)).astype(o_ref.dtype)


def _rope(x, c, s1, s2):
    half = ROPE_DIM // 2
    return x * c + pltpu.roll(x, LANE - half, 1) * s1 + pltpu.roll(x, half, 1) * s2


def _lane_sumsq(x, ones):
    sq = x * x
    hi = sq.astype(BF16)
    lo = (sq - hi.astype(F32)).astype(BF16)
    return jnp.dot(jnp.concatenate([hi, lo], axis=1), ones, preferred_element_type=F32)


def _mla_rows(cq, ckv, kr, rows, qag_ref, kvag_ref, wuq_ref, wuqs_ref, wuk_ref, wuv_ref,
              qng_ref, qngs_ref, kng_ref, rc_ref, rs1_ref, rs2_ref, vone_ref, q_out, k_out, v_out):
    inv_d = 1.0 / QK_DIM
    ones = jnp.ones((2 * LANE, LANE), BF16)
    cq = (_rms(cq) * qag_ref[...]).astype(BF16)
    q = jnp.dot(cq, wuq_ref[...], preferred_element_type=F32)
    qs = jnp.dot(cq, wuqs_ref[...], preferred_element_type=F32)
    ckv = (_rms(ckv) * kvag_ref[...]).astype(BF16)
    kn = jnp.dot(ckv, wuk_ref[...], preferred_element_type=F32)
    v = jnp.dot(ckv, wuv_ref[...], preferred_element_type=F32) + vone_ref[...]
    rc, rs1, rs2 = rc_ref[rows, :], rs1_ref[rows, :], rs2_ref[rows, :]
    q_c = qng_ref[...] * rc
    q_s = qngs_ref[...] * (rs1 + rs2)
    kr_ss = _lane_sumsq(kr, ones)
    kr_rot = _rope(kr * kng_ref[...], rc, rs1, rs2)
    for h in range(MLA_HEADS):
        sl = slice(h * HEAD_SLOT, (h + 1) * HEAD_SLOT)
        qh = q[:, sl]
        q_out[0, h, rows, :] = ((qh * q_c + qs[:, sl] * q_s) * lax.rsqrt(
            _lane_sumsq(qh, ones) * inv_d + EPS)).astype(q_out.dtype)
        kh = kn[:, sl]
        k_out[0, h, rows, :] = ((kh * kng_ref[...] + kr_rot) * lax.rsqrt(
            (_lane_sumsq(kh, ones) + kr_ss) * inv_d + EPS)).astype(k_out.dtype)
        v_out[0, h, rows, :] = v[:, sl].astype(v_out.dtype)


def _hgrn_head(qh, fr, iv, gv, oml, ng, lvl, st_ref, h):
    C = HGRN_C
    rowid = lax.broadcasted_iota(jnp.int32, (C, HGRN_DK), 0)
    nt = (((1,), (1,)), ((), ()))
    tn = (((0,), (0,)), ((), ()))
    kk = oml * jax.nn.sigmoid(-fr)
    lf = jnp.log1p(-jnp.minimum(kk, GATE_CLAMP)) * LOG2E
    v = _silu(iv).astype(BF16)

    fwd = lf
    bwd = jnp.zeros_like(lf)
    n, level = 1, 0
    a = None
    while n < C:
        if n >= HGRN_BLK:
            qs = (qh * jnp.exp2(fwd)).astype(BF16)
            if a is None:
                a = jnp.where(lvl == 0, lax.dot_general(
                    qs, (kk * jnp.exp2(-fwd)).astype(BF16), nt,
                    preferred_element_type=F32), 0.0)
            level += 1
            ks = (kk * jnp.exp2(bwd)).astype(BF16)
            a = jnp.where(lvl == level,
                          lax.dot_general(qs, ks, nt, preferred_element_type=F32), a)
        tot = fwd + bwd
        upper = (rowid & n) != 0
        fwd = fwd + jnp.where(upper, pltpu.roll(tot, n, 0), 0.0)
        bwd = bwd + jnp.where(upper, 0.0, pltpu.roll(tot, C - n, 0))
        n *= 2

    st = st_ref[h]
    o = jnp.dot(a.astype(BF16), v, preferred_element_type=F32)
    o = o + lax.dot_general((qh * jnp.exp2(fwd)).astype(BF16), st.astype(BF16), nt,
                            preferred_element_type=F32)
    ks = (kk * jnp.exp2(bwd)).astype(BF16)
    st_ref[h] = st * jnp.exp2(fwd[C - 1:C, :]) + lax.dot_general(
        v, ks, tn, preferred_element_type=F32)

    return _rms(o) * ng * _silu(gv)


def _hgrn_levels():
    t = np.arange(HGRN_C)[:, None]
    s = np.arange(HGRN_C)[None, :]
    x = (t ^ s) // HGRN_BLK
    lvl = np.where(x == 0, 0, np.floor(np.log2(np.maximum(x, 1))).astype(np.int64) + 1)
    return jnp.asarray(np.where(s > t, -1, lvl), dtype=jnp.int32)


def _window_start(j, tile, seq_len):
    return jnp.clip(j * tile - N_META, 0, seq_len - tile)


def _padded_rows(win, meta_ref, j, n_tiles, seq_len):
    tile, d = win.shape
    drop = tile * (n_tiles - 1) - N_META - (seq_len - tile)
    first = jnp.concatenate([meta_ref[...].astype(win.dtype), win[:tile - N_META]], axis=0)
    last = jnp.concatenate([win[drop:], jnp.zeros((drop, d), win.dtype)], axis=0)
    return jnp.where(j == 0, first, jnp.where(j == n_tiles - 1, last, win))


def _front_kernel(*refs, seq_len=None, n_tiles=None):
    if seq_len is None:
        x_ref, *rest = refs
        x_tile = x_ref[0]
    else:
        x_ref, meta_ref, *rest = refs
        x_tile = _padded_rows(x_ref[...], meta_ref, pl.program_id(1), n_tiles, seq_len)
    _front_tile(x_tile, *rest)


def _front_tile(x_tile, g1_ref, wbr_ref, wgate_ref,
                cw_ref, cb_ref, lg_ref, lb_ref,
                qag_ref, kvag_ref, wuq_ref, wuqs_ref, wuk_ref, wuv_ref, qng_ref, qngs_ref, kng_ref,
                rc_ref, rs1_ref, rs2_ref, vone_ref,
                oml_ref, ng_ref, lvl_ref,
                gate_out, ha_out, q_out, k_out, v_out, oc_out,
                h_scr, ym_scr, yh_scr, hbuf, sbuf, st_ref):
    j = pl.program_id(1)
    hw = HGRN_HEADS * HGRN_DK
    n_mla = COL_HGRN - COL_CQ

    @pl.when(j == 0)
    def _():
        st_ref[...] = jnp.zeros(st_ref.shape, F32)

    h = (_rms(x_tile) * g1_ref[...]).astype(BF16)
    h_scr[...] = h
    yc = jnp.dot(h, wbr_ref[:, 0:2 * CONV_DIM], preferred_element_type=F32)
    _conv_stage(yc[:, :CONV_DIM] * _sigmoid(yc[:, CONV_DIM:]), j, hbuf, sbuf)
    ym_scr[...] = jnp.dot(h, wbr_ref[:, 2 * CONV_DIM:2 * CONV_DIM + n_mla],
                          preferred_element_type=F32)
    yh = jnp.dot(h, wbr_ref[:, 2 * CONV_DIM + n_mla:], preferred_element_type=F32)
    for idx in range(4 * HGRN_HEADS):
        yh_scr[idx] = yh[:, idx * HGRN_DK:(idx + 1) * HGRN_DK]

    conv_per_gate = (TL // RC_CONV) // N_GATE_CHUNK

    def gate_and_conv(c, carry):
        gate_out[c, 0] = _sigmoid(jnp.dot(h_scr[...], wgate_ref[c],
                                          preferred_element_type=F32)).astype(gate_out.dtype)
        for t in range(conv_per_gate):
            r0 = pl.multiple_of((c * conv_per_gate + t) * RC_CONV, RC_CONV)
            _conv_rows(r0, cw_ref, cb_ref, lg_ref, lb_ref, ha_out, hbuf, sbuf)
        return carry

    for c in range(N_GATE_CHUNK):
        gate_and_conv(c, 0)

    def mla_chunk(c, carry):
        rows = pl.ds(pl.multiple_of(c * RC_PREP, RC_PREP), RC_PREP)
        _mla_rows(ym_scr[rows, 0:Q_RANK], ym_scr[rows, Q_RANK:Q_RANK + KV_RANK],
                  ym_scr[rows, Q_RANK + KV_RANK:], rows,
                  qag_ref, kvag_ref, wuq_ref, wuqs_ref, wuk_ref, wuv_ref,
                  qng_ref, qngs_ref, kng_ref, rc_ref, rs1_ref, rs2_ref, vone_ref, q_out, k_out, v_out)
        return carry

    for c in range(TL // RC_PREP):
        mla_chunk(c, 0)

    lvl = lvl_ref[...]

    def hgrn_tile(t, carry):
        for cc in range(TL // HGRN_C):
            rows = slice(cc * HGRN_C, (cc + 1) * HGRN_C)
            for hd in range(HGRN_HEADS):
                oc_out[0, rows, hd * HGRN_DV:(hd + 1) * HGRN_DV] = _hgrn_head(
                    yh_scr[hd, rows, :], yh_scr[HGRN_HEADS + hd, rows, :],
                    yh_scr[2 * HGRN_HEADS + hd, rows, :], yh_scr[3 * HGRN_HEADS + hd, rows, :],
                    oml_ref[hd], ng_ref[hd], lvl, st_ref, hd).astype(oc_out.dtype)
        return carry

    lax.fori_loop(0, jnp.minimum(j + 1, 1), hgrn_tile, 0)


def _x_window_specs(tile, seq_len, d, index_of):
    assert seq_len % SUBLANE == 0 and tile % SUBLANE == 0 and N_META % SUBLANE == 0
    return [
        pl.BlockSpec((pl.Element(tile), pl.Element(d)),
                     lambda *ids: (pl.multiple_of(
                         index_of(*ids)[0] * seq_len + _window_start(index_of(*ids)[1], tile, seq_len),
                         SUBLANE), 0)),
        pl.BlockSpec((N_META, d), lambda *ids: (0, 0)),
    ]


def _front(x_in, g1, w_br, w_gate, li, conv_w, conv_b, ln_g, ln_b, qag, kvag, wuq, wuqs, wuk, wuv,
           qng, qngs, kng, rc, rs1, rs2, vone, one_minus_lb, hgrn_ng, lp):
    first_layer = isinstance(x_in, tuple)
    b, seq_len, d = x_in[0].shape if first_layer else x_in.shape
    hw = MLA_HEADS * HEAD_SLOT
    gw = HGRN_HEADS * HGRN_DK
    n_br = N_IN_PAD - COL_CONV
    const = lambda shape: pl.BlockSpec(shape, lambda bb, j: (0,) * len(shape))
    resident = lambda shape: pl.BlockSpec((None,) + shape, lambda bb, j: (li,) + (0,) * len(shape),
                                          pipeline_mode=pl.Buffered(1))
    tab = pl.BlockSpec((TL, LANE), lambda bb, j: (j, 0))
    seq = lambda w: pl.BlockSpec((1, TL, w), lambda bb, j: (bb, j, 0))
    heads = pl.BlockSpec((1, MLA_HEADS, TL, HEAD_SLOT), lambda bb, j: (bb, 0, j, 0))
    qkv = jax.ShapeDtypeStruct((b, MLA_HEADS, lp, HEAD_SLOT), BF16)
    if first_layer:
        x_args = (x_in[0].reshape(b * seq_len, d), x_in[1])
        x_specs = _x_window_specs(TL, seq_len, d, lambda bb, j: (bb, j))
        body = functools.partial(_front_kernel, seq_len=seq_len, n_tiles=lp // TL)
    else:
        x_args, x_specs, body = (x_in,), [seq(d)], _front_kernel
    return pl.pallas_call(
        body,
        out_shape=(jax.ShapeDtypeStruct((N_GATE_CHUNK, b, lp, TN_IN), BF16),
                   jax.ShapeDtypeStruct((b, lp, CONV_DIM), BF16),
                   qkv, qkv, qkv,
                   jax.ShapeDtypeStruct((b, lp, gw), BF16)),
        grid=(b, lp // TL),
        in_specs=x_specs + [
            const((1, d)), resident((d, n_br)), resident((N_GATE_CHUNK, d, TN_IN)),
            const((CONV_K * SUBLANE, CONV_DIM)), const((1, CONV_DIM)), const((1, CONV_DIM)),
            const((1, CONV_DIM)),
            const((1, Q_RANK)), const((1, KV_RANK)),
            const((Q_RANK, hw)), const((Q_RANK, hw)), const((KV_RANK, hw)), const((KV_RANK, hw)),
            const((1, LANE)), const((1, LANE)), const((1, LANE)),
            tab, tab, tab, const((1, hw)),
            const((HGRN_HEADS, 1, HGRN_DK)), const((HGRN_HEADS, 1, HGRN_DV)), const((HGRN_C, HGRN_C)),
        ],
        out_specs=(pl.BlockSpec((N_GATE_CHUNK, 1, TL, TN_IN), lambda bb, j: (0, bb, j, 0)),
                   seq(CONV_DIM), heads, heads, heads, seq(gw)),
        scratch_shapes=[pltpu.VMEM((TL, d), BF16),
                        pltpu.VMEM((TL, COL_HGRN - COL_CQ), F32),
                        pltpu.VMEM((4 * HGRN_HEADS, TL, HGRN_DK), F32),
                        pltpu.VMEM((CONV_HALO + TL, CONV_DIM), F32),
                        pltpu.VMEM((SUBLANE - 1, CONV_SPAN, CONV_DIM), F32),
                        pltpu.VMEM((HGRN_HEADS, HGRN_DV, HGRN_DK), F32)],
        compiler_params=pltpu.CompilerParams(
            dimension_semantics=("parallel", "arbitrary"), vmem_limit_bytes=VMEM_LIMIT),
    )(*x_args, g1, w_br, w_gate, conv_w, conv_b, ln_g, ln_b, qag, kvag, wuq, wuqs, wuk, wuv,
      qng, qngs, kng, rc, rs1, rs2, vone, one_minus_lb, hgrn_ng, _hgrn_levels())


def _attn_kernel(q_ref, k_ref, v_ref, o_ref, s_ref, m_ref, acc_ref):
    i = pl.program_id(2)
    nt = (((1,), (1,)), ((), ()))
    heads = range(ATTN_HG)
    qs = [q_ref[0, h] for h in heads]
    odd = (i & 1) == 1

    def scores(h, tile0, ntiles):
        kt = k_ref[0, h, pl.ds(pl.multiple_of(tile0 * TQ, TQ), ntiles * TQ), :]
        return lax.dot_general(qs[h], kt, nt, preferred_element_type=F32)

    def fold(s):
        m = s[:, 0:LANE]
        for c in range(1, s.shape[1] // LANE):
            m = jnp.maximum(m, s[:, c * LANE:(c + 1) * LANE])
        return m

    def score_span(tile0, ntiles, m_in):
        out = []
        for h in heads:
            s = scores(h, tile0, ntiles)
            for t in range(ntiles):
                s_ref[h, tile0 + t] = s[:, t * TQ:(t + 1) * TQ]
            out.append(jnp.maximum(m_in[h], fold(s)))
        return tuple(out)

    def pv_span(tile0, ntiles):
        out = []
        for h in heads:
            s = jnp.concatenate([s_ref[h, tile0 + t] for t in range(ntiles)], axis=1)
            p = jnp.exp2(s - ms[h]).astype(BF16)
            vt = v_ref[0, h, pl.ds(pl.multiple_of(tile0 * TQ, TQ), ntiles * TQ), :]
            out.append(jnp.dot(p, vt, preferred_element_type=F32))
        return out

    quads = lax.shift_right_logical(i, 2)
    mloc = lax.fori_loop(0, quads, lambda t, m: score_span(4 * t, 4, m),
                         tuple(jnp.full((TQ, LANE), -1e30, F32) for _ in heads))
    for h in heads:
        m_ref[h] = mloc[h]

    @pl.when((i & 2) != 0)
    def _():
        m_new = score_span(4 * quads, 2, [m_ref[h] for h in heads])
        for h in heads:
            m_ref[h] = m_new[h]

    keep = (lax.broadcasted_iota(jnp.int32, (TQ, TQ), 0)
            >= lax.broadcasted_iota(jnp.int32, (TQ, TQ), 1))

    @pl.when(odd)
    def _():
        for h in heads:
            s = scores(h, i - 1, 2)
            lo, hi = s[:, :TQ], jnp.where(keep, s[:, TQ:], -1e30)
            s_ref[h, i - 1] = lo
            s_ref[h, i] = hi
            m_ref[h] = jnp.maximum(m_ref[h], jnp.maximum(fold(lo), fold(hi)))

    @pl.when(jnp.logical_not(odd))
    def _():
        for h in heads:
            s = jnp.where(keep, scores(h, i, 1), -1e30)
            s_ref[h, i] = s
            m_ref[h] = jnp.maximum(m_ref[h], fold(s))

    ms = [jnp.max(m_ref[h], axis=-1, keepdims=True) for h in heads]

    quads2 = lax.shift_right_logical(i + 1, 2)

    def pass2(t, acc):
        part = pv_span(4 * t, 4)
        return tuple(acc[h] + part[h] for h in heads)

    acc = lax.fori_loop(0, quads2, pass2, tuple(jnp.zeros((TQ, HEAD_SLOT), F32) for _ in heads))
    for h in heads:
        acc_ref[h] = acc[h]

    @pl.when(((i + 1) & 2) != 0)
    def _():
        part = pv_span(4 * quads2, 2)
        for h in heads:
            acc_ref[h] += part[h]

    @pl.when(jnp.logical_not(odd))
    def _():
        part = pv_span(i, 1)
        for h in heads:
            acc_ref[h] += part[h]

    first = lax.broadcasted_iota(jnp.int32, (TQ, HEAD_SLOT), 1) < V_DIM
    for hp in range(ATTN_HG // 2):
        a, b = acc_ref[2 * hp], acc_ref[2 * hp + 1]
        den = pltpu.roll(jnp.where(first, b, a), V_DIM, 1)
        o_ref[0, :, hp * HEAD_SLOT:(hp + 1) * HEAD_SLOT] = (
            jnp.where(first, a, b) / den).astype(o_ref.dtype)


def _attention(q, k, v):
    b, nh, lp, hs = q.shape
    nq = lp // TQ
    return pl.pallas_call(
        _attn_kernel,
        out_shape=jax.ShapeDtypeStruct((b, lp, nh * V_DIM), BF16),
        grid=(b, nh // ATTN_HG, nq),
        in_specs=[
            pl.BlockSpec((1, ATTN_HG, TQ, hs), lambda bb, g, i: (bb, g, i, 0)),
            pl.BlockSpec((1, ATTN_HG, lp, hs), lambda bb, g, i: (bb, g, 0, 0)),
            pl.BlockSpec((1, ATTN_HG, lp, hs), lambda bb, g, i: (bb, g, 0, 0)),
        ],
        out_specs=pl.BlockSpec((1, TQ, ATTN_HG * V_DIM), lambda bb, g, i: (bb, i, g)),
        scratch_shapes=[pltpu.VMEM((ATTN_HG, nq, TQ, TQ), F32),
                        pltpu.VMEM((ATTN_HG, TQ, LANE), F32),
                        pltpu.VMEM((ATTN_HG, TQ, HEAD_SLOT), F32)],
        compiler_params=pltpu.CompilerParams(
            dimension_semantics=("parallel", "parallel", "arbitrary"), vmem_limit_bytes=VMEM_LIMIT),
    )(q, k, v)


def _merge_kernel(ha_ref, ob_ref, oc_ref, gate_ref, *refs, seq_len=None, n_tiles=None):
    if seq_len is None:
        x_ref, wa_ref, wb_ref, wc_ref, wo_ref, o_ref = refs
        x_tile = x_ref[...]
    else:
        x_ref, meta_ref, wa_ref, wb_ref, wc_ref, wo_ref, o_ref = refs
        x_tile = _padded_rows(x_ref[...], meta_ref, pl.program_id(1), n_tiles, seq_len)
    mix = None
    for n, (br_ref, w_ref) in enumerate(((ha_ref, wa_ref), (ob_ref, wb_ref), (oc_ref, wc_ref))):
        y = jnp.dot(br_ref[...], w_ref[...], preferred_element_type=F32)
        per = D_MODEL // TN_IN
        gate = jnp.concatenate([gate_ref[n * per + t] for t in range(per)], axis=1)
        y = y * gate.astype(F32)
        mix = y if mix is None else mix + y
    o_ref[...] = x_tile + jnp.dot(mix.astype(BF16), wo_ref[...], preferred_element_type=F32)


def _merge(ha, ob, oc, gates, x2, wa, wb, wc, wo, li):
    t = x2.shape[0]
    br = lambda: pl.BlockSpec((TM_MERGE, 512), lambda i: (i, 0))
    wbr = lambda: pl.BlockSpec((None, 512, D_MODEL), lambda i: (li, 0, 0))
    return pl.pallas_call(
        _merge_kernel,
        out_shape=jax.ShapeDtypeStruct((t, D_MODEL), F32),
        grid=(t // TM_MERGE,),
        in_specs=[br(), br(), br(),
                  pl.BlockSpec((N_GATE_CHUNK, TM_MERGE, TN_IN), lambda i: (0, i, 0)),
                  pl.BlockSpec((TM_MERGE, D_MODEL), lambda i: (i, 0)),
                  wbr(), wbr(), wbr(),
                  pl.BlockSpec((None, D_MODEL, D_MODEL), lambda i: (li, 0, 0))],
        out_specs=pl.BlockSpec((TM_MERGE, D_MODEL), lambda i: (i, 0)),
        compiler_params=pltpu.CompilerParams(
            dimension_semantics=("parallel",), vmem_limit_bytes=VMEM_LIMIT),
    )(ha, ob, oc, gates, x2, wa, wb, wc, wo)


def _merge_first(ha, ob, oc, gates, x, meta, wa, wb, wc, wo, li, lp):
    b, seq_len, d = x.shape
    n_tiles = lp // TM_MERGE0
    rows = lambda bb, j: bb * n_tiles + j
    br = lambda: pl.BlockSpec((TM_MERGE0, 512), lambda bb, j: (rows(bb, j), 0))
    wbr = lambda: pl.BlockSpec((None, 512, D_MODEL), lambda bb, j: (li, 0, 0))
    return pl.pallas_call(
        functools.partial(_merge_kernel, seq_len=seq_len, n_tiles=n_tiles),
        out_shape=jax.ShapeDtypeStruct((b * lp, D_MODEL), F32),
        grid=(b, n_tiles),
        in_specs=[br(), br(), br(),
                  pl.BlockSpec((N_GATE_CHUNK, TM_MERGE0, TN_IN), lambda bb, j: (0, rows(bb, j), 0))]
        + _x_window_specs(TM_MERGE0, seq_len, d, lambda bb, j: (bb, j))
        + [wbr(), wbr(), wbr(),
           pl.BlockSpec((None, D_MODEL, D_MODEL), lambda bb, j: (li, 0, 0))],
        out_specs=pl.BlockSpec((TM_MERGE0, D_MODEL), lambda bb, j: (rows(bb, j), 0)),
        compiler_params=pltpu.CompilerParams(
            dimension_semantics=("parallel", "parallel"), vmem_limit_bytes=VMEM_LIMIT),
    )(ha, ob, oc, gates, x.reshape(b * seq_len, d), meta, wa, wb, wc, wo)


def _ffn_rows(x, g_ref, w1_ref, w2_ref):
    h = (_rms(x) * g_ref[...]).astype(BF16)
    acc = x
    for c in range(D_FF // TF_FF):
        cols = slice(c * TF_FF, (c + 1) * TF_FF)
        a = jnp.maximum(jnp.dot(h, w1_ref[:, cols], preferred_element_type=F32), 0.0)
        acc = acc + jnp.dot((a * a).astype(BF16), w2_ref[cols, :], preferred_element_type=F32)
    return acc


def _ffn_kernel(x_ref, g_ref, w1_ref, w2_ref, o_ref):
    o_ref[...] = _ffn_rows(x_ref[...], g_ref, w1_ref, w2_ref)


def _ffn_weight_specs(li):
    return [
        pl.BlockSpec((None, D_MODEL, D_FF), lambda *_: (li, 0, 0), pipeline_mode=pl.Buffered(1)),
        pl.BlockSpec((None, D_FF, D_MODEL), lambda *_: (li, 0, 0), pipeline_mode=pl.Buffered(1)),
    ]


def _ffn(x2, g, w1, w2, li):
    t = x2.shape[0]
    return pl.pallas_call(
        _ffn_kernel,
        out_shape=jax.ShapeDtypeStruct((t, D_MODEL), F32),
        grid=(t // TM_FF,),
        in_specs=[
            pl.BlockSpec((TM_FF, D_MODEL), lambda i: (i, 0)),
            pl.BlockSpec((1, D_MODEL), lambda i: (0, 0)),
        ] + _ffn_weight_specs(li),
        out_specs=pl.BlockSpec((TM_FF, D_MODEL), lambda i: (i, 0)),
        compiler_params=pltpu.CompilerParams(
            dimension_semantics=("parallel",), vmem_limit_bytes=VMEM_LIMIT),
    )(x2, g, w1, w2)


def _ffn_final(x2, g, w1, w2, li, b, lp, n_lead, n_out):
    assert lp % SUBLANE == 0 and n_lead % SUBLANE == 0
    return pl.pallas_call(
        _ffn_kernel,
        out_shape=jax.ShapeDtypeStruct((b, n_out, D_MODEL), F32),
        grid=(b, n_out // TM_FF),
        in_specs=[
            pl.BlockSpec((pl.Element(TM_FF), pl.Element(D_MODEL)),
                         lambda bb, t: (pl.multiple_of(bb * lp + n_lead + t * TM_FF, SUBLANE), 0)),
            pl.BlockSpec((1, D_MODEL), lambda bb, t: (0, 0)),
        ] + _ffn_weight_specs(li),
        out_specs=pl.BlockSpec((None, TM_FF, D_MODEL), lambda bb, t: (bb, t, 0)),
        compiler_params=pltpu.CompilerParams(
            dimension_semantics=("parallel", "parallel"), vmem_limit_bytes=VMEM_LIMIT),
    )(x2, g, w1, w2)


def _regroup_w_in(w):
    o = 0
    conv = w[..., o:o + 2 * CONV_DIM]; o += 2 * CONV_DIM
    cq = w[..., o:o + Q_RANK]; o += Q_RANK
    ckv = w[..., o:o + KV_RANK]; o += KV_RANK
    kr = w[..., o:o + ROPE_DIM]; o += ROPE_DIM
    hg = w[..., o:o + 4 * HGRN_HEADS * HGRN_DK]; o += 4 * HGRN_HEADS * HGRN_DK
    gate = w[..., o:]
    kr_slot = jnp.pad(kr, ((0, 0), (0, 0), (NOPE_DIM, LANE - QK_DIM)))
    w_br = jnp.concatenate([conv, cq, ckv, kr_slot, hg], axis=-1).astype(BF16)
    depth, d, _ = w.shape
    w_gate = gate.astype(BF16).reshape(depth, d, N_GATE_CHUNK, TN_IN).transpose(0, 2, 1, 3)
    return w_br, w_gate


def _head_slots(w, per_head, start, width):
    r = w.shape[0]
    wh = w.reshape(r, MLA_HEADS, per_head)[:, :, start:start + width]
    wh = jnp.pad(wh, ((0, 0), (0, 0), (0, HEAD_SLOT - width)))
    return wh.reshape(r, MLA_HEADS * HEAD_SLOT)


def _swap_rope_halves(a):
    half = ROPE_DIM // 2
    return jnp.concatenate([jnp.zeros_like(a[..., :NOPE_DIM]), a[..., NOPE_DIM + half:],
                            a[..., NOPE_DIM:NOPE_DIM + half]], axis=-1)


def _v_slots(w_ukv):
    r = w_ukv.shape[0]
    wv = w_ukv.reshape(r, MLA_HEADS // 2, 2, NOPE_DIM + V_DIM)[..., NOPE_DIM:]
    z = jnp.zeros_like(wv[:, :, 0])
    slots = jnp.stack([jnp.concatenate([wv[:, :, 0], z], axis=-1),
                       jnp.concatenate([z, wv[:, :, 1]], axis=-1)], axis=2)
    ones = jnp.ones((MLA_HEADS // 2, V_DIM), F32)
    zo = jnp.zeros_like(ones)
    vone = jnp.stack([jnp.concatenate([zo, ones], axis=-1),
                      jnp.concatenate([ones, zo], axis=-1)], axis=1)
    return slots.reshape(r, MLA_HEADS * HEAD_SLOT), vone.reshape(1, MLA_HEADS * HEAD_SLOT)


def _rope_tables(lp):
    half = ROPE_DIM // 2
    pos = jnp.arange(lp, dtype=F32)
    inv_freq = ROPE_BASE ** (-jnp.arange(half, dtype=F32) / half)
    ang = pos[:, None] * inv_freq[None, :]
    cos, sin = jnp.cos(ang), jnp.sin(ang)
    ones = jnp.ones((lp, NOPE_DIM), F32)
    z16 = jnp.zeros((lp, half), F32)
    z64 = jnp.zeros((lp, NOPE_DIM), F32)
    tail = jnp.zeros((lp, LANE - QK_DIM), F32)
    rc = jnp.concatenate([ones, cos, cos, tail], axis=1)
    rs1 = jnp.concatenate([z64, -sin, z16, tail], axis=1)
    rs2 = jnp.concatenate([z64, z16, sin, tail], axis=1)
    return rc, rs1, rs2


def kernel(x, meta, norm1_g, w_in, conv_w, conv_b, conv_ln_g, conv_ln_b, w_conv_out, q_a_norm_g, w_uq, kv_a_norm_g, w_ukv, q_norm_g, k_norm_g, w_attn_out, hgrn_lb_logits, hgrn_norm_g, w_hgrn_out, w_out, norm2_g, w_ff1, w_ff2):
    b, seq, d = x.shape
    depth = w_in.shape[0]
    l = seq + N_META
    lp = -(-l // TL) * TL
    assert lp % TM_MERGE0 == 0 and (b * lp) % TM_MERGE == 0 and lp - l < TL
    x2 = None

    rc, rs1, rs2 = _rope_tables(lp)
    p_lb = jax.nn.softmax(hgrn_lb_logits.astype(F32), axis=0)
    lower_bounds = jnp.cumsum(p_lb, axis=0) - p_lb[0:1]
    row = lambda a: a.astype(F32).reshape(1, -1)
    pad_qk = lambda g: jnp.pad(g.astype(F32), (0, LANE - QK_DIM)).reshape(1, LANE)
    w_a, w_b, w_c, w_o = (w.astype(BF16) for w in (w_conv_out, w_attn_out, w_hgrn_out, w_out))
    w_1, w_2 = w_ff1.astype(BF16), w_ff2.astype(BF16)
    w_br, w_gate = _regroup_w_in(w_in)

    for li in range(depth):
        wuv, vone = _v_slots(w_ukv[li])
        w_uq_heads = w_uq[li].reshape(Q_RANK, MLA_HEADS, QK_DIM)
        w_uq_swapped = _swap_rope_halves(w_uq_heads).reshape(Q_RANK, MLA_HEADS * QK_DIM)
        q_gain = q_norm_g[li].astype(F32) * (QK_DIM ** -0.5 * LOG2E)
        gates, ha, q, k, v, oc = _front(
            (x, meta) if li == 0 else x2.reshape(b, lp, d), row(norm1_g[li]), w_br, w_gate, li,
            jnp.repeat(conv_w[li].astype(F32), SUBLANE, axis=0),
            row(conv_b[li]), row(conv_ln_g[li]), row(conv_ln_b[li]),
            row(q_a_norm_g[li]), row(kv_a_norm_g[li]),
            _head_slots(w_uq[li], QK_DIM, 0, QK_DIM).astype(BF16),
            _head_slots(w_uq_swapped, QK_DIM, 0, QK_DIM).astype(BF16),
            _head_slots(w_ukv[li], NOPE_DIM + V_DIM, 0, NOPE_DIM).astype(BF16),
            wuv.astype(BF16),
            pad_qk(q_gain), pad_qk(_swap_rope_halves(q_gain)), pad_qk(k_norm_g[li]),
            rc, rs1, rs2, vone,
            (1.0 - lower_bounds[li]).reshape(HGRN_HEADS, 1, HGRN_DK),
            hgrn_norm_g[li].astype(F32).reshape(HGRN_HEADS, 1, HGRN_DV), lp)
        ob = _attention(q, k, v)
        flat = lambda a: a.reshape(b * lp, -1)
        branches = (flat(ha), flat(ob), flat(oc), gates.reshape(N_GATE_CHUNK, b * lp, TN_IN))
        if li == 0:
            x2 = _merge_first(*branches, x, meta.astype(x.dtype), w_a, w_b, w_c, w_o, li, lp)
        else:
            x2 = _merge(*branches, x2, w_a, w_b, w_c, w_o, li)
        if li + 1 < depth:
            x2 = _ffn(x2, row(norm2_g[li]), w_1, w_2, li)
    assert seq % TM_FF == 0
    return _ffn_final(x2, row(norm2_g[depth - 1]), w_1, w_2, depth - 1, b, lp, N_META, seq)
```

```python
import functools

import jax
import jax.numpy as jnp
import numpy as np
from jax import lax
from jax.experimental import pallas as pl
from jax.experimental.pallas import tpu as pltpu

F32 = jnp.float32
BF16 = jnp.bfloat16

D_MODEL = 1024
N_META = 16
EPS = 1e-6
GATE_CLAMP = 1.0 - 1e-6
CONV_DIM = 512
CONV_K = 31
MLA_HEADS = 8
Q_RANK = 256
KV_RANK = 128
NOPE_DIM = 64
ROPE_DIM = 32
V_DIM = 64
QK_DIM = NOPE_DIM + ROPE_DIM
ROPE_BASE = 10000.0
HGRN_HEADS = 4
HGRN_DK = 128
HGRN_DV = 128
D_FF = 4096

LANE = 128
HEAD_SLOT = LANE

COL_CONV = 3 * D_MODEL
COL_CQ = COL_CONV + 2 * CONV_DIM
COL_HGRN = COL_CQ + Q_RANK + KV_RANK + LANE
N_IN_PAD = COL_HGRN + 4 * HGRN_HEADS * HGRN_DK

VMEM_LIMIT = 52 * 1024 * 1024

TL = 384
TN_IN = 512
N_GATE_CHUNK = COL_CONV // TN_IN
SUBLANE = 8
RC_CONV = 16
CONV_HALO = 32
CONV_SPAN = TL + CONV_HALO - SUBLANE
RC_PREP = 128
TQ = 384
ATTN_HG = 4
LOG2E = 1.4426950408889634
HGRN_C = 128
HGRN_BLK = 4
TM_MERGE = 1024
TM_MERGE0 = 704
TM_FF = 1024
TF_FF = 1024


def _rms(x, eps=EPS):
    return x * lax.rsqrt(jnp.mean(x * x, axis=-1, keepdims=True) + eps)


def _sigmoid(x):
    return 0.5 * jnp.tanh(0.5 * x) + 0.5


def _silu(x):
    return x * _sigmoid(x)


def _conv_stage(glu, j, hbuf, sbuf):
    @pl.when(j == 0)
    def _():
        hbuf[0:CONV_HALO, :] = jnp.zeros((CONV_HALO, CONV_DIM), F32)

    @pl.when(j > 0)
    def _():
        hbuf[0:CONV_HALO, :] = hbuf[TL:TL + CONV_HALO, :]

    hbuf[CONV_HALO:CONV_HALO + TL, :] = glu

    for r in range(1, SUBLANE):
        sbuf[r - 1] = hbuf[r:r + CONV_SPAN, :]


def _conv_rows(r0, w_ref, cb_ref, lg_ref, lb_ref, o_ref, hbuf, sbuf):
    base = CONV_HALO - (CONV_K - 1)
    acc = jnp.zeros((RC_CONV, CONV_DIM), F32) + cb_ref[...]
    for k in range(CONV_K):
        r = (base + k) % SUBLANE
        lo = pl.multiple_of(r0 + (base + k - r), SUBLANE)
        src = hbuf if r == 0 else sbuf.at[r - 1]
        wk = jnp.tile(w_ref[k * SUBLANE:(k + 1) * SUBLANE, :], (RC_CONV // SUBLANE, 1))
        acc = acc + wk * src[pl.ds(lo, RC_CONV), :]
    mu = jnp.mean(acc, axis=-1, keepdims=True)
    xc = acc - mu
    y = xc * lax.rsqrt(jnp.mean(xc * xc, axis=-1, keepdims=True) + EPS)
    y = y * lg_ref[...] + lb_ref[...]
    o_ref[0, pl.ds(r0, RC_CONV), :] = (y * jax.nn.sigmoid(y)).astype(o_ref.dtype)


def _rope(x, c, s1, s2):
    half = ROPE_DIM // 2
    return x * c + pltpu.roll(x, LANE - half, 1) * s1 + pltpu.roll(x, half, 1) * s2


def _lane_sumsq(x, ones):
    sq = x * x
    hi = sq.astype(BF16)
    lo = (sq - hi.astype(F32)).astype(BF16)
    return jnp.dot(jnp.concatenate([hi, lo], axis=1), ones, preferred_element_type=F32)


def _mla_rows(cq, ckv, kr, rows, qag_ref, kvag_ref, wuq_ref, wuqs_ref, wuk_ref, wuv_ref,
              qng_ref, qngs_ref, kng_ref, rc_ref, rs1_ref, rs2_ref, vone_ref, q_out, k_out, v_out):
    inv_d = 1.0 / QK_DIM
    ones = jnp.ones((2 * LANE, LANE), BF16)
    cq = (_rms(cq) * qag_ref[...]).astype(BF16)
    q = jnp.dot(cq, wuq_ref[...], preferred_element_type=F32)
    qs = jnp.dot(cq, wuqs_ref[...], preferred_element_type=F32)
    ckv = (_rms(ckv) * kvag_ref[...]).astype(BF16)
    kn = jnp.dot(ckv, wuk_ref[...], preferred_element_type=F32)
    v = jnp.dot(ckv, wuv_ref[...], preferred_element_type=F32) + vone_ref[...]
    rc, rs1, rs2 = rc_ref[rows, :], rs1_ref[rows, :], rs2_ref[rows, :]
    q_c = qng_ref[...] * rc
    q_s = qngs_ref[...] * (rs1 + rs2)
    kr_ss = _lane_sumsq(kr, ones)
    kr_rot = _rope(kr * kng_ref[...], rc, rs1, rs2)
    for h in range(MLA_HEADS):
        sl = slice(h * HEAD_SLOT, (h + 1) * HEAD_SLOT)
        qh = q[:, sl]
        q_out[0, h, rows, :] = ((qh * q_c + qs[:, sl] * q_s) * lax.rsqrt(
            _lane_sumsq(qh, ones) * inv_d + EPS)).astype(q_out.dtype)
        kh = kn[:, sl]
        k_out[0, h, rows, :] = ((kh * kng_ref[...] + kr_rot) * lax.rsqrt(
            (_lane_sumsq(kh, ones) + kr_ss) * inv_d + EPS)).astype(k_out.dtype)
        v_out[0, h, rows, :] = v[:, sl].astype(v_out.dtype)


def _hgrn_head(qh, fr, iv, gv, oml, ng, lvl, st_ref, h):
    C = HGRN_C
    rowid = lax.broadcasted_iota(jnp.int32, (C, HGRN_DK), 0)
    nt = (((1,), (1,)), ((), ()))
    tn = (((0,), (0,)), ((), ()))
    kk = oml * jax.nn.sigmoid(-fr)
    lf = jnp.log1p(-jnp.minimum(kk, GATE_CLAMP)) * LOG2E
    v = _silu(iv).astype(BF16)

    fwd = lf
    bwd = jnp.zeros_like(lf)
    n, level = 1, 0
    a = None
    while n < C:
        if n >= HGRN_BLK:
            qs = (qh * jnp.exp2(fwd)).astype(BF16)
            if a is None:
                a = jnp.where(lvl == 0, lax.dot_general(
                    qs, (kk * jnp.exp2(-fwd)).astype(BF16), nt,
                    preferred_element_type=F32), 0.0)
            level += 1
            ks = (kk * jnp.exp2(bwd)).astype(BF16)
            a = jnp.where(lvl == level,
                          lax.dot_general(qs, ks, nt, preferred_element_type=F32), a)
        tot = fwd + bwd
        upper = (rowid & n) != 0
        fwd = fwd + jnp.where(upper, pltpu.roll(tot, n, 0), 0.0)
        bwd = bwd + jnp.where(upper, 0.0, pltpu.roll(tot, C - n, 0))
        n *= 2

    st = st_ref[h]
    o = jnp.dot(a.astype(BF16), v, preferred_element_type=F32)
    o = o + lax.dot_general((qh * jnp.exp2(fwd)).astype(BF16), st.astype(BF16), nt,
                            preferred_element_type=F32)
    ks = (kk * jnp.exp2(bwd)).astype(BF16)
    st_ref[h] = st * jnp.exp2(fwd[C - 1:C, :]) + lax.dot_general(
        v, ks, tn, preferred_element_type=F32)

    return _rms(o) * ng * _silu(gv)


def _hgrn_levels():
    t = np.arange(HGRN_C)[:, None]
    s = np.arange(HGRN_C)[None, :]
    x = (t ^ s) // HGRN_BLK
    lvl = np.where(x == 0, 0, np.floor(np.log2(np.maximum(x, 1))).astype(np.int64) + 1)
    return jnp.asarray(np.where(s > t, -1, lvl), dtype=jnp.int32)


def _window_start(j, tile, seq_len):
    return jnp.clip(j * tile - N_META, 0, seq_len - tile)


def _padded_rows(win, meta_ref, j, n_tiles, seq_len):
    tile, d = win.shape
    drop = tile * (n_tiles - 1) - N_META - (seq_len - tile)
    first = jnp.concatenate([meta_ref[...].astype(win.dtype), win[:tile - N_META]], axis=0)
    last = jnp.concatenate([win[drop:], jnp.zeros((drop, d), win.dtype)], axis=0)
    return jnp.where(j == 0, first, jnp.where(j == n_tiles - 1, last, win))


def _front_kernel(*refs, seq_len=None, n_tiles=None):
    if seq_len is None:
        x_ref, *rest = refs
        x_tile = x_ref[0]
    else:
        x_ref, meta_ref, *rest = refs
        x_tile = _padded_rows(x_ref[...], meta_ref, pl.program_id(1), n_tiles, seq_len)
    _front_tile(x_tile, *rest)


def _front_tile(x_tile, g1_ref, wbr_ref, wgate_ref,
                cw_ref, cb_ref, lg_ref, lb_ref,
                qag_ref, kvag_ref, wuq_ref, wuqs_ref, wuk_ref, wuv_ref, qng_ref, qngs_ref, kng_ref,
                rc_ref, rs1_ref, rs2_ref, vone_ref,
                oml_ref, ng_ref, lvl_ref,
                gate_out, ha_out, q_out, k_out, v_out, oc_out,
                h_scr, ym_scr, yh_scr, hbuf, sbuf, st_ref):
    j = pl.program_id(1)
    hw = HGRN_HEADS * HGRN_DK
    n_mla = COL_HGRN - COL_CQ

    @pl.when(j == 0)
    def _():
        st_ref[...] = jnp.zeros(st_ref.shape, F32)

    h = (_rms(x_tile) * g1_ref[...]).astype(BF16)
    h_scr[...] = h
    yc = jnp.dot(h, wbr_ref[:, 0:2 * CONV_DIM], preferred_element_type=F32)
    _conv_stage(yc[:, :CONV_DIM] * _sigmoid(yc[:, CONV_DIM:]), j, hbuf, sbuf)
    ym_scr[...] = jnp.dot(h, wbr_ref[:, 2 * CONV_DIM:2 * CONV_DIM + n_mla],
                          preferred_element_type=F32)
    yh = jnp.dot(h, wbr_ref[:, 2 * CONV_DIM + n_mla:], preferred_element_type=F32)
    for idx in range(4 * HGRN_HEADS):
        yh_scr[idx] = yh[:, idx * HGRN_DK:(idx + 1) * HGRN_DK]

    conv_per_gate = (TL // RC_CONV) // N_GATE_CHUNK

    def gate_and_conv(c, carry):
        gate_out[c, 0] = _sigmoid(jnp.dot(h_scr[...], wgate_ref[c],
                                          preferred_element_type=F32)).astype(gate_out.dtype)
        for t in range(conv_per_gate):
            r0 = pl.multiple_of((c * conv_per_gate + t) * RC_CONV, RC_CONV)
            _conv_rows(r0, cw_ref, cb_ref, lg_ref, lb_ref, ha_out, hbuf, sbuf)
        return carry

    for c in range(N_GATE_CHUNK):
        gate_and_conv(c, 0)

    def mla_chunk(c, carry):
        rows = pl.ds(pl.multiple_of(c * RC_PREP, RC_PREP), RC_PREP)
        _mla_rows(ym_scr[rows, 0:Q_RANK], ym_scr[rows, Q_RANK:Q_RANK + KV_RANK],
                  ym_scr[rows, Q_RANK + KV_RANK:], rows,
                  qag_ref, kvag_ref, wuq_ref, wuqs_ref, wuk_ref, wuv_ref,
                  qng_ref, qngs_ref, kng_ref, rc_ref, rs1_ref, rs2_ref, vone_ref, q_out, k_out, v_out)
        return carry

    for c in range(TL // RC_PREP):
        mla_chunk(c, 0)

    lvl = lvl_ref[...]

    def hgrn_tile(t, carry):
        for cc in range(TL // HGRN_C):
            rows = slice(cc * HGRN_C, (cc + 1) * HGRN_C)
            for hd in range(HGRN_HEADS):
                oc_out[0, rows, hd * HGRN_DV:(hd + 1) * HGRN_DV] = _hgrn_head(
                    yh_scr[hd, rows, :], yh_scr[HGRN_HEADS + hd, rows, :],
                    yh_scr[2 * HGRN_HEADS + hd, rows, :], yh_scr[3 * HGRN_HEADS + hd, rows, :],
                    oml_ref[hd], ng_ref[hd], lvl, st_ref, hd).astype(oc_out.dtype)
        return carry

    lax.fori_loop(0, jnp.minimum(j + 1, 1), hgrn_tile, 0)


def _x_window_specs(tile, seq_len, d, index_of):
    assert seq_len % SUBLANE == 0 and tile % SUBLANE == 0 and N_META % SUBLANE == 0
    return [
        pl.BlockSpec((pl.Element(tile), pl.Element(d)),
                     lambda *ids: (pl.multiple_of(
                         index_of(*ids)[0] * seq_len + _window_start(index_of(*ids)[1], tile, seq_len),
                         SUBLANE), 0)),
        pl.BlockSpec((N_META, d), lambda *ids: (0, 0)),
    ]


def _front(x_in, g1, w_br, w_gate, li, conv_w, conv_b, ln_g, ln_b, qag, kvag, wuq, wuqs, wuk, wuv,
           qng, qngs, kng, rc, rs1, rs2, vone, one_minus_lb, hgrn_ng, lp):
    first_layer = isinstance(x_in, tuple)
    b, seq_len, d = x_in[0].shape if first_layer else x_in.shape
    hw = MLA_HEADS * HEAD_SLOT
    gw = HGRN_HEADS * HGRN_DK
    n_br = N_IN_PAD - COL_CONV
    const = lambda shape: pl.BlockSpec(shape, lambda bb, j: (0,) * len(shape))
    resident = lambda shape: pl.BlockSpec((None,) + shape, lambda bb, j: (li,) + (0,) * len(shape),
                                          pipeline_mode=pl.Buffered(1))
    tab = pl.BlockSpec((TL, LANE), lambda bb, j: (j, 0))
    seq = lambda w: pl.BlockSpec((1, TL, w), lambda bb, j: (bb, j, 0))
    heads = pl.BlockSpec((1, MLA_HEADS, TL, HEAD_SLOT), lambda bb, j: (bb, 0, j, 0))
    qkv = jax.ShapeDtypeStruct((b, MLA_HEADS, lp, HEAD_SLOT), BF16)
    if first_layer:
        x_args = (x_in[0].reshape(b * seq_len, d), x_in[1])
        x_specs = _x_window_specs(TL, seq_len, d, lambda bb, j: (bb, j))
        body = functools.partial(_front_kernel, seq_len=seq_len, n_tiles=lp // TL)
    else:
        x_args, x_specs, body = (x_in,), [seq(d)], _front_kernel
    return pl.pallas_call(
        body,
        out_shape=(jax.ShapeDtypeStruct((N_GATE_CHUNK, b, lp, TN_IN), BF16),
                   jax.ShapeDtypeStruct((b, lp, CONV_DIM), BF16),
                   qkv, qkv, qkv,
                   jax.ShapeDtypeStruct((b, lp, gw), BF16)),
        grid=(b, lp // TL),
        in_specs=x_specs + [
            const((1, d)), resident((d, n_br)), resident((N_GATE_CHUNK, d, TN_IN)),
            const((CONV_K * SUBLANE, CONV_DIM)), const((1, CONV_DIM)), const((1, CONV_DIM)),
            const((1, CONV_DIM)),
            const((1, Q_RANK)), const((1, KV_RANK)),
            const((Q_RANK, hw)), const((Q_RANK, hw)), const((KV_RANK, hw)), const((KV_RANK, hw)),
            const((1, LANE)), const((1, LANE)), const((1, LANE)),
            tab, tab, tab, const((1, hw)),
            const((HGRN_HEADS, 1, HGRN_DK)), const((HGRN_HEADS, 1, HGRN_DV)), const((HGRN_C, HGRN_C)),
        ],
        out_specs=(pl.BlockSpec((N_GATE_CHUNK, 1, TL, TN_IN), lambda bb, j: (0, bb, j, 0)),
                   seq(CONV_DIM), heads, heads, heads, seq(gw)),
        scratch_shapes=[pltpu.VMEM((TL, d), BF16),
                        pltpu.VMEM((TL, COL_HGRN - COL_CQ), F32),
                        pltpu.VMEM((4 * HGRN_HEADS, TL, HGRN_DK), F32),
                        pltpu.VMEM((CONV_HALO + TL, CONV_DIM), F32),
                        pltpu.VMEM((SUBLANE - 1, CONV_SPAN, CONV_DIM), F32),
                        pltpu.VMEM((HGRN_HEADS, HGRN_DV, HGRN_DK), F32)],
        compiler_params=pltpu.CompilerParams(
            dimension_semantics=("parallel", "arbitrary"), vmem_limit_bytes=VMEM_LIMIT),
    )(*x_args, g1, w_br, w_gate, conv_w, conv_b, ln_g, ln_b, qag, kvag, wuq, wuqs, wuk, wuv,
      qng, qngs, kng, rc, rs1, rs2, vone, one_minus_lb, hgrn_ng, _hgrn_levels())


def _attn_kernel(q_ref, k_ref, v_ref, o_ref, s_ref, m_ref, acc_ref):
    i = pl.program_id(2)
    nt = (((1,), (1,)), ((), ()))
    heads = range(ATTN_HG)
    qs = [q_ref[0, h] for h in heads]
    odd = (i & 1) == 1

    def scores(h, tile0, ntiles):
        kt = k_ref[0, h, pl.ds(pl.multiple_of(tile0 * TQ, TQ), ntiles * TQ), :]
        return lax.dot_general(qs[h], kt, nt, preferred_element_type=F32)

    def fold(s):
        m = s[:, 0:LANE]
        for c in range(1, s.shape[1] // LANE):
            m = jnp.maximum(m, s[:, c * LANE:(c + 1) * LANE])
        return m

    def score_span(tile0, ntiles, m_in):
        out = []
        for h in heads:
            s = scores(h, tile0, ntiles)
            for t in range(ntiles):
                s_ref[h, tile0 + t] = s[:, t * TQ:(t + 1) * TQ]
            out.append(jnp.maximum(m_in[h], fold(s)))
        return tuple(out)

    def pv_span(tile0, ntiles):
        out = []
        for h in heads:
            s = jnp.concatenate([s_ref[h, tile0 + t] for t in range(ntiles)], axis=1)
            p = jnp.exp2(s - ms[h]).astype(BF16)
            vt = v_ref[0, h, pl.ds(pl.multiple_of(tile0 * TQ, TQ), ntiles * TQ), :]
            out.append(jnp.dot(p, vt, preferred_element_type=F32))
        return out

    quads = lax.shift_right_logical(i, 2)
    mloc = lax.fori_loop(0, quads, lambda t, m: score_span(4 * t, 4, m),
                         tuple(jnp.full((TQ, LANE), -1e30, F32) for _ in heads))
    for h in heads:
        m_ref[h] = mloc[h]

    @pl.when((i & 2) != 0)
    def _():
        m_new = score_span(4 * quads, 2, [m_ref[h] for h in heads])
        for h in heads:
            m_ref[h] = m_new[h]

    keep = (lax.broadcasted_iota(jnp.int32, (TQ, TQ), 0)
            >= lax.broadcasted_iota(jnp.int32, (TQ, TQ), 1))

    @pl.when(odd)
    def _():
        for h in heads:
            s = scores(h, i - 1, 2)
            lo, hi = s[:, :TQ], jnp.where(keep, s[:, TQ:], -1e30)
            s_ref[h, i - 1] = lo
            s_ref[h, i] = hi
            m_ref[h] = jnp.maximum(m_ref[h], jnp.maximum(fold(lo), fold(hi)))

    @pl.when(jnp.logical_not(odd))
    def _():
        for h in heads:
            s = jnp.where(keep, scores(h, i, 1), -1e30)
            s_ref[h, i] = s
            m_ref[h] = jnp.maximum(m_ref[h], fold(s))

    ms = [jnp.max(m_ref[h], axis=-1, keepdims=True) for h in heads]

    quads2 = lax.shift_right_logical(i + 1, 2)

    def pass2(t, acc):
        part = pv_span(4 * t, 4)
        return tuple(acc[h] + part[h] for h in heads)

    acc = lax.fori_loop(0, quads2, pass2, tuple(jnp.zeros((TQ, HEAD_SLOT), F32) for _ in heads))
    for h in heads:
        acc_ref[h] = acc[h]

    @pl.when(((i + 1) & 2) != 0)
    def _():
        part = pv_span(4 * quads2, 2)
        for h in heads:
            acc_ref[h] += part[h]

    @pl.when(jnp.logical_not(odd))
    def _():
        part = pv_span(i, 1)
        for h in heads:
            acc_ref[h] += part[h]

    first = lax.broadcasted_iota(jnp.int32, (TQ, HEAD_SLOT), 1) < V_DIM
    for hp in range(ATTN_HG // 2):
        a, b = acc_ref[2 * hp], acc_ref[2 * hp + 1]
        den = pltpu.roll(jnp.where(first, b, a), V_DIM, 1)
        o_ref[0, :, hp * HEAD_SLOT:(hp + 1) * HEAD_SLOT] = (
            jnp.where(first, a, b) / den).astype(o_ref.dtype)


def _attention(q, k, v):
    b, nh, lp, hs = q.shape
    nq = lp // TQ
    return pl.pallas_call(
        _attn_kernel,
        out_shape=jax.ShapeDtypeStruct((b, lp, nh * V_DIM), BF16),
        grid=(b, nh // ATTN_HG, nq),
        in_specs=[
            pl.BlockSpec((1, ATTN_HG, TQ, hs), lambda bb, g, i: (bb, g, i, 0)),
            pl.BlockSpec((1, ATTN_HG, lp, hs), lambda bb, g, i: (bb, g, 0, 0)),
            pl.BlockSpec((1, ATTN_HG, lp, hs), lambda bb, g, i: (bb, g, 0, 0)),
        ],
        out_specs=pl.BlockSpec((1, TQ, ATTN_HG * V_DIM), lambda bb, g, i: (bb, i, g)),
        scratch_shapes=[pltpu.VMEM((ATTN_HG, nq, TQ, TQ), F32),
                        pltpu.VMEM((ATTN_HG, TQ, LANE), F32),
                        pltpu.VMEM((ATTN_HG, TQ, HEAD_SLOT), F32)],
        compiler_params=pltpu.CompilerParams(
            dimension_semantics=("parallel", "parallel", "arbitrary"), vmem_limit_bytes=VMEM_LIMIT),
    )(q, k, v)


def _merge_kernel(ha_ref, ob_ref, oc_ref, gate_ref, *refs, seq_len=None, n_tiles=None):
    if seq_len is None:
        x_ref, wa_ref, wb_ref, wc_ref, wo_ref, o_ref = refs
        x_tile = x_ref[...]
    else:
        x_ref, meta_ref, wa_ref, wb_ref, wc_ref, wo_ref, o_ref = refs
        x_tile = _padded_rows(x_ref[...], meta_ref, pl.program_id(1), n_tiles, seq_len)
    mix = None
    for n, (br_ref, w_ref) in enumerate(((ha_ref, wa_ref), (ob_ref, wb_ref), (oc_ref, wc_ref))):
        y = jnp.dot(br_ref[...], w_ref[...], preferred_element_type=F32)
        per = D_MODEL // TN_IN
        gate = jnp.concatenate([gate_ref[n * per + t] for t in range(per)], axis=1)
        y = y * gate.astype(F32)
        mix = y if mix is None else mix + y
    o_ref[...] = x_tile + jnp.dot(mix.astype(BF16), wo_ref[...], preferred_element_type=F32)


def _merge(ha, ob, oc, gates, x2, wa, wb, wc, wo, li):
    t = x2.shape[0]
    br = lambda: pl.BlockSpec((TM_MERGE, 512), lambda i: (i, 0))
    wbr = lambda: pl.BlockSpec((None, 512, D_MODEL), lambda i: (li, 0, 0))
    return pl.pallas_call(
        _merge_kernel,
        out_shape=jax.ShapeDtypeStruct((t, D_MODEL), F32),
        grid=(t // TM_MERGE,),
        in_specs=[br(), br(), br(),
                  pl.BlockSpec((N_GATE_CHUNK, TM_MERGE, TN_IN), lambda i: (0, i, 0)),
                  pl.BlockSpec((TM_MERGE, D_MODEL), lambda i: (i, 0)),
                  wbr(), wbr(), wbr(),
                  pl.BlockSpec((None, D_MODEL, D_MODEL), lambda i: (li, 0, 0))],
        out_specs=pl.BlockSpec((TM_MERGE, D_MODEL), lambda i: (i, 0)),
        compiler_params=pltpu.CompilerParams(
            dimension_semantics=("parallel",), vmem_limit_bytes=VMEM_LIMIT),
    )(ha, ob, oc, gates, x2, wa, wb, wc, wo)


def _merge_first(ha, ob, oc, gates, x, meta, wa, wb, wc, wo, li, lp):
    b, seq_len, d = x.shape
    n_tiles = lp // TM_MERGE0
    rows = lambda bb, j: bb * n_tiles + j
    br = lambda: pl.BlockSpec((TM_MERGE0, 512), lambda bb, j: (rows(bb, j), 0))
    wbr = lambda: pl.BlockSpec((None, 512, D_MODEL), lambda bb, j: (li, 0, 0))
    return pl.pallas_call(
        functools.partial(_merge_kernel, seq_len=seq_len, n_tiles=n_tiles),
        out_shape=jax.ShapeDtypeStruct((b * lp, D_MODEL), F32),
        grid=(b, n_tiles),
        in_specs=[br(), br(), br(),
                  pl.BlockSpec((N_GATE_CHUNK, TM_MERGE0, TN_IN), lambda bb, j: (0, rows(bb, j), 0))]
        + _x_window_specs(TM_MERGE0, seq_len, d, lambda bb, j: (bb, j))
        + [wbr(), wbr(), wbr(),
           pl.BlockSpec((None, D_MODEL, D_MODEL), lambda bb, j: (li, 0, 0))],
        out_specs=pl.BlockSpec((TM_MERGE0, D_MODEL), lambda bb, j: (rows(bb, j), 0)),
        compiler_params=pltpu.CompilerParams(
            dimension_semantics=("parallel", "parallel"), vmem_limit_bytes=VMEM_LIMIT),
    )(ha, ob, oc, gates, x.reshape(b * seq_len, d), meta, wa, wb, wc, wo)


def _ffn_rows(x, g_ref, w1_ref, w2_ref):
    h = (_rms(x) * g_ref[...]).astype(BF16)
    acc = x
    for c in range(D_FF // TF_FF):
        cols = slice(c * TF_FF, (c + 1) * TF_FF)
        a = jnp.maximum(jnp.dot(h, w1_ref[:, cols], preferred_element_type=F32), 0.0)
        acc = acc + jnp.dot((a * a).astype(BF16), w2_ref[cols, :], preferred_element_type=F32)
    return acc


def _ffn_kernel(x_ref, g_ref, w1_ref, w2_ref, o_ref):
    o_ref[...] = _ffn_rows(x_ref[...], g_ref, w1_ref, w2_ref)


def _ffn_weight_specs(li):
    return [
        pl.BlockSpec((None, D_MODEL, D_FF), lambda *_: (li, 0, 0), pipeline_mode=pl.Buffered(1)),
        pl.BlockSpec((None, D_FF, D_MODEL), lambda *_: (li, 0, 0), pipeline_mode=pl.Buffered(1)),
    ]


def _ffn(x2, g, w1, w2, li):
    t = x2.shape[0]
    return pl.pallas_call(
        _ffn_kernel,
        out_shape=jax.ShapeDtypeStruct((t, D_MODEL), F32),
        grid=(t // TM_FF,),
        in_specs=[
            pl.BlockSpec((TM_FF, D_MODEL), lambda i: (i, 0)),
            pl.BlockSpec((1, D_MODEL), lambda i: (0, 0)),
        ] + _ffn_weight_specs(li),
        out_specs=pl.BlockSpec((TM_FF, D_MODEL), lambda i: (i, 0)),
        compiler_params=pltpu.CompilerParams(
            dimension_semantics=("parallel",), vmem_limit_bytes=VMEM_LIMIT),
    )(x2, g, w1, w2)


def _ffn_final(x2, g, w1, w2, li, b, lp, n_lead, n_out):
    assert lp % SUBLANE == 0 and n_lead % SUBLANE == 0
    return pl.pallas_call(
        _ffn_kernel,
        out_shape=jax.ShapeDtypeStruct((b, n_out, D_MODEL), F32),
        grid=(b, n_out // TM_FF),
        in_specs=[
            pl.BlockSpec((pl.Element(TM_FF), pl.Element(D_MODEL)),
                         lambda bb, t: (pl.multiple_of(bb * lp + n_lead + t * TM_FF, SUBLANE), 0)),
            pl.BlockSpec((1, D_MODEL), lambda bb, t: (0, 0)),
        ] + _ffn_weight_specs(li),
        out_specs=pl.BlockSpec((None, TM_FF, D_MODEL), lambda bb, t: (bb, t, 0)),
        compiler_params=pltpu.CompilerParams(
            dimension_semantics=("parallel", "parallel"), vmem_limit_bytes=VMEM_LIMIT),
    )(x2, g, w1, w2)


def _regroup_w_in(w):
    o = 0
    conv = w[..., o:o + 2 * CONV_DIM]; o += 2 * CONV_DIM
    cq = w[..., o:o + Q_RANK]; o += Q_RANK
    ckv = w[..., o:o + KV_RANK]; o += KV_RANK
    kr = w[..., o:o + ROPE_DIM]; o += ROPE_DIM
    hg = w[..., o:o + 4 * HGRN_HEADS * HGRN_DK]; o += 4 * HGRN_HEADS * HGRN_DK
    gate = w[..., o:]
    kr_slot = jnp.pad(kr, ((0, 0), (0, 0), (NOPE_DIM, LANE - QK_DIM)))
    w_br = jnp.concatenate([conv, cq, ckv, kr_slot, hg], axis=-1).astype(BF16)
    depth, d, _ = w.shape
    w_gate = gate.astype(BF16).reshape(depth, d, N_GATE_CHUNK, TN_IN).transpose(0, 2, 1, 3)
    return w_br, w_gate


def _head_slots(w, per_head, start, width):
    r = w.shape[0]
    wh = w.reshape(r, MLA_HEADS, per_head)[:, :, start:start + width]
    wh = jnp.pad(wh, ((0, 0), (0, 0), (0, HEAD_SLOT - width)))
    return wh.reshape(r, MLA_HEADS * HEAD_SLOT)


def _swap_rope_halves(a):
    half = ROPE_DIM // 2
    return jnp.concatenate([jnp.zeros_like(a[..., :NOPE_DIM]), a[..., NOPE_DIM + half:],
                            a[..., NOPE_DIM:NOPE_DIM + half]], axis=-1)


def _v_slots(w_ukv):
    r = w_ukv.shape[0]
    wv = w_ukv.reshape(r, MLA_HEADS // 2, 2, NOPE_DIM + V_DIM)[..., NOPE_DIM:]
    z = jnp.zeros_like(wv[:, :, 0])
    slots = jnp.stack([jnp.concatenate([wv[:, :, 0], z], axis=-1),
                       jnp.concatenate([z, wv[:, :, 1]], axis=-1)], axis=2)
    ones = jnp.ones((MLA_HEADS // 2, V_DIM), F32)
    zo = jnp.zeros_like(ones)
    vone = jnp.stack([jnp.concatenate([zo, ones], axis=-1),
                      jnp.concatenate([ones, zo], axis=-1)], axis=1)
    return slots.reshape(r, MLA_HEADS * HEAD_SLOT), vone.reshape(1, MLA_HEADS * HEAD_SLOT)


def _rope_tables(lp):
    half = ROPE_DIM // 2
    pos = jnp.arange(lp, dtype=F32)
    inv_freq = ROPE_BASE ** (-jnp.arange(half, dtype=F32) / half)
    ang = pos[:, None] * inv_freq[None, :]
    cos, sin = jnp.cos(ang), jnp.sin(ang)
    ones = jnp.ones((lp, NOPE_DIM), F32)
    z16 = jnp.zeros((lp, half), F32)
    z64 = jnp.zeros((lp, NOPE_DIM), F32)
    tail = jnp.zeros((lp, LANE - QK_DIM), F32)
    rc = jnp.concatenate([ones, cos, cos, tail], axis=1)
    rs1 = jnp.concatenate([z64, -sin, z16, tail], axis=1)
    rs2 = jnp.concatenate([z64, z16, sin, tail], axis=1)
    return rc, rs1, rs2


def kernel(x, meta, norm1_g, w_in, conv_w, conv_b, conv_ln_g, conv_ln_b, w_conv_out, q_a_norm_g, w_uq, kv_a_norm_g, w_ukv, q_norm_g, k_norm_g, w_attn_out, hgrn_lb_logits, hgrn_norm_g, w_hgrn_out, w_out, norm2_g, w_ff1, w_ff2):
    b, seq, d = x.shape
    depth = w_in.shape[0]
    l = seq + N_META
    lp = -(-l // TL) * TL
    assert lp % TM_MERGE0 == 0 and (b * lp) % TM_MERGE == 0 and lp - l < TL
    x2 = None

    rc, rs1, rs2 = _rope_tables(lp)
    p_lb = jax.nn.softmax(hgrn_lb_logits.astype(F32), axis=0)
    lower_bounds = jnp.cumsum(p_lb, axis=0) - p_lb[0:1]
    row = lambda a: a.astype(F32).reshape(1, -1)
    pad_qk = lambda g: jnp.pad(g.astype(F32), (0, LANE - QK_DIM)).reshape(1, LANE)
    w_a, w_b, w_c, w_o = (w.astype(BF16) for w in (w_conv_out, w_attn_out, w_hgrn_out, w_out))
    w_1, w_2 = w_ff1.astype(BF16), w_ff2.astype(BF16)
    w_br, w_gate = _regroup_w_in(w_in)

    for li in range(depth):
        wuv, vone = _v_slots(w_ukv[li])
        w_uq_heads = w_uq[li].reshape(Q_RANK, MLA_HEADS, QK_DIM)
        w_uq_swapped = _swap_rope_halves(w_uq_heads).reshape(Q_RANK, MLA_HEADS * QK_DIM)
        q_gain = q_norm_g[li].astype(F32) * (QK_DIM ** -0.5 * LOG2E)
        gates, ha, q, k, v, oc = _front(
            (x, meta) if li == 0 else x2.reshape(b, lp, d), row(norm1_g[li]), w_br, w_gate, li,
            jnp.repeat(conv_w[li].astype(F32), SUBLANE, axis=0),
            row(conv_b[li]), row(conv_ln_g[li]), row(conv_ln_b[li]),
            row(q_a_norm_g[li]), row(kv_a_norm_g[li]),
            _head_slots(w_uq[li], QK_DIM, 0, QK_DIM).astype(BF16),
            _head_slots(w_uq_swapped, QK_DIM, 0, QK_DIM).astype(BF16),
            _head_slots(w_ukv[li], NOPE_DIM + V_DIM, 0, NOPE_DIM).astype(BF16),
            wuv.astype(BF16),
            pad_qk(q_gain), pad_qk(_swap_rope_halves(q_gain)), pad_qk(k_norm_g[li]),
            rc, rs1, rs2, vone,
            (1.0 - lower_bounds[li]).reshape(HGRN_HEADS, 1, HGRN_DK),
            hgrn_norm_g[li].astype(F32).reshape(HGRN_HEADS, 1, HGRN_DV), lp)
        ob = _attention(q, k, v)
        flat = lambda a: a.reshape(b * lp, -1)
        branches = (flat(ha), flat(ob), flat(oc), gates.reshape(N_GATE_CHUNK, b * lp, TN_IN))
        if li == 0:
            x2 = _merge_first(*branches, x, meta.astype(x.dtype), w_a, w_b, w_c, w_o, li, lp)
        else:
            x2 = _merge(*branches, x2, w_a, w_b, w_c, w_o, li)
        if li + 1 < depth:
            x2 = _ffn(x2, row(norm2_g[li]), w_1, w_2, li)
    assert seq % TM_FF == 0
    return _ffn_final(x2, row(norm2_g[depth - 1]), w_1, w_2, depth - 1, b, lp, N_META, seq)
```

```python
import functools

import jax
import jax.numpy as jnp
import numpy as np
from jax import lax
from jax.experimental import pallas as pl
from jax.experimental.pallas import tpu as pltpu

F32 = jnp.float32
BF16 = jnp.bfloat16

D_MODEL = 1024
N_META = 16
EPS = 1e-6
GATE_CLAMP = 1.0 - 1e-6
CONV_DIM = 512
CONV_K = 31
MLA_HEADS = 8
Q_RANK = 256
KV_RANK = 128
NOPE_DIM = 64
ROPE_DIM = 32
V_DIM = 64
QK_DIM = NOPE_DIM + ROPE_DIM
ROPE_BASE = 10000.0
HGRN_HEADS = 4
HGRN_DK = 128
HGRN_DV = 128
D_FF = 4096

LANE = 128
HEAD_SLOT = LANE

COL_CONV = 3 * D_MODEL
COL_CQ = COL_CONV + 2 * CONV_DIM
COL_HGRN = COL_CQ + Q_RANK + KV_RANK + LANE
N_IN_PAD = COL_HGRN + 4 * HGRN_HEADS * HGRN_DK

VMEM_LIMIT = 52 * 1024 * 1024

TL = 384
TN_IN = 512
N_GATE_CHUNK = COL_CONV // TN_IN
SUBLANE = 8
RC_CONV = 16
CONV_HALO = 32
CONV_SPAN = TL + CONV_HALO - SUBLANE
RC_PREP = 128
TQ = 384
ATTN_HG = 4
LOG2E = 1.4426950408889634
HGRN_C = 128
HGRN_BLK = 4
TM_MERGE = 1024
TM_MERGE0 = 352
TM_FF = 1024
TF_FF = 1024


def _rms(x, eps=EPS):
    return x * lax.rsqrt(jnp.mean(x * x, axis=-1, keepdims=True) + eps)


def _sigmoid(x):
    return 0.5 * jnp.tanh(0.5 * x) + 0.5


def _silu(x):
    return x * _sigmoid(x)


def _conv_stage(glu, j, hbuf, sbuf):
    @pl.when(j == 0)
    def _():
        hbuf[0:CONV_HALO, :] = jnp.zeros((CONV_HALO, CONV_DIM), F32)

    @pl.when(j > 0)
    def _():
        hbuf[0:CONV_HALO, :] = hbuf[TL:TL + CONV_HALO, :]

    hbuf[CONV_HALO:CONV_HALO + TL, :] = glu

    for r in range(1, SUBLANE):
        sbuf[r - 1] = hbuf[r:r + CONV_SPAN, :]


def _conv_rows(r0, w_ref, cb_ref, lg_ref, lb_ref, o_ref, hbuf, sbuf):
    base = CONV_HALO - (CONV_K - 1)
    acc = jnp.zeros((RC_CONV, CONV_DIM), F32) + cb_ref[...]
    for k in range(CONV_K):
        r = (base + k) % SUBLANE
        lo = pl.multiple_of(r0 + (base + k - r), SUBLANE)
        src = hbuf if r == 0 else sbuf.at[r - 1]
        wk = jnp.tile(w_ref[k * SUBLANE:(k + 1) * SUBLANE, :], (RC_CONV // SUBLANE, 1))
        acc = acc + wk * src[pl.ds(lo, RC_CONV), :]
    mu = jnp.mean(acc, axis=-1, keepdims=True)
    xc = acc - mu
    y = xc * lax.rsqrt(jnp.mean(xc * xc, axis=-1, keepdims=True) + EPS)
    y = y * lg_ref[...] + lb_ref[...]
    o_ref[0, pl.ds(r0, RC_CONV), :] = (y * jax.nn.sigmoid(y)).astype(o_ref.dtype)


def _rope(x, c, s1, s2):
    half = ROPE_DIM // 2
    return x * c + pltpu.roll(x, LANE - half, 1) * s1 + pltpu.roll(x, half, 1) * s2


def _lane_sumsq(x, ones):
    sq = x * x
    hi = sq.astype(BF16)
    lo = (sq - hi.astype(F32)).astype(BF16)
    return jnp.dot(jnp.concatenate([hi, lo], axis=1), ones, preferred_element_type=F32)


def _mla_rows(cq, ckv, kr, rows, qag_ref, kvag_ref, wuq_ref, wuqs_ref, wuk_ref, wuv_ref,
              qng_ref, qngs_ref, kng_ref, rc_ref, rs1_ref, rs2_ref, vone_ref, q_out, k_out, v_out):
    inv_d = 1.0 / QK_DIM
    ones = jnp.ones((2 * LANE, LANE), BF16)
    cq = (_rms(cq) * qag_ref[...]).astype(BF16)
    q = jnp.dot(cq, wuq_ref[...], preferred_element_type=F32)
    qs = jnp.dot(cq, wuqs_ref[...], preferred_element_type=F32)
    ckv = (_rms(ckv) * kvag_ref[...]).astype(BF16)
    kn = jnp.dot(ckv, wuk_ref[...], preferred_element_type=F32)
    v = jnp.dot(ckv, wuv_ref[...], preferred_element_type=F32) + vone_ref[...]
    rc, rs1, rs2 = rc_ref[rows, :], rs1_ref[rows, :], rs2_ref[rows, :]
    q_c = qng_ref[...] * rc
    q_s = qngs_ref[...] * (rs1 + rs2)
    kr_ss = _lane_sumsq(kr, ones)
    kr_rot = _rope(kr * kng_ref[...], rc, rs1, rs2)
    for h in range(MLA_HEADS):
        sl = slice(h * HEAD_SLOT, (h + 1) * HEAD_SLOT)
        qh = q[:, sl]
        q_out[0, h, rows, :] = ((qh * q_c + qs[:, sl] * q_s) * lax.rsqrt(
            _lane_sumsq(qh, ones) * inv_d + EPS)).astype(q_out.dtype)
        kh = kn[:, sl]
        k_out[0, h, rows, :] = ((kh * kng_ref[...] + kr_rot) * lax.rsqrt(
            (_lane_sumsq(kh, ones) + kr_ss) * inv_d + EPS)).astype(k_out.dtype)
        v_out[0, h, rows, :] = v[:, sl].astype(v_out.dtype)


def _hgrn_head(qh, fr, iv, gv, oml, ng, lvl, st_ref, h):
    C = HGRN_C
    rowid = lax.broadcasted_iota(jnp.int32, (C, HGRN_DK), 0)
    nt = (((1,), (1,)), ((), ()))
    tn = (((0,), (0,)), ((), ()))
    kk = oml * jax.nn.sigmoid(-fr)
    lf = jnp.log1p(-jnp.minimum(kk, GATE_CLAMP)) * LOG2E
    v = _silu(iv).astype(BF16)

    fwd = lf
    bwd = jnp.zeros_like(lf)
    n, level = 1, 0
    a = None
    while n < C:
        if n >= HGRN_BLK:
            qs = (qh * jnp.exp2(fwd)).astype(BF16)
            if a is None:
                a = jnp.where(lvl == 0, lax.dot_general(
                    qs, (kk * jnp.exp2(-fwd)).astype(BF16), nt,
                    preferred_element_type=F32), 0.0)
            level += 1
            ks = (kk * jnp.exp2(bwd)).astype(BF16)
            a = jnp.where(lvl == level,
                          lax.dot_general(qs, ks, nt, preferred_element_type=F32), a)
        tot = fwd + bwd
        upper = (rowid & n) != 0
        fwd = fwd + jnp.where(upper, pltpu.roll(tot, n, 0), 0.0)
        bwd = bwd + jnp.where(upper, 0.0, pltpu.roll(tot, C - n, 0))
        n *= 2

    st = st_ref[h]
    o = jnp.dot(a.astype(BF16), v, preferred_element_type=F32)
    o = o + lax.dot_general((qh * jnp.exp2(fwd)).astype(BF16), st.astype(BF16), nt,
                            preferred_element_type=F32)
    ks = (kk * jnp.exp2(bwd)).astype(BF16)
    st_ref[h] = st * jnp.exp2(fwd[C - 1:C, :]) + lax.dot_general(
        v, ks, tn, preferred_element_type=F32)

    return _rms(o) * ng * _silu(gv)


def _hgrn_levels():
    t = np.arange(HGRN_C)[:, None]
    s = np.arange(HGRN_C)[None, :]
    x = (t ^ s) // HGRN_BLK
    lvl = np.where(x == 0, 0, np.floor(np.log2(np.maximum(x, 1))).astype(np.int64) + 1)
    return jnp.asarray(np.where(s > t, -1, lvl), dtype=jnp.int32)


def _window_start(j, tile, seq_len):
    return jnp.clip(j * tile - N_META, 0, seq_len - tile)


def _padded_rows(win, meta_ref, j, n_tiles, seq_len):
    tile, d = win.shape
    drop = tile * (n_tiles - 1) - N_META - (seq_len - tile)
    first = jnp.concatenate([meta_ref[...].astype(win.dtype), win[:tile - N_META]], axis=0)
    last = jnp.concatenate([win[drop:], jnp.zeros((drop, d), win.dtype)], axis=0)
    return jnp.where(j == 0, first, jnp.where(j == n_tiles - 1, last, win))


def _front_kernel(*refs, seq_len=None, n_tiles=None):
    if seq_len is None:
        x_ref, *rest = refs
        x_tile = x_ref[0]
    else:
        x_ref, meta_ref, *rest = refs
        x_tile = _padded_rows(x_ref[...], meta_ref, pl.program_id(1), n_tiles, seq_len)
    _front_tile(x_tile, *rest)


def _front_tile(x_tile, g1_ref, wbr_ref, wgate_ref,
                cw_ref, cb_ref, lg_ref, lb_ref,
                qag_ref, kvag_ref, wuq_ref, wuqs_ref, wuk_ref, wuv_ref, qng_ref, qngs_ref, kng_ref,
                rc_ref, rs1_ref, rs2_ref, vone_ref,
                oml_ref, ng_ref, lvl_ref,
                gate_out, ha_out, q_out, k_out, v_out, oc_out,
                h_scr, ym_scr, yh_scr, hbuf, sbuf, st_ref):
    j = pl.program_id(1)
    hw = HGRN_HEADS * HGRN_DK
    n_mla = COL_HGRN - COL_CQ

    @pl.when(j == 0)
    def _():
        st_ref[...] = jnp.zeros(st_ref.shape, F32)

    h = (_rms(x_tile) * g1_ref[...]).astype(BF16)
    h_scr[...] = h
    yc = jnp.dot(h, wbr_ref[:, 0:2 * CONV_DIM], preferred_element_type=F32)
    _conv_stage(yc[:, :CONV_DIM] * _sigmoid(yc[:, CONV_DIM:]), j, hbuf, sbuf)
    ym_scr[...] = jnp.dot(h, wbr_ref[:, 2 * CONV_DIM:2 * CONV_DIM + n_mla],
                          preferred_element_type=F32)
    yh = jnp.dot(h, wbr_ref[:, 2 * CONV_DIM + n_mla:], preferred_element_type=F32)
    for idx in range(4 * HGRN_HEADS):
        yh_scr[idx] = yh[:, idx * HGRN_DK:(idx + 1) * HGRN_DK]

    conv_per_gate = (TL // RC_CONV) // N_GATE_CHUNK

    def gate_and_conv(c, carry):
        gate_out[c, 0] = _sigmoid(jnp.dot(h_scr[...], wgate_ref[c],
                                          preferred_element_type=F32)).astype(gate_out.dtype)
        for t in range(conv_per_gate):
            r0 = pl.multiple_of((c * conv_per_gate + t) * RC_CONV, RC_CONV)
            _conv_rows(r0, cw_ref, cb_ref, lg_ref, lb_ref, ha_out, hbuf, sbuf)
        return carry

    for c in range(N_GATE_CHUNK):
        gate_and_conv(c, 0)

    def mla_chunk(c, carry):
        rows = pl.ds(pl.multiple_of(c * RC_PREP, RC_PREP), RC_PREP)
        _mla_rows(ym_scr[rows, 0:Q_RANK], ym_scr[rows, Q_RANK:Q_RANK + KV_RANK],
                  ym_scr[rows, Q_RANK + KV_RANK:], rows,
                  qag_ref, kvag_ref, wuq_ref, wuqs_ref, wuk_ref, wuv_ref,
                  qng_ref, qngs_ref, kng_ref, rc_ref, rs1_ref, rs2_ref, vone_ref, q_out, k_out, v_out)
        return carry

    for c in range(TL // RC_PREP):
        mla_chunk(c, 0)

    lvl = lvl_ref[...]

    def hgrn_tile(t, carry):
        for cc in range(TL // HGRN_C):
            rows = slice(cc * HGRN_C, (cc + 1) * HGRN_C)
            for hd in range(HGRN_HEADS):
                oc_out[0, rows, hd * HGRN_DV:(hd + 1) * HGRN_DV] = _hgrn_head(
                    yh_scr[hd, rows, :], yh_scr[HGRN_HEADS + hd, rows, :],
                    yh_scr[2 * HGRN_HEADS + hd, rows, :], yh_scr[3 * HGRN_HEADS + hd, rows, :],
                    oml_ref[hd], ng_ref[hd], lvl, st_ref, hd).astype(oc_out.dtype)
        return carry

    lax.fori_loop(0, jnp.minimum(j + 1, 1), hgrn_tile, 0)


def _x_window_specs(tile, seq_len, d, index_of):
    assert seq_len % SUBLANE == 0 and tile % SUBLANE == 0 and N_META % SUBLANE == 0
    return [
        pl.BlockSpec((pl.Element(tile), pl.Element(d)),
                     lambda *ids: (pl.multiple_of(
                         index_of(*ids)[0] * seq_len + _window_start(index_of(*ids)[1], tile, seq_len),
                         SUBLANE), 0)),
        pl.BlockSpec((N_META, d), lambda *ids: (0, 0)),
    ]


def _front(x_in, g1, w_br, w_gate, li, conv_w, conv_b, ln_g, ln_b, qag, kvag, wuq, wuqs, wuk, wuv,
           qng, qngs, kng, rc, rs1, rs2, vone, one_minus_lb, hgrn_ng, lp):
    first_layer = isinstance(x_in, tuple)
    b, seq_len, d = x_in[0].shape if first_layer else x_in.shape
    hw = MLA_HEADS * HEAD_SLOT
    gw = HGRN_HEADS * HGRN_DK
    n_br = N_IN_PAD - COL_CONV
    const = lambda shape: pl.BlockSpec(shape, lambda bb, j: (0,) * len(shape))
    resident = lambda shape: pl.BlockSpec((None,) + shape, lambda bb, j: (li,) + (0,) * len(shape),
                                          pipeline_mode=pl.Buffered(1))
    tab = pl.BlockSpec((TL, LANE), lambda bb, j: (j, 0))
    seq = lambda w: pl.BlockSpec((1, TL, w), lambda bb, j: (bb, j, 0))
    heads = pl.BlockSpec((1, MLA_HEADS, TL, HEAD_SLOT), lambda bb, j: (bb, 0, j, 0))
    qkv = jax.ShapeDtypeStruct((b, MLA_HEADS, lp, HEAD_SLOT), BF16)
    if first_layer:
        x_args = (x_in[0].reshape(b * seq_len, d), x_in[1])
        x_specs = _x_window_specs(TL, seq_len, d, lambda bb, j: (bb, j))
        body = functools.partial(_front_kernel, seq_len=seq_len, n_tiles=lp // TL)
    else:
        x_args, x_specs, body = (x_in,), [seq(d)], _front_kernel
    return pl.pallas_call(
        body,
        out_shape=(jax.ShapeDtypeStruct((N_GATE_CHUNK, b, lp, TN_IN), BF16),
                   jax.ShapeDtypeStruct((b, lp, CONV_DIM), BF16),
                   qkv, qkv, qkv,
                   jax.ShapeDtypeStruct((b, lp, gw), BF16)),
        grid=(b, lp // TL),
        in_specs=x_specs + [
            const((1, d)), resident((d, n_br)), resident((N_GATE_CHUNK, d, TN_IN)),
            const((CONV_K * SUBLANE, CONV_DIM)), const((1, CONV_DIM)), const((1, CONV_DIM)),
            const((1, CONV_DIM)),
            const((1, Q_RANK)), const((1, KV_RANK)),
            const((Q_RANK, hw)), const((Q_RANK, hw)), const((KV_RANK, hw)), const((KV_RANK, hw)),
            const((1, LANE)), const((1, LANE)), const((1, LANE)),
            tab, tab, tab, const((1, hw)),
            const((HGRN_HEADS, 1, HGRN_DK)), const((HGRN_HEADS, 1, HGRN_DV)), const((HGRN_C, HGRN_C)),
        ],
        out_specs=(pl.BlockSpec((N_GATE_CHUNK, 1, TL, TN_IN), lambda bb, j: (0, bb, j, 0)),
                   seq(CONV_DIM), heads, heads, heads, seq(gw)),
        scratch_shapes=[pltpu.VMEM((TL, d), BF16),
                        pltpu.VMEM((TL, COL_HGRN - COL_CQ), F32),
                        pltpu.VMEM((4 * HGRN_HEADS, TL, HGRN_DK), F32),
                        pltpu.VMEM((CONV_HALO + TL, CONV_DIM), F32),
                        pltpu.VMEM((SUBLANE - 1, CONV_SPAN, CONV_DIM), F32),
                        pltpu.VMEM((HGRN_HEADS, HGRN_DV, HGRN_DK), F32)],
        compiler_params=pltpu.CompilerParams(
            dimension_semantics=("parallel", "arbitrary"), vmem_limit_bytes=VMEM_LIMIT),
    )(*x_args, g1, w_br, w_gate, conv_w, conv_b, ln_g, ln_b, qag, kvag, wuq, wuqs, wuk, wuv,
      qng, qngs, kng, rc, rs1, rs2, vone, one_minus_lb, hgrn_ng, _hgrn_levels())


def _attn_kernel(q_ref, k_ref, v_ref, o_ref, s_ref, m_ref, acc_ref):
    i = pl.program_id(2)
    nt = (((1,), (1,)), ((), ()))
    heads = range(ATTN_HG)
    qs = [q_ref[0, h] for h in heads]
    odd = (i & 1) == 1

    def scores(h, tile0, ntiles):
        kt = k_ref[0, h, pl.ds(pl.multiple_of(tile0 * TQ, TQ), ntiles * TQ), :]
        return lax.dot_general(qs[h], kt, nt, preferred_element_type=F32)

    def fold(s):
        m = s[:, 0:LANE]
        for c in range(1, s.shape[1] // LANE):
            m = jnp.maximum(m, s[:, c * LANE:(c + 1) * LANE])
        return m

    def score_span(tile0, ntiles, m_in):
        out = []
        for h in heads:
            s = scores(h, tile0, ntiles)
            for t in range(ntiles):
                s_ref[h, tile0 + t] = s[:, t * TQ:(t + 1) * TQ]
            out.append(jnp.maximum(m_in[h], fold(s)))
        return tuple(out)

    def pv_span(tile0, ntiles):
        out = []
        for h in heads:
            s = jnp.concatenate([s_ref[h, tile0 + t] for t in range(ntiles)], axis=1)
            p = jnp.exp2(s - ms[h]).astype(BF16)
            vt = v_ref[0, h, pl.ds(pl.multiple_of(tile0 * TQ, TQ), ntiles * TQ), :]
            out.append(jnp.dot(p, vt, preferred_element_type=F32))
        return out

    quads = lax.shift_right_logical(i, 2)
    mloc = lax.fori_loop(0, quads, lambda t, m: score_span(4 * t, 4, m),
                         tuple(jnp.full((TQ, LANE), -1e30, F32) for _ in heads))
    for h in heads:
        m_ref[h] = mloc[h]

    @pl.when((i & 2) != 0)
    def _():
        m_new = score_span(4 * quads, 2, [m_ref[h] for h in heads])
        for h in heads:
            m_ref[h] = m_new[h]

    keep = (lax.broadcasted_iota(jnp.int32, (TQ, TQ), 0)
            >= lax.broadcasted_iota(jnp.int32, (TQ, TQ), 1))

    @pl.when(odd)
    def _():
        for h in heads:
            s = scores(h, i - 1, 2)
            lo, hi = s[:, :TQ], jnp.where(keep, s[:, TQ:], -1e30)
            s_ref[h, i - 1] = lo
            s_ref[h, i] = hi
            m_ref[h] = jnp.maximum(m_ref[h], jnp.maximum(fold(lo), fold(hi)))

    @pl.when(jnp.logical_not(odd))
    def _():
        for h in heads:
            s = jnp.where(keep, scores(h, i, 1), -1e30)
            s_ref[h, i] = s
            m_ref[h] = jnp.maximum(m_ref[h], fold(s))

    ms = [jnp.max(m_ref[h], axis=-1, keepdims=True) for h in heads]

    quads2 = lax.shift_right_logical(i + 1, 2)

    def pass2(t, acc):
        part = pv_span(4 * t, 4)
        return tuple(acc[h] + part[h] for h in heads)

    acc = lax.fori_loop(0, quads2, pass2, tuple(jnp.zeros((TQ, HEAD_SLOT), F32) for _ in heads))
    for h in heads:
        acc_ref[h] = acc[h]

    @pl.when(((i + 1) & 2) != 0)
    def _():
        part = pv_span(4 * quads2, 2)
        for h in heads:
            acc_ref[h] += part[h]

    @pl.when(jnp.logical_not(odd))
    def _():
        part = pv_span(i, 1)
        for h in heads:
            acc_ref[h] += part[h]

    first = lax.broadcasted_iota(jnp.int32, (TQ, HEAD_SLOT), 1) < V_DIM
    for hp in range(ATTN_HG // 2):
        a, b = acc_ref[2 * hp], acc_ref[2 * hp + 1]
        den = pltpu.roll(jnp.where(first, b, a), V_DIM, 1)
        o_ref[0, :, hp * HEAD_SLOT:(hp + 1) * HEAD_SLOT] = (
            jnp.where(first, a, b) / den).astype(o_ref.dtype)


def _attention(q, k, v):
    b, nh, lp, hs = q.shape
    nq = lp // TQ
    return pl.pallas_call(
        _attn_kernel,
        out_shape=jax.ShapeDtypeStruct((b, lp, nh * V_DIM), BF16),
        grid=(b, nh // ATTN_HG, nq),
        in_specs=[
            pl.BlockSpec((1, ATTN_HG, TQ, hs), lambda bb, g, i: (bb, g, i, 0)),
            pl.BlockSpec((1, ATTN_HG, lp, hs), lambda bb, g, i: (bb, g, 0, 0)),
            pl.BlockSpec((1, ATTN_HG, lp, hs), lambda bb, g, i: (bb, g, 0, 0)),
        ],
        out_specs=pl.BlockSpec((1, TQ, ATTN_HG * V_DIM), lambda bb, g, i: (bb, i, g)),
        scratch_shapes=[pltpu.VMEM((ATTN_HG, nq, TQ, TQ), F32),
                        pltpu.VMEM((ATTN_HG, TQ, LANE), F32),
                        pltpu.VMEM((ATTN_HG, TQ, HEAD_SLOT), F32)],
        compiler_params=pltpu.CompilerParams(
            dimension_semantics=("parallel", "parallel", "arbitrary"), vmem_limit_bytes=VMEM_LIMIT),
    )(q, k, v)


def _merge_kernel(ha_ref, ob_ref, oc_ref, gate_ref, *refs, seq_len=None, n_tiles=None):
    if seq_len is None:
        x_ref, wa_ref, wb_ref, wc_ref, wo_ref, o_ref = refs
        x_tile = x_ref[...]
    else:
        x_ref, meta_ref, wa_ref, wb_ref, wc_ref, wo_ref, g2_ref, w1_ref, w2_ref, o_ref = refs
        x_tile = _padded_rows(x_ref[...], meta_ref, pl.program_id(1), n_tiles, seq_len)
    mix = None
    for n, (br_ref, w_ref) in enumerate(((ha_ref, wa_ref), (ob_ref, wb_ref), (oc_ref, wc_ref))):
        y = jnp.dot(br_ref[...], w_ref[...], preferred_element_type=F32)
        per = D_MODEL // TN_IN
        gate = jnp.concatenate([gate_ref[n * per + t] for t in range(per)], axis=1)
        y = y * gate.astype(F32)
        mix = y if mix is None else mix + y
    x_new = x_tile + jnp.dot(mix.astype(BF16), wo_ref[...], preferred_element_type=F32)
    o_ref[...] = x_new if seq_len is None else _ffn_rows(x_new, g2_ref, w1_ref, w2_ref)


def _merge(ha, ob, oc, gates, x2, wa, wb, wc, wo, li):
    t = x2.shape[0]
    br = lambda: pl.BlockSpec((TM_MERGE, 512), lambda i: (i, 0))
    wbr = lambda: pl.BlockSpec((None, 512, D_MODEL), lambda i: (li, 0, 0))
    return pl.pallas_call(
        _merge_kernel,
        out_shape=jax.ShapeDtypeStruct((t, D_MODEL), F32),
        grid=(t // TM_MERGE,),
        in_specs=[br(), br(), br(),
                  pl.BlockSpec((N_GATE_CHUNK, TM_MERGE, TN_IN), lambda i: (0, i, 0)),
                  pl.BlockSpec((TM_MERGE, D_MODEL), lambda i: (i, 0)),
                  wbr(), wbr(), wbr(),
                  pl.BlockSpec((None, D_MODEL, D_MODEL), lambda i: (li, 0, 0))],
        out_specs=pl.BlockSpec((TM_MERGE, D_MODEL), lambda i: (i, 0)),
        compiler_params=pltpu.CompilerParams(
            dimension_semantics=("parallel",), vmem_limit_bytes=VMEM_LIMIT),
    )(ha, ob, oc, gates, x2, wa, wb, wc, wo)


def _merge_first(ha, ob, oc, gates, x, meta, wa, wb, wc, wo, g2, w1, w2, li, lp):
    b, seq_len, d = x.shape
    n_tiles = lp // TM_MERGE0
    rows = lambda bb, j: bb * n_tiles + j
    br = lambda: pl.BlockSpec((TM_MERGE0, 512), lambda bb, j: (rows(bb, j), 0))
    wbr = lambda: pl.BlockSpec((None, 512, D_MODEL), lambda bb, j: (li, 0, 0),
                               pipeline_mode=pl.Buffered(1))
    return pl.pallas_call(
        functools.partial(_merge_kernel, seq_len=seq_len, n_tiles=n_tiles),
        out_shape=jax.ShapeDtypeStruct((b * lp, D_MODEL), F32),
        grid=(b, n_tiles),
        in_specs=[br(), br(), br(),
                  pl.BlockSpec((N_GATE_CHUNK, TM_MERGE0, TN_IN), lambda bb, j: (0, rows(bb, j), 0))]
        + _x_window_specs(TM_MERGE0, seq_len, d, lambda bb, j: (bb, j))
        + [wbr(), wbr(), wbr(),
           pl.BlockSpec((None, D_MODEL, D_MODEL), lambda bb, j: (li, 0, 0),
                        pipeline_mode=pl.Buffered(1)),
           pl.BlockSpec((1, D_MODEL), lambda bb, j: (0, 0))] + _ffn_weight_specs(li),
        out_specs=pl.BlockSpec((TM_MERGE0, D_MODEL), lambda bb, j: (rows(bb, j), 0)),
        compiler_params=pltpu.CompilerParams(
            dimension_semantics=("parallel", "parallel"), vmem_limit_bytes=VMEM_LIMIT),
    )(ha, ob, oc, gates, x.reshape(b * seq_len, d), meta, wa, wb, wc, wo, g2, w1, w2)


def _ffn_rows(x, g_ref, w1_ref, w2_ref):
    h = (_rms(x) * g_ref[...]).astype(BF16)
    acc = x
    for c in range(D_FF // TF_FF):
        cols = slice(c * TF_FF, (c + 1) * TF_FF)
        a = jnp.maximum(jnp.dot(h, w1_ref[:, cols], preferred_element_type=F32), 0.0)
        acc = acc + jnp.dot((a * a).astype(BF16), w2_ref[cols, :], preferred_element_type=F32)
    return acc


def _ffn_kernel(x_ref, g_ref, w1_ref, w2_ref, o_ref):
    o_ref[...] = _ffn_rows(x_ref[...], g_ref, w1_ref, w2_ref)


def _ffn_weight_specs(li):
    return [
        pl.BlockSpec((None, D_MODEL, D_FF), lambda *_: (li, 0, 0), pipeline_mode=pl.Buffered(1)),
        pl.BlockSpec((None, D_FF, D_MODEL), lambda *_: (li, 0, 0), pipeline_mode=pl.Buffered(1)),
    ]


def _ffn(x2, g, w1, w2, li):
    t = x2.shape[0]
    return pl.pallas_call(
        _ffn_kernel,
        out_shape=jax.ShapeDtypeStruct((t, D_MODEL), F32),
        grid=(t // TM_FF,),
        in_specs=[
            pl.BlockSpec((TM_FF, D_MODEL), lambda i: (i, 0)),
            pl.BlockSpec((1, D_MODEL), lambda i: (0, 0)),
        ] + _ffn_weight_specs(li),
        out_specs=pl.BlockSpec((TM_FF, D_MODEL), lambda i: (i, 0)),
        compiler_params=pltpu.CompilerParams(
            dimension_semantics=("parallel",), vmem_limit_bytes=VMEM_LIMIT),
    )(x2, g, w1, w2)


def _ffn_final(x2, g, w1, w2, li, b, lp, n_lead, n_out):
    assert lp % SUBLANE == 0 and n_lead % SUBLANE == 0
    return pl.pallas_call(
        _ffn_kernel,
        out_shape=jax.ShapeDtypeStruct((b, n_out, D_MODEL), F32),
        grid=(b, n_out // TM_FF),
        in_specs=[
            pl.BlockSpec((pl.Element(TM_FF), pl.Element(D_MODEL)),
                         lambda bb, t: (pl.multiple_of(bb * lp + n_lead + t * TM_FF, SUBLANE), 0)),
            pl.BlockSpec((1, D_MODEL), lambda bb, t: (0, 0)),
        ] + _ffn_weight_specs(li),
        out_specs=pl.BlockSpec((None, TM_FF, D_MODEL), lambda bb, t: (bb, t, 0)),
        compiler_params=pltpu.CompilerParams(
            dimension_semantics=("parallel", "parallel"), vmem_limit_bytes=VMEM_LIMIT),
    )(x2, g, w1, w2)


def _regroup_w_in(w):
    o = 0
    conv = w[..., o:o + 2 * CONV_DIM]; o += 2 * CONV_DIM
    cq = w[..., o:o + Q_RANK]; o += Q_RANK
    ckv = w[..., o:o + KV_RANK]; o += KV_RANK
    kr = w[..., o:o + ROPE_DIM]; o += ROPE_DIM
    hg = w[..., o:o + 4 * HGRN_HEADS * HGRN_DK]; o += 4 * HGRN_HEADS * HGRN_DK
    gate = w[..., o:]
    kr_slot = jnp.pad(kr, ((0, 0), (0, 0), (NOPE_DIM, LANE - QK_DIM)))
    w_br = jnp.concatenate([conv, cq, ckv, kr_slot, hg], axis=-1).astype(BF16)
    depth, d, _ = w.shape
    w_gate = gate.astype(BF16).reshape(depth, d, N_GATE_CHUNK, TN_IN).transpose(0, 2, 1, 3)
    return w_br, w_gate


def _head_slots(w, per_head, start, width):
    r = w.shape[0]
    wh = w.reshape(r, MLA_HEADS, per_head)[:, :, start:start + width]
    wh = jnp.pad(wh, ((0, 0), (0, 0), (0, HEAD_SLOT - width)))
    return wh.reshape(r, MLA_HEADS * HEAD_SLOT)


def _swap_rope_halves(a):
    half = ROPE_DIM // 2
    return jnp.concatenate([jnp.zeros_like(a[..., :NOPE_DIM]), a[..., NOPE_DIM + half:],
                            a[..., NOPE_DIM:NOPE_DIM + half]], axis=-1)


def _v_slots(w_ukv):
    r = w_ukv.shape[0]
    wv = w_ukv.reshape(r, MLA_HEADS // 2, 2, NOPE_DIM + V_DIM)[..., NOPE_DIM:]
    z = jnp.zeros_like(wv[:, :, 0])
    slots = jnp.stack([jnp.concatenate([wv[:, :, 0], z], axis=-1),
                       jnp.concatenate([z, wv[:, :, 1]], axis=-1)], axis=2)
    ones = jnp.ones((MLA_HEADS // 2, V_DIM), F32)
    zo = jnp.zeros_like(ones)
    vone = jnp.stack([jnp.concatenate([zo, ones], axis=-1),
                      jnp.concatenate([ones, zo], axis=-1)], axis=1)
    return slots.reshape(r, MLA_HEADS * HEAD_SLOT), vone.reshape(1, MLA_HEADS * HEAD_SLOT)


def _rope_tables(lp):
    half = ROPE_DIM // 2
    pos = jnp.arange(lp, dtype=F32)
    inv_freq = ROPE_BASE ** (-jnp.arange(half, dtype=F32) / half)
    ang = pos[:, None] * inv_freq[None, :]
    cos, sin = jnp.cos(ang), jnp.sin(ang)
    ones = jnp.ones((lp, NOPE_DIM), F32)
    z16 = jnp.zeros((lp, half), F32)
    z64 = jnp.zeros((lp, NOPE_DIM), F32)
    tail = jnp.zeros((lp, LANE - QK_DIM), F32)
    rc = jnp.concatenate([ones, cos, cos, tail], axis=1)
    rs1 = jnp.concatenate([z64, -sin, z16, tail], axis=1)
    rs2 = jnp.concatenate([z64, z16, sin, tail], axis=1)
    return rc, rs1, rs2


def kernel(x, meta, norm1_g, w_in, conv_w, conv_b, conv_ln_g, conv_ln_b, w_conv_out, q_a_norm_g, w_uq, kv_a_norm_g, w_ukv, q_norm_g, k_norm_g, w_attn_out, hgrn_lb_logits, hgrn_norm_g, w_hgrn_out, w_out, norm2_g, w_ff1, w_ff2):
    b, seq, d = x.shape
    depth = w_in.shape[0]
    l = seq + N_META
    lp = -(-l // TL) * TL
    assert lp % TM_MERGE0 == 0 and (b * lp) % TM_MERGE == 0 and lp - l < TL
    x2 = None

    rc, rs1, rs2 = _rope_tables(lp)
    p_lb = jax.nn.softmax(hgrn_lb_logits.astype(F32), axis=0)
    lower_bounds = jnp.cumsum(p_lb, axis=0) - p_lb[0:1]
    row = lambda a: a.astype(F32).reshape(1, -1)
    pad_qk = lambda g: jnp.pad(g.astype(F32), (0, LANE - QK_DIM)).reshape(1, LANE)
    w_a, w_b, w_c, w_o = (w.astype(BF16) for w in (w_conv_out, w_attn_out, w_hgrn_out, w_out))
    w_1, w_2 = w_ff1.astype(BF16), w_ff2.astype(BF16)
    w_br, w_gate = _regroup_w_in(w_in)

    for li in range(depth):
        wuv, vone = _v_slots(w_ukv[li])
        w_uq_heads = w_uq[li].reshape(Q_RANK, MLA_HEADS, QK_DIM)
        w_uq_swapped = _swap_rope_halves(w_uq_heads).reshape(Q_RANK, MLA_HEADS * QK_DIM)
        q_gain = q_norm_g[li].astype(F32) * (QK_DIM ** -0.5 * LOG2E)
        gates, ha, q, k, v, oc = _front(
            (x, meta) if li == 0 else x2.reshape(b, lp, d), row(norm1_g[li]), w_br, w_gate, li,
            jnp.repeat(conv_w[li].astype(F32), SUBLANE, axis=0),
            row(conv_b[li]), row(conv_ln_g[li]), row(conv_ln_b[li]),
            row(q_a_norm_g[li]), row(kv_a_norm_g[li]),
            _head_slots(w_uq[li], QK_DIM, 0, QK_DIM).astype(BF16),
            _head_slots(w_uq_swapped, QK_DIM, 0, QK_DIM).astype(BF16),
            _head_slots(w_ukv[li], NOPE_DIM + V_DIM, 0, NOPE_DIM).astype(BF16),
            wuv.astype(BF16),
            pad_qk(q_gain), pad_qk(_swap_rope_halves(q_gain)), pad_qk(k_norm_g[li]),
            rc, rs1, rs2, vone,
            (1.0 - lower_bounds[li]).reshape(HGRN_HEADS, 1, HGRN_DK),
            hgrn_norm_g[li].astype(F32).reshape(HGRN_HEADS, 1, HGRN_DV), lp)
        ob = _attention(q, k, v)
        flat = lambda a: a.reshape(b * lp, -1)
        branches = (flat(ha), flat(ob), flat(oc), gates.reshape(N_GATE_CHUNK, b * lp, TN_IN))
        if li == 0:
            assert depth >= 2
            x2 = _merge_first(*branches, x, meta.astype(x.dtype), w_a, w_b, w_c, w_o,
                              row(norm2_g[li]), w_1, w_2, li, lp)
        else:
            x2 = _merge(*branches, x2, w_a, w_b, w_c, w_o, li)
            if li + 1 < depth:
                x2 = _ffn(x2, row(norm2_g[li]), w_1, w_2, li)
    assert seq % TM_FF == 0
    return _ffn_final(x2, row(norm2_g[depth - 1]), w_1, w_2, depth - 1, b, lp, N_META, seq)
```
